```python
import math
import jax, jax.numpy as jnp
from jax import lax
import numpy as np

D_MODEL = 2048
BATCH = 4
SEQ = 2048
DEPTH = 1
DEC_BATCH = 128
DEC_SEQ = 1
PAST_LEN = 16384
PAGE_SIZE = 128

D_MIX = D_MODEL
D_A = D_MIX // 2
D_CONV = D_MIX - D_A
N_HEADS_A = 4
HEAD_V = D_A // N_HEADS_A
D_K = D_A // 2
HEAD_K = D_K // N_HEADS_A
GATE_RANK = 16
GATE_NORM = 16.0
CHUNK = 16
CONV_W = 31
CONV_GROUPS = 8
D_FF = ((8 * D_MODEL // 3 + 255) // 256) * 256
IN_SIZES = (D_K, D_K, D_A, GATE_RANK, D_A, 2 * D_CONV)
IN_COLS = sum(IN_SIZES)
IN_SPLITS = tuple(int(s) for s in np.cumsum(IN_SIZES)[:-1])
RMS_EPS = 1e-6
LN_EPS = 1e-5

kernel_name = "hymba_gla_conformerconv_decode_step"


def rmsnorm(x, g):
    xf = x.astype(jnp.float32)
    y = xf * lax.rsqrt(jnp.mean(xf * xf, axis=-1, keepdims=True) + RMS_EPS)
    return (y * g.astype(jnp.float32)).astype(x.dtype)


def layernorm(x, g, b):
    xf = x.astype(jnp.float32)
    mu = jnp.mean(xf, axis=-1, keepdims=True)
    xc = xf - mu
    y = xc * lax.rsqrt(jnp.mean(xc * xc, axis=-1, keepdims=True) + LN_EPS)
    return (y * g.astype(jnp.float32) + b.astype(jnp.float32)).astype(x.dtype)


def gla_recurrence(q, k, v, log_a, s0):
    B, T, H, DK = q.shape
    DV = v.shape[-1]
    n = -(-T // CHUNK)
    pad = n * CHUNK - T
    padf = lambda a: jnp.pad(a.astype(jnp.float32), ((0, 0), (0, pad), (0, 0), (0, 0)))
    to_chunks = lambda a: padf(a).reshape(B, n, CHUNK, H, a.shape[-1]).transpose(1, 0, 3, 2, 4)
    qc, kc, vc, gc = to_chunks(q), to_chunks(k), to_chunks(v), to_chunks(log_a)
    causal = jnp.tril(jnp.ones((CHUNK, CHUNK), dtype=bool))[:, :, None]

    def step(S, inp):
        qi, ki, vi, gi = inp
        b = jnp.cumsum(gi, axis=-2)
        diff = b[..., :, None, :] - b[..., None, :, :]
        decay = jnp.exp(jnp.where(causal, diff, -jnp.inf))
        A = jnp.einsum('bhid,bhjd,bhijd->bhij', qi, ki, decay)
        o = jnp.einsum('bhij,bhjv->bhiv', A, vi) + jnp.einsum('bhid,bhdv->bhiv', qi * jnp.exp(b), S)
        b_last = b[..., -1:, :]
        S = jnp.exp(b_last[..., 0, :])[..., None] * S + jnp.einsum(
            'bhjd,bhjv->bhdv', ki * jnp.exp(b_last - b), vi)
        return S, o

    S, o = lax.scan(step, s0.astype(jnp.float32), (qc, kc, vc, gc))
    o = o.transpose(1, 0, 3, 2, 4).reshape(B, n * CHUNK, H, DV)[:, :T]
    return o, S


def causal_depthwise_conv(u, buf, w, b):
    full = jnp.concatenate([buf.astype(u.dtype), u], axis=1)
    y = lax.conv_general_dilated(
        full, w.astype(u.dtype)[:, None, :], window_strides=(1,), padding='VALID',
        dimension_numbers=('NWC', 'WIO', 'NWC'), feature_group_count=u.shape[-1])
    return y + b.astype(u.dtype), full[:, -(CONV_W - 1):]


def hybrid_layer(x, s_gla, s_conv, norm_mix, w_in, w_gate_up, b_gate, gla_norm,
                 conv_w, conv_b, conv_ln_g, conv_ln_b, w_out, norm_ffn, w_ffn_in, w_ffn_out):
    B, T, _ = x.shape
    h = rmsnorm(x, norm_mix)
    z = h @ w_in
    q, k, v, g_lr, g_out, u = jnp.split(z, IN_SPLITS, axis=-1)
    q = q.reshape(B, T, N_HEADS_A, HEAD_K) * (HEAD_K ** -0.5)
    k = k.reshape(B, T, N_HEADS_A, HEAD_K)
    v = v.reshape(B, T, N_HEADS_A, HEAD_V)
    gate_logit = (g_lr @ w_gate_up + b_gate).astype(jnp.float32)
    log_a = (jax.nn.log_sigmoid(gate_logit) / GATE_NORM).reshape(B, T, N_HEADS_A, HEAD_K)
    o, S_new = gla_recurrence(q, k, v, log_a, s_gla)
    o = rmsnorm(o, gla_norm) * jax.nn.silu(g_out.reshape(B, T, N_HEADS_A, HEAD_V).astype(jnp.float32))
    o_a = o.reshape(B, T, D_A).astype(x.dtype)
    ua, ug = jnp.split(u, 2, axis=-1)
    glu = ua * jax.nn.sigmoid(ug)
    c, conv_buf_new = causal_depthwise_conv(glu, s_conv, conv_w, conv_b)
    c = jax.nn.silu(layernorm(c, conv_ln_g, conv_ln_b))
    x = x + jnp.concatenate([o_a, c], axis=-1) @ w_out
    hf = rmsnorm(x, norm_ffn) @ w_ffn_in
    f_gate, f_up = jnp.split(hf, 2, axis=-1)
    x = x + (jax.nn.silu(f_gate) * f_up) @ w_ffn_out
    return x, S_new.astype(x.dtype), conv_buf_new.astype(x.dtype)


def setup_inputs(seed: int = 0) -> dict:
    key = jax.random.key(seed)
    ks = jax.random.split(key, 20)
    nrm = lambda k, shape, s: jax.random.normal(k, shape, jnp.float32) * s
    return {
        "x_prompt": nrm(ks[0], (BATCH, SEQ, D_MODEL), 1.0),
        "x_sample": nrm(ks[1], (DEC_BATCH, DEC_SEQ, D_MODEL), 1.0),
        "state_gla": nrm(ks[2], (DEPTH, DEC_BATCH, N_HEADS_A, HEAD_K, HEAD_V), 0.5),
        "state_conv": nrm(ks[3], (DEPTH, DEC_BATCH, CONV_W - 1, D_CONV), 0.5),
        "norm_mix": 1.0 + nrm(ks[4], (DEPTH, D_MODEL), 0.02),
        "w_in": nrm(ks[5], (DEPTH, D_MODEL, IN_COLS), D_MODEL ** -0.5),
        "w_gate_up": nrm(ks[6], (DEPTH, GATE_RANK, D_K), GATE_RANK ** -0.5),
        "b_gate": nrm(ks[7], (DEPTH, D_K), 0.1),
        "gla_norm": 1.0 + nrm(ks[8], (DEPTH, HEAD_V), 0.02),
        "conv_w": nrm(ks[9], (DEPTH, CONV_W, D_CONV), CONV_W ** -0.5),
        "conv_b": nrm(ks[10], (DEPTH, D_CONV), 0.01),
        "conv_ln_g": 1.0 + nrm(ks[11], (DEPTH, D_CONV), 0.02),
        "conv_ln_b": nrm(ks[12], (DEPTH, D_CONV), 0.01),
        "w_out": nrm(ks[13], (DEPTH, D_MIX, D_MODEL), D_MIX ** -0.5),
        "norm_ffn": 1.0 + nrm(ks[14], (DEPTH, D_MODEL), 0.02),
        "w_ffn_in": nrm(ks[15], (DEPTH, D_MODEL, 2 * D_FF), D_MODEL ** -0.5),
        "w_ffn_out": nrm(ks[16], (DEPTH, D_FF, D_MODEL), D_FF ** -0.5),
        "norm_final": 1.0 + nrm(ks[17], (D_MODEL,), 0.02),
    }


def reference(x_prompt, x_sample, state_gla, state_conv, norm_mix, w_in, w_gate_up, b_gate,
              gla_norm, conv_w, conv_b, conv_ln_g, conv_ln_b, w_out, norm_ffn, w_ffn_in,
              w_ffn_out, norm_final):
    yp, ys = x_prompt, x_sample
    bp = x_prompt.shape[0]
    gla_p, conv_p, gla_s, conv_s = [], [], [], []
    for l in range(DEPTH):
        params = (norm_mix[l], w_in[l], w_gate_up[l], b_gate[l], gla_norm[l], conv_w[l], conv_b[l],
                  conv_ln_g[l], conv_ln_b[l], w_out[l], norm_ffn[l], w_ffn_in[l], w_ffn_out[l])
        s0 = jnp.zeros((bp, N_HEADS_A, HEAD_K, HEAD_V), jnp.float32)
        c0 = jnp.zeros((bp, CONV_W - 1, D_CONV), x_prompt.dtype)
        yp, sp, cp = hybrid_layer(yp, s0, c0, *params)
        ys, ss, cs = hybrid_layer(ys, state_gla[l], state_conv[l], *params)
        gla_p.append(sp)
        conv_p.append(cp)
        gla_s.append(ss)
        conv_s.append(cs)
    yp = rmsnorm(yp, norm_final)
    ys = rmsnorm(ys, norm_final)
    return (yp, ys, jnp.stack(gla_p), jnp.stack(conv_p), jnp.stack(gla_s), jnp.stack(conv_s))
```

```python
import functools

import numpy as np
import jax
import jax.numpy as jnp
from jax import lax
from jax.experimental import pallas as pl
from jax.experimental.pallas import tpu as pltpu

F32 = jnp.float32
BF16 = jnp.bfloat16

RMS_EPS = 1e-6
LN_EPS = 1e-5
GATE_NORM = 16.0
N_HEADS = 4
HEAD_K = 128
HEAD_V = 256
D_K = N_HEADS * HEAD_K
D_A = N_HEADS * HEAD_V
GATE_RANK = 16
CONV_W = 31

LANES = 128
GLA_CHUNK = 128
VMEM_LIMIT = 58 * 1024 * 1024


def _cparams(sem):
    return pltpu.CompilerParams(dimension_semantics=sem, vmem_limit_bytes=VMEM_LIMIT)


def _dot(a, b):
    return jnp.dot(a, b, preferred_element_type=F32)


def _dot_nt(a, b):
    return lax.dot_general(a, b, (((1,), (1,)), ((), ())), preferred_element_type=F32)


def _rms_rows(x, gain):
    ms = jnp.mean(x * x, axis=-1, keepdims=True)
    return x * lax.rsqrt(ms + RMS_EPS) * gain


def _log_sigmoid(x):
    return jnp.minimum(x, 0.0) - jnp.log(1.0 + jnp.exp(-jnp.abs(x)))


def _norm_to_scratch(h_ref, xp_ref, xs_ref, gain, tmp, tms, rows_per_pass=256):
    for r0 in range(0, tmp, rows_per_pass):
        h_ref[r0:r0 + rows_per_pass, :] = _rms_rows(xp_ref[r0:r0 + rows_per_pass, :], gain).astype(BF16)
    h_ref[tmp:tmp + tms, :] = _rms_rows(xs_ref[...], gain).astype(BF16)


def _qkv_kernel(xp_ref, xs_ref, g_ref, w_ref, op_ref, os_ref, h_ref, *, tmp, tms):
    @pl.when(pl.program_id(1) == 0)
    def _():
        _norm_to_scratch(h_ref, xp_ref, xs_ref, g_ref[...], tmp, tms)

    z = _dot(h_ref[...], w_ref[...].astype(BF16))
    op_ref[...] = z[:tmp]
    os_ref[...] = z[tmp:]


def _gates_kernel(xp_ref, xs_ref, g_ref, wlr_ref, wup_ref, bg_ref, wgo_ref,
                  lap_ref, las_ref, sgp_ref, sgs_ref, h_ref, *, tmp, tms):
    @pl.when(pl.program_id(1) == 0)
    def _():
        _norm_to_scratch(h_ref, xp_ref, xs_ref, g_ref[...], tmp, tms)
        glr = _dot(h_ref[...], wlr_ref[...].astype(BF16))
        logit = _dot(glr, wup_ref[...]) + bg_ref[...]
        la = _log_sigmoid(logit) * (1.0 / GATE_NORM)
        lap_ref[...] = la[:tmp]
        las_ref[...] = la[tmp:]

    z = _dot(h_ref[...], wgo_ref[...].astype(BF16))
    sg = z * jax.nn.sigmoid(z)
    sgp_ref[...] = sg[:tmp]
    sgs_ref[...] = sg[tmp:]


def _glu_kernel(xp_ref, xs_ref, g_ref, wa_ref, wg_ref, op_ref, os_ref, h_ref, *, tmp, tms):
    @pl.when(pl.program_id(1) == 0)
    def _():
        _norm_to_scratch(h_ref, xp_ref, xs_ref, g_ref[...], tmp, tms)

    h = h_ref[...]
    ua = _dot(h, wa_ref[...].astype(BF16))
    ug = _dot(h, wg_ref[...].astype(BF16))
    glu = ua * jax.nn.sigmoid(ug)
    op_ref[...] = glu[:tmp]
    os_ref[...] = glu[tmp:]


def _outproj_kernel(xp_ref, xs_ref, oap_ref, oas_ref, cp_ref, cs_ref, w1_ref, w2_ref,
                    op_ref, os_ref, a_ref, c_ref, *, tmp, tms):
    @pl.when(pl.program_id(1) == 0)
    def _():
        a_ref[0:tmp, :] = oap_ref[...].astype(BF16)
        a_ref[tmp:tmp + tms, :] = oas_ref[...].astype(BF16)
        c_ref[0:tmp, :] = cp_ref[...].astype(BF16)
        c_ref[tmp:tmp + tms, :] = cs_ref[...].astype(BF16)

    z = _dot(a_ref[...], w1_ref[...].astype(BF16)) + _dot(c_ref[...], w2_ref[...].astype(BF16))
    op_ref[...] = xp_ref[...] + z[:tmp]
    os_ref[...] = xs_ref[...] + z[tmp:]


def _ffn_kernel(xp_ref, xs_ref, g_ref, gf_ref, wg_ref, wu_ref, wo_ref, op_ref, os_ref, h_ref,
                *, tmp, tms, nf):
    j = pl.program_id(1)

    @pl.when(j == 0)
    def _():
        _norm_to_scratch(h_ref, xp_ref, xs_ref, g_ref[...], tmp, tms)
        op_ref[...] = xp_ref[...]
        os_ref[...] = xs_ref[...]

    h = h_ref[...]
    fg = _dot(h, wg_ref[...].astype(BF16))
    fu = _dot(h, wu_ref[...].astype(BF16))
    act = (fg * jax.nn.sigmoid(fg) * fu).astype(BF16)
    wo = wo_ref[...].astype(BF16)
    half = tmp // 2
    op_ref[0:half, :] += _dot(act[0:half], wo)
    z = _dot(act[half:], wo)
    op_ref[half:tmp, :] += z[:tmp - half]
    os_ref[...] += z[tmp - half:]

    @pl.when(j == nf - 1)
    def _():
        gf = gf_ref[...]
        for r0 in range(0, tmp, 256):
            op_ref[r0:r0 + 256, :] = _rms_rows(op_ref[r0:r0 + 256, :], gf)
        os_ref[...] = _rms_rows(os_ref[...], gf)


@functools.lru_cache(maxsize=None)
def _gla_constants(L):
    nlev = int(np.log2(L)) + 1
    m = np.zeros((nlev + 2, L, L), np.float32)
    uq = np.zeros((nlev, L, LANES), np.float32)
    uk = np.zeros((nlev, L, LANES), np.float32)
    wm = np.zeros((nlev, L, L), np.float32)
    idx = np.arange(L)
    uq[0] = 1.0
    uk[0] = 1.0
    wm[0] = np.eye(L, dtype=np.float32)
    for lev in range(1, nlev):
        half = L >> lev
        pos = idx % (2 * half)
        upper = pos >= half
        r = idx - pos + half - 1
        t = idx[None, :]
        up_rows = (t > r[:, None]) & (t <= idx[:, None]) & upper[:, None]
        lo_rows = (t > idx[:, None]) & (t <= r[:, None]) & (~upper)[:, None]
        m[lev] = (up_rows | lo_rows).astype(np.float32)
        uq[lev] = upper[:, None].astype(np.float32)
        uk[lev] = (~upper)[:, None].astype(np.float32)
        same = (idx[:, None] // (2 * half)) == (idx[None, :] // (2 * half))
        wm[lev] = (same & upper[:, None] & (~upper)[None, :]).astype(np.float32)
    m[nlev] = (idx[None, :] <= idx[:, None]).astype(np.float32)
    m[nlev + 1] = (idx[None, :] > idx[:, None]).astype(np.float32)
    return nlev, m.reshape((nlev + 2) * L, L), uq, uk, wm


def _gla_prompt_kernel(qkv_ref, la_ref, sg_ref, mall_ref, uq_ref, uk_ref, wm_ref, gn_ref,
                       o_ref, snew_ref, s_ref, *, nlev, nchunks):
    L = GLA_CHUNK
    c = pl.program_id(1)

    @pl.when(c == 0)
    def _():
        s_ref[...] = jnp.zeros_like(s_ref)

    la = la_ref[...]
    la_hi = la.astype(BF16)
    la_lo = (la - la_hi.astype(F32)).astype(BF16)
    mall = mall_ref[...]
    e_all = _dot(mall, la_hi) + _dot(mall, la_lo)
    gn = gn_ref[...]
    scale = HEAD_K ** -0.5

    for h in range(N_HEADS):
        ks = slice(h * HEAD_K, (h + 1) * HEAD_K)
        vs = slice(h * HEAD_V, (h + 1) * HEAD_V)
        q = qkv_ref[:, h * HEAD_K:(h + 1) * HEAD_K] * scale
        k = qkv_ref[:, D_K + h * HEAD_K:D_K + (h + 1) * HEAD_K]
        v = qkv_ref[:, 2 * D_K + h * HEAD_V:2 * D_K + (h + 1) * HEAD_V]
        vb = v.astype(BF16)

        a = wm_ref[0] * _dot_nt(q.astype(BF16), k.astype(BF16))
        for lev in range(1, nlev):
            e = jnp.exp(e_all[lev * L:(lev + 1) * L, ks])
            qt = (q * e * uq_ref[lev]).astype(BF16)
            kt = (k * e * uk_ref[lev]).astype(BF16)
            a = a + wm_ref[lev] * _dot_nt(qt, kt)

        eb = e_all[nlev * L:(nlev + 1) * L, ks]
        ek = e_all[(nlev + 1) * L:(nlev + 2) * L, ks]
        qh = (q * jnp.exp(eb)).astype(BF16)
        kh = k * jnp.exp(ek)
        s = s_ref[h]
        o = _dot(a.astype(BF16), vb) + _dot(qh, s.astype(BF16))

        blast = jnp.broadcast_to(eb[L - 1:L, :], (HEAD_K, HEAD_K))
        acol = jnp.exp(blast.T)
        acol = jnp.concatenate([acol, acol], axis=1)
        s_ref[h] = acol * s + _dot(kh.T.astype(BF16), vb)

        y = _rms_rows(o, gn) * sg_ref[:, vs]
        o_ref[:, vs] = y

    @pl.when(c == nchunks - 1)
    def _():
        snew_ref[0, 0] = s_ref[...]


def _gla_sample_kernel(qkv_ref, la_ref, sg_ref, st_ref, gn_ref, o_ref, snew_ref,
                       qt_ref, kt_ref, at_ref, *, bblk):
    s_idx = pl.program_id(0)

    @pl.when(s_idx == 0)
    def _():
        scale = HEAD_K ** -0.5
        for h in range(N_HEADS):
            ks = slice(h * HEAD_K, (h + 1) * HEAD_K)
            qt_ref[h] = (qkv_ref[:, ks] * scale).T
            kt_ref[h] = qkv_ref[:, D_K + h * HEAD_K:D_K + (h + 1) * HEAD_K].T
            at_ref[h] = jnp.exp(la_ref[:, ks]).T

    gn = gn_ref[...]
    lane = lax.broadcasted_iota(jnp.int32, (HEAD_K, LANES), 1)
    for j in range(bblk):
        b = s_idx * bblk + j
        sel = lane == b
        for h in range(N_HEADS):
            vs = slice(h * HEAD_V, (h + 1) * HEAD_V)
            acol = jnp.sum(jnp.where(sel, at_ref[h], 0.0), axis=1, keepdims=True)
            kcol = jnp.sum(jnp.where(sel, kt_ref[h], 0.0), axis=1, keepdims=True)
            qcol = jnp.sum(jnp.where(sel, qt_ref[h], 0.0), axis=1, keepdims=True)
            vrow = qkv_ref[pl.ds(b, 1), 2 * D_K + h * HEAD_V:2 * D_K + (h + 1) * HEAD_V]
            s_new = acol * st_ref[j, h] + kcol * vrow
            snew_ref[0, j, h] = s_new
            o = jnp.sum(qcol * s_new, axis=0, keepdims=True)
            y = _rms_rows(o, gn) * sg_ref[pl.ds(b, 1), vs]
            o_ref[pl.ds(b, 1), vs] = y


def _ln_silu(y, g, b):
    mu = jnp.mean(y, axis=-1, keepdims=True)
    yc = y - mu
    var = jnp.mean(yc * yc, axis=-1, keepdims=True)
    z = yc * lax.rsqrt(var + LN_EPS) * g + b
    return z * jax.nn.sigmoid(z)


def _conv_prompt_kernel(glu_ref, cw_ref, cb_ref, lg_ref, lb_ref, c_ref, cnew_ref, slab_ref, y_ref,
                        *, tt, nsteps, hist):
    t_idx = pl.program_id(1)
    nslab = slab_ref.shape[0]
    pad = hist - (CONV_W - 1)

    @pl.when(t_idx == 0)
    def _():
        slab_ref[:, 0:hist, :] = jnp.zeros((nslab, hist, LANES), F32)

    @pl.when(t_idx > 0)
    def _():
        slab_ref[:, 0:hist, :] = slab_ref[:, tt:tt + hist, :]

    for s in range(nslab):
        slab_ref[s, hist:hist + tt, :] = glu_ref[:, s * LANES:(s + 1) * LANES]

    def slab_conv(s, carry):
        acc = None
        for r in range(8):
            part = None
            for w in range(r, CONV_W, 8):
                term = slab_ref[s, pad + w:pad + w + tt, :] * cw_ref[w, pl.ds(s, 1), :]
                part = term if part is None else part + term
            acc = part if acc is None else acc + part
        y_ref[s] = acc
        return carry

    lax.fori_loop(0, nslab, slab_conv, 0)

    for s in range(nslab):
        cs = slice(s * LANES, (s + 1) * LANES)
        y_ref[s] = y_ref[s] + cb_ref[:, cs]
    y = y_ref[...]
    inv_c = 1.0 / (nslab * LANES)
    mu = jnp.sum(jnp.sum(y, axis=0), axis=-1, keepdims=True) * inv_c
    yc = y - mu[None]
    var = jnp.sum(jnp.sum(yc * yc, axis=0), axis=-1, keepdims=True) * inv_c
    rstd = lax.rsqrt(var + LN_EPS)
    for s in range(nslab):
        cs = slice(s * LANES, (s + 1) * LANES)
        z = yc[s] * rstd * lg_ref[:, cs] + lb_ref[:, cs]
        c_ref[:, cs] = z * jax.nn.sigmoid(z)

    @pl.when(t_idx == nsteps - 1)
    def _():
        for s in range(nslab):
            cnew_ref[0, 0, :, s * LANES:(s + 1) * LANES] = slab_ref[s, hist + tt - (CONV_W - 1):hist + tt, :]


def _conv_sample_kernel(sc_ref, glu_ref, cw_ref, cb_ref, lg_ref, lb_ref, c_ref, cnew_ref, *, bblk):
    s_idx = pl.program_id(0)
    nh = CONV_W - 1
    cw_hist = cw_ref[0:nh, :]
    cw_last = cw_ref[nh:nh + 1, :]
    for j in range(bblk):
        b = s_idx * bblk + j
        grow = glu_ref[pl.ds(b, 1), :]
        y = jnp.sum(sc_ref[j] * cw_hist, axis=0, keepdims=True) + grow * cw_last + cb_ref[...]
        c_ref[pl.ds(b, 1), :] = _ln_silu(y, lg_ref[...], lb_ref[...])
        cnew_ref[0, j, 0:nh - 1, :] = sc_ref[j, 1:nh, :]
        cnew_ref[0, j, nh - 1:nh, :] = grow


def _row_specs(tmp, tms, cols, col_block=None):
    if col_block is None:
        return (pl.BlockSpec((tmp, cols), lambda i, j: (i, 0)),
                pl.BlockSpec((tms, cols), lambda i, j: (i, 0)))
    return (pl.BlockSpec((tmp, col_block), lambda i, j: (i, j)),
            pl.BlockSpec((tms, col_block), lambda i, j: (i, j)))


def _const_spec(shape):
    nd = len(shape)
    return pl.BlockSpec(shape, lambda *_: (0,) * nd)


def kernel(x_prompt, x_sample, state_gla, state_conv, norm_mix, w_in, w_gate_up, b_gate, gla_norm,
           conv_w, conv_b, conv_ln_g, conv_ln_b, w_out, norm_ffn, w_ffn_in, w_ffn_out, norm_final):
    B, T, D = x_prompt.shape
    BS = x_sample.shape[0]
    depth = state_gla.shape[0]
    assert depth == 1 and x_sample.shape[1] == 1
    d_conv = state_conv.shape[-1]
    d_ff = w_ffn_out.shape[1]
    MP = B * T
    n_tiles = 8
    tmp, tms = MP // n_tiles, BS // n_tiles
    assert tmp * n_tiles == MP and tms * n_tiles == BS and tms % 8 == 0 and tmp % 256 == 0
    tm = tmp + tms

    xp = x_prompt.reshape(MP, D)
    xs = x_sample.reshape(BS, D)
    w_in0 = w_in[0]
    gain_mix = norm_mix[0].reshape(1, D)
    c_glr = 2 * D_K + D_A
    c_gout = c_glr + GATE_RANK
    c_u = c_gout + D_A
    w_glr = w_in0[:, c_glr:c_gout]
    w_gout = w_in0[:, c_gout:c_u]
    w_u = w_in0[:, c_u:]

    def pair_shapes(cols):
        return (jax.ShapeDtypeStruct((MP, cols), F32), jax.ShapeDtypeStruct((BS, cols), F32))

    h_scratch = pltpu.VMEM((tm, D), BF16)
    xspecs = _row_specs(tmp, tms, D)
    gspec = _const_spec((1, D))

    tn = 512
    nqkv = 2 * D_K + D_A
    qkv_p, qkv_s = pl.pallas_call(
        functools.partial(_qkv_kernel, tmp=tmp, tms=tms),
        grid=(n_tiles, nqkv // tn),
        in_specs=[*xspecs, gspec, pl.BlockSpec((D, tn), lambda i, j: (0, j))],
        out_specs=_row_specs(tmp, tms, nqkv, tn),
        out_shape=pair_shapes(nqkv),
        scratch_shapes=[h_scratch],
        compiler_params=_cparams(("arbitrary", "arbitrary")),
        name="in_proj_qkv",
    )(xp, xs, gain_mix, w_in0)

    la_p, la_s, sg_p, sg_s = pl.pallas_call(
        functools.partial(_gates_kernel, tmp=tmp, tms=tms),
        grid=(n_tiles, D_A // tn),
        in_specs=[*xspecs, gspec, _const_spec((D, GATE_RANK)), _const_spec((GATE_RANK, D_K)),
                  _const_spec((1, D_K)), pl.BlockSpec((D, tn), lambda i, j: (0, j))],
        out_specs=(*_row_specs(tmp, tms, D_K), *_row_specs(tmp, tms, D_A, tn)),
        out_shape=(*pair_shapes(D_K), *pair_shapes(D_A)),
        scratch_shapes=[h_scratch],
        compiler_params=_cparams(("arbitrary", "arbitrary")),
        name="in_proj_gates",
    )(xp, xs, gain_mix, w_glr, w_gate_up[0], b_gate[0].reshape(1, D_K), w_gout)

    nj = d_conv // tn
    glu_p, glu_s = pl.pallas_call(
        functools.partial(_glu_kernel, tmp=tmp, tms=tms),
        grid=(n_tiles, nj),
        in_specs=[*xspecs, gspec, pl.BlockSpec((D, tn), lambda i, j: (0, j)),
                  pl.BlockSpec((D, tn), lambda i, j: (0, j + nj))],
        out_specs=_row_specs(tmp, tms, d_conv, tn),
        out_shape=pair_shapes(d_conv),
        scratch_shapes=[h_scratch],
        compiler_params=_cparams(("arbitrary", "arbitrary")),
        name="in_proj_glu",
    )(xp, xs, gain_mix, w_u, w_u)

    L = GLA_CHUNK
    nchunks = T // L
    assert nchunks * L == T
    nlev, mall, uq, uk, wm = _gla_constants(L)
    gn = gla_norm[0].reshape(1, HEAD_V)
    oa_p, gla_new_p = pl.pallas_call(
        functools.partial(_gla_prompt_kernel, nlev=nlev, nchunks=nchunks),
        grid=(B, nchunks),
        in_specs=[pl.BlockSpec((L, nqkv), lambda b, c: (b * nchunks + c, 0)),
                  pl.BlockSpec((L, D_K), lambda b, c: (b * nchunks + c, 0)),
                  pl.BlockSpec((L, D_A), lambda b, c: (b * nchunks + c, 0)),
                  _const_spec(mall.shape), _const_spec(uq.shape), _const_spec(uk.shape),
                  _const_spec(wm.shape), _const_spec((1, HEAD_V))],
        out_specs=(pl.BlockSpec((L, D_A), lambda b, c: (b * nchunks + c, 0)),
                   pl.BlockSpec((1, 1, N_HEADS, HEAD_K, HEAD_V), lambda b, c: (0, b, 0, 0, 0))),
        out_shape=(jax.ShapeDtypeStruct((MP, D_A), F32),
                   jax.ShapeDtypeStruct((1, B, N_HEADS, HEAD_K, HEAD_V), F32)),
        scratch_shapes=[pltpu.VMEM((N_HEADS, HEAD_K, HEAD_V), F32)],
        compiler_params=_cparams(("arbitrary", "arbitrary")),
        name="gla_prompt",
    )(qkv_p, la_p, sg_p, jnp.asarray(mall, BF16), jnp.asarray(uq), jnp.asarray(uk), jnp.asarray(wm), gn)

    bblk = 8
    assert BS == LANES and BS % bblk == 0
    oa_s, gla_new_s = pl.pallas_call(
        functools.partial(_gla_sample_kernel, bblk=bblk),
        grid=(BS // bblk,),
        in_specs=[_const_spec((BS, nqkv)), _const_spec((BS, D_K)), _const_spec((BS, D_A)),
                  pl.BlockSpec((bblk, N_HEADS, HEAD_K, HEAD_V), lambda s: (s, 0, 0, 0)),
                  _const_spec((1, HEAD_V))],
        out_specs=(_const_spec((BS, D_A)),
                   pl.BlockSpec((1, bblk, N_HEADS, HEAD_K, HEAD_V), lambda s: (0, s, 0, 0, 0))),
        out_shape=(jax.ShapeDtypeStruct((BS, D_A), F32),
                   jax.ShapeDtypeStruct((1, BS, N_HEADS, HEAD_K, HEAD_V), F32)),
        scratch_shapes=[pltpu.VMEM((N_HEADS, HEAD_K, BS), F32)] * 3,
        compiler_params=_cparams(("arbitrary",)),
        name="gla_sample",
    )(qkv_s, la_s, sg_s, state_gla[0], gn)

    tt = 512
    hist = 32
    nsteps = T // tt
    nslab = d_conv // LANES
    vec = lambda a: a.reshape(1, d_conv)
    c_p, conv_new_p = pl.pallas_call(
        functools.partial(_conv_prompt_kernel, tt=tt, nsteps=nsteps, hist=hist),
        grid=(B, nsteps),
        in_specs=[pl.BlockSpec((tt, d_conv), lambda b, t: (b * nsteps + t, 0)),
                  _const_spec((CONV_W, nslab, LANES)), _const_spec((1, d_conv)),
                  _const_spec((1, d_conv)), _const_spec((1, d_conv))],
        out_specs=(pl.BlockSpec((tt, d_conv), lambda b, t: (b * nsteps + t, 0)),
                   pl.BlockSpec((1, 1, CONV_W - 1, d_conv), lambda b, t: (0, b, 0, 0))),
        out_shape=(jax.ShapeDtypeStruct((MP, d_conv), F32),
                   jax.ShapeDtypeStruct((1, B, CONV_W - 1, d_conv), F32)),
        scratch_shapes=[pltpu.VMEM((nslab, hist + tt, LANES), F32), pltpu.VMEM((nslab, tt, LANES), F32)],
        compiler_params=_cparams(("arbitrary", "arbitrary")),
        name="conv_prompt",
    )(glu_p, conv_w[0].reshape(CONV_W, nslab, LANES), vec(conv_b[0]), vec(conv_ln_g[0]), vec(conv_ln_b[0]))

    c_s, conv_new_s = pl.pallas_call(
        functools.partial(_conv_sample_kernel, bblk=bblk),
        grid=(BS // bblk,),
        in_specs=[pl.BlockSpec((bblk, CONV_W - 1, d_conv), lambda s: (s, 0, 0)),
                  _const_spec((BS, d_conv)), _const_spec((CONV_W, d_conv)), _const_spec((1, d_conv)),
                  _const_spec((1, d_conv)), _const_spec((1, d_conv))],
        out_specs=(_const_spec((BS, d_conv)),
                   pl.BlockSpec((1, bblk, CONV_W - 1, d_conv), lambda s: (0, s, 0, 0))),
        out_shape=(jax.ShapeDtypeStruct((BS, d_conv), F32),
                   jax.ShapeDtypeStruct((1, BS, CONV_W - 1, d_conv), F32)),
        compiler_params=_cparams(("arbitrary",)),
        name="conv_sample",
    )(state_conv[0], glu_s, conv_w[0], vec(conv_b[0]), vec(conv_ln_g[0]), vec(conv_ln_b[0]))

    w_out0 = w_out[0]
    x1_p, x1_s = pl.pallas_call(
        functools.partial(_outproj_kernel, tmp=tmp, tms=tms),
        grid=(n_tiles, D // tn),
        in_specs=[*_row_specs(tmp, tms, D, tn), *_row_specs(tmp, tms, D_A), *_row_specs(tmp, tms, d_conv),
                  pl.BlockSpec((D_A, tn), lambda i, j: (0, j)),
                  pl.BlockSpec((d_conv, tn), lambda i, j: (D_A // d_conv, j))],
        out_specs=_row_specs(tmp, tms, D, tn),
        out_shape=pair_shapes(D),
        scratch_shapes=[pltpu.VMEM((tm, D_A), BF16), pltpu.VMEM((tm, d_conv), BF16)],
        compiler_params=_cparams(("arbitrary", "arbitrary")),
        name="out_proj",
    )(xp, xs, oa_p, oa_s, c_p, c_s, w_out0, w_out0)

    tf = 256
    nf = d_ff // tf
    assert nf * tf == d_ff
    y_p, y_s = pl.pallas_call(
        functools.partial(_ffn_kernel, tmp=tmp, tms=tms, nf=nf),
        grid=(n_tiles, nf),
        in_specs=[*xspecs, gspec, gspec,
                  pl.BlockSpec((D, tf), lambda i, j: (0, j)),
                  pl.BlockSpec((D, tf), lambda i, j: (0, j + nf)),
                  pl.BlockSpec((tf, D), lambda i, j: (j, 0))],
        out_specs=_row_specs(tmp, tms, D),
        out_shape=pair_shapes(D),
        scratch_shapes=[h_scratch],
        compiler_params=_cparams(("arbitrary", "arbitrary")),
        name="ffn",
    )(x1_p, x1_s, norm_ffn[0].reshape(1, D), norm_final.reshape(1, D), w_ffn_in[0], w_ffn_in[0], w_ffn_out[0])

    return (y_p.reshape(B, T, D), y_s.reshape(BS, 1, D), gla_new_p, conv_new_p, gla_new_s, conv_new_s)
```

```python
import functools

import numpy as np
import jax
import jax.numpy as jnp
from jax import lax
from jax.experimental import pallas as pl
from jax.experimental.pallas import tpu as pltpu

F32 = jnp.float32
BF16 = jnp.bfloat16

RMS_EPS = 1e-6
LN_EPS = 1e-5
GATE_NORM = 16.0
N_HEADS = 4
HEAD_K = 128
HEAD_V = 256
D_K = N_HEADS * HEAD_K
D_A = N_HEADS * HEAD_V
GATE_RANK = 16
CONV_W = 31

LANES = 128
MXU_N = 256
GLA_CHUNK = 128
VMEM_LIMIT = 58 * 1024 * 1024

ZB = 512
Z_QKV, Z_SG, Z_GLU, Z_LA = 0, 2 * D_K + D_A, 2 * D_K + 2 * D_A, 2 * D_K + 3 * D_A
Z_COLS = Z_LA + D_K


def _cparams(sem):
    return pltpu.CompilerParams(dimension_semantics=sem, vmem_limit_bytes=VMEM_LIMIT)


def _dot(a, b):
    return jnp.dot(a, b, preferred_element_type=F32)


def _dot_nt(a, b):
    return lax.dot_general(a, b, (((1,), (1,)), ((), ())), preferred_element_type=F32)


def _rms_rows(x, gain):
    ms = jnp.mean(x * x, axis=-1, keepdims=True)
    return x * lax.rsqrt(ms + RMS_EPS) * gain


def _log_sigmoid(x):
    return jnp.minimum(x, 0.0) - jnp.log(1.0 + jnp.exp(-jnp.abs(x)))


def _norm_to_scratch(h_ref, xp_ref, xs_ref, gain, tmp, tms, rows_per_pass=256):
    for r0 in range(0, tmp, rows_per_pass):
        h_ref[r0:r0 + rows_per_pass, :] = _rms_rows(xp_ref[r0:r0 + rows_per_pass, :], gain).astype(BF16)
    h_ref[tmp:tmp + tms, :] = _rms_rows(xs_ref[...], gain).astype(BF16)


def _in_proj_kernel(wt_ref, xp_ref, xs_ref, g_ref, wup_ref, bg_ref, zp_ref, zs_ref,
                    wb_ref, wlr_ref, prev_ref, h_ref, *, nw, nj, tmp, tms):
    s = pl.program_id(0)
    rb = MXU_N
    n_direct = (2 * D_K + D_A) // rb

    @pl.when(s < nw)
    def _load_weight():
        t = s - 1

        @pl.when((s >= 1) & (t < n_direct))
        def _():
            wb_ref[t] = prev_ref[...].T.astype(BF16)

        @pl.when(t >= n_direct)
        def _():
            rows = jnp.concatenate([prev_ref[GATE_RANK:rb, :], wt_ref[0:GATE_RANK, :]], axis=0)
            wb_ref[t] = rows.T.astype(BF16)

        @pl.when(s == n_direct)
        def _():
            wlr_ref[...] = wt_ref[0:LANES, :].T.astype(BF16)

        @pl.when(s < nw - 1)
        def _():
            prev_ref[...] = wt_ref[...]

    @pl.when(s >= nw)
    def _project():
        j = lax.rem(s - nw, nj)

        @pl.when(j == 0)
        def _():
            _norm_to_scratch(h_ref, xp_ref, xs_ref, g_ref[...], tmp, tms)

        def pair(t0):
            h = h_ref[...]
            return jnp.concatenate([_dot(h, wb_ref[t0]), _dot(h, wb_ref[t0 + 1])], axis=1)

        def put(z):
            zp_ref[...] = z[:tmp]
            zs_ref[...] = z[tmp:]

        nqkv, nsg, nglu = Z_SG // ZB, (Z_GLU - Z_SG) // ZB, (Z_LA - Z_GLU) // ZB
        bpz = ZB // rb

        @pl.when(j < nqkv)
        def _():
            put(pair(bpz * j))

        @pl.when((j >= nqkv) & (j < nqkv + nsg))
        def _():
            z = pair(bpz * j)
            put(z * jax.nn.sigmoid(z))

        @pl.when((j >= nqkv + nsg) & (j < nqkv + nsg + nglu))
        def _():
            ta = bpz * j
            ua = pair(ta)
            ug = pair(ta + bpz * nglu)
            put(ua * jax.nn.sigmoid(ug))

        @pl.when(j == nqkv + nsg + nglu)
        def _():
            glr = _dot(h_ref[...], wlr_ref[...])[:, 0:GATE_RANK]
            logit = _dot(glr, wup_ref[...]) + bg_ref[...]
            put(_log_sigmoid(logit) * (1.0 / GATE_NORM))


def _outproj_kernel(w_ref, xp_ref, xs_ref, oap_ref, oas_ref, cp_ref, cs_ref, op_ref, os_ref,
                    wb_ref, a_ref, c_ref, *, nw, nj, tmp, tms):
    s = pl.program_id(0)
    rb = MXU_N

    @pl.when(s < nw)
    def _load_weight():
        wb_ref[s] = w_ref[...].astype(BF16)

    @pl.when(s >= nw)
    def _project():
        j = lax.rem(s - nw, nj)

        @pl.when(j == 0)
        def _():
            a_ref[0:tmp, :] = oap_ref[...].astype(BF16)
            a_ref[tmp:tmp + tms, :] = oas_ref[...].astype(BF16)
            c_ref[0:tmp, :] = cp_ref[...].astype(BF16)
            c_ref[tmp:tmp + tms, :] = cs_ref[...].astype(BF16)

        for half in range(ZB // rb):
            t = (ZB // rb) * j + half
            cols = slice(half * rb, (half + 1) * rb)
            z = _dot(a_ref[...], wb_ref[t, 0:D_A, :]) + _dot(c_ref[...], wb_ref[t, D_A:, :])
            op_ref[:, cols] = xp_ref[:, cols] + z[:tmp]
            os_ref[:, cols] = xs_ref[:, cols] + z[tmp:]


def _ffn_kernel(xp_ref, xs_ref, g_ref, gf_ref, wg_ref, wu_ref, wo_ref, op_ref, os_ref, h_ref,
                *, tmp, tms, nf):
    j = pl.program_id(1)

    @pl.when(j == 0)
    def _():
        _norm_to_scratch(h_ref, xp_ref, xs_ref, g_ref[...], tmp, tms)
        op_ref[...] = xp_ref[...]
        os_ref[...] = xs_ref[...]

    h = h_ref[...]
    fg = _dot(h, wg_ref[...].astype(BF16))
    fu = _dot(h, wu_ref[...].astype(BF16))
    act = (fg * jax.nn.sigmoid(fg) * fu).astype(BF16)
    wo = wo_ref[...].astype(BF16)
    half = tmp // 2
    op_ref[0:half, :] += _dot(act[0:half], wo)
    z = _dot(act[half:], wo)
    op_ref[half:tmp, :] += z[:tmp - half]
    os_ref[...] += z[tmp - half:]

    @pl.when(j == nf - 1)
    def _():
        gf = gf_ref[...]
        for r0 in range(0, tmp, 256):
            op_ref[r0:r0 + 256, :] = _rms_rows(op_ref[r0:r0 + 256, :], gf)
        os_ref[...] = _rms_rows(os_ref[...], gf)


@functools.lru_cache(maxsize=None)
def _gla_constants(L):
    nlev = int(np.log2(L)) + 1
    m = np.zeros((nlev + 2, L, L), np.float32)
    uq = np.zeros((nlev, L, LANES), np.float32)
    uk = np.zeros((nlev, L, LANES), np.float32)
    wm = np.zeros((nlev, L, L), np.float32)
    idx = np.arange(L)
    uq[0] = 1.0
    uk[0] = 1.0
    wm[0] = np.eye(L, dtype=np.float32)
    for lev in range(1, nlev):
        half = L >> lev
        pos = idx % (2 * half)
        upper = pos >= half
        r = idx - pos + half - 1
        t = idx[None, :]
        up_rows = (t > r[:, None]) & (t <= idx[:, None]) & upper[:, None]
        lo_rows = (t > idx[:, None]) & (t <= r[:, None]) & (~upper)[:, None]
        m[lev] = (up_rows | lo_rows).astype(np.float32)
        uq[lev] = upper[:, None].astype(np.float32)
        uk[lev] = (~upper)[:, None].astype(np.float32)
        same = (idx[:, None] // (2 * half)) == (idx[None, :] // (2 * half))
        wm[lev] = (same & upper[:, None] & (~upper)[None, :]).astype(np.float32)
    m[nlev] = (idx[None, :] <= idx[:, None]).astype(np.float32)
    m[nlev + 1] = (idx[None, :] > idx[:, None]).astype(np.float32)
    return nlev, m.reshape((nlev + 2) * L, L), uq, uk, wm


def _gla_prompt_kernel(qkv_ref, la_ref, sg_ref, mall_ref, uq_ref, uk_ref, wm_ref, gn_ref,
                       o_ref, snew_ref, s_ref, *, nlev, nchunks):
    L = GLA_CHUNK
    c = pl.program_id(1)

    @pl.when(c == 0)
    def _():
        s_ref[...] = jnp.zeros_like(s_ref)

    la = la_ref[...]
    la_hi = la.astype(BF16)
    la_lo = (la - la_hi.astype(F32)).astype(BF16)
    mall = mall_ref[...]
    e_all = _dot(mall, la_hi) + _dot(mall, la_lo)
    gn = gn_ref[...]
    scale = HEAD_K ** -0.5

    for h in range(N_HEADS):
        ks = slice(h * HEAD_K, (h + 1) * HEAD_K)
        vs = slice(h * HEAD_V, (h + 1) * HEAD_V)
        q = qkv_ref[:, h * HEAD_K:(h + 1) * HEAD_K] * scale
        k = qkv_ref[:, D_K + h * HEAD_K:D_K + (h + 1) * HEAD_K]
        v = qkv_ref[:, 2 * D_K + h * HEAD_V:2 * D_K + (h + 1) * HEAD_V]
        vb = v.astype(BF16)

        a = wm_ref[0] * _dot_nt(q.astype(BF16), k.astype(BF16))
        for lev in range(1, nlev):
            e = jnp.exp(e_all[lev * L:(lev + 1) * L, ks])
            qt = (q * e * uq_ref[lev]).astype(BF16)
            kt = (k * e * uk_ref[lev]).astype(BF16)
            a = a + wm_ref[lev] * _dot_nt(qt, kt)

        eb = e_all[nlev * L:(nlev + 1) * L, ks]
        ek = e_all[(nlev + 1) * L:(nlev + 2) * L, ks]
        qh = (q * jnp.exp(eb)).astype(BF16)
        kh = k * jnp.exp(ek)
        s = s_ref[h]
        o = _dot(a.astype(BF16), vb) + _dot(qh, s.astype(BF16))

        blast = jnp.broadcast_to(eb[L - 1:L, :], (HEAD_K, HEAD_K))
        acol = jnp.exp(blast.T)
        acol = jnp.concatenate([acol, acol], axis=1)
        s_ref[h] = acol * s + _dot(kh.T.astype(BF16), vb)

        y = _rms_rows(o, gn) * sg_ref[:, vs]
        o_ref[:, vs] = y

    @pl.when(c == nchunks - 1)
    def _():
        snew_ref[0, 0] = s_ref[...]


def _gla_sample_kernel(qkv_ref, la_ref, sg_ref, st_ref, gn_ref, o_ref, snew_ref,
                       qt_ref, kt_ref, at_ref, *, bblk):
    s_idx = pl.program_id(0)

    @pl.when(s_idx == 0)
    def _():
        scale = HEAD_K ** -0.5
        for h in range(N_HEADS):
            ks = slice(h * HEAD_K, (h + 1) * HEAD_K)
            qt_ref[h] = (qkv_ref[:, ks] * scale).T
            kt_ref[h] = qkv_ref[:, D_K + h * HEAD_K:D_K + (h + 1) * HEAD_K].T
            at_ref[h] = jnp.exp(la_ref[:, ks]).T

    gn = gn_ref[...]
    lane = lax.broadcasted_iota(jnp.int32, (HEAD_K, LANES), 1)
    for j in range(bblk):
        b = s_idx * bblk + j
        sel = lane == b
        for h in range(N_HEADS):
            vs = slice(h * HEAD_V, (h + 1) * HEAD_V)
            acol = jnp.sum(jnp.where(sel, at_ref[h], 0.0), axis=1, keepdims=True)
            kcol = jnp.sum(jnp.where(sel, kt_ref[h], 0.0), axis=1, keepdims=True)
            qcol = jnp.sum(jnp.where(sel, qt_ref[h], 0.0), axis=1, keepdims=True)
            vrow = qkv_ref[pl.ds(b, 1), 2 * D_K + h * HEAD_V:2 * D_K + (h + 1) * HEAD_V]
            s_new = acol * st_ref[j, h] + kcol * vrow
            snew_ref[0, j, h] = s_new
            o = jnp.sum(qcol * s_new, axis=0, keepdims=True)
            y = _rms_rows(o, gn) * sg_ref[pl.ds(b, 1), vs]
            o_ref[pl.ds(b, 1), vs] = y


def _ln_silu(y, g, b):
    mu = jnp.mean(y, axis=-1, keepdims=True)
    yc = y - mu
    var = jnp.mean(yc * yc, axis=-1, keepdims=True)
    z = yc * lax.rsqrt(var + LN_EPS) * g + b
    return z * jax.nn.sigmoid(z)


def _conv_prompt_kernel(glu_ref, cw_ref, cb_ref, lg_ref, lb_ref, c_ref, cnew_ref, slab_ref, y_ref,
                        *, tt, nsteps, hist):
    t_idx = pl.program_id(1)
    nslab = slab_ref.shape[0]
    pad = hist - (CONV_W - 1)

    @pl.when(t_idx == 0)
    def _():
        slab_ref[:, 0:hist, :] = jnp.zeros((nslab, hist, LANES), F32)

    @pl.when(t_idx > 0)
    def _():
        slab_ref[:, 0:hist, :] = slab_ref[:, tt:tt + hist, :]

    for s in range(nslab):
        slab_ref[s, hist:hist + tt, :] = glu_ref[:, s * LANES:(s + 1) * LANES]

    def slab_conv(s, carry):
        acc = None
        for r in range(8):
            part = None
            for w in range(r, CONV_W, 8):
                term = slab_ref[s, pad + w:pad + w + tt, :] * cw_ref[w, pl.ds(s, 1), :]
                part = term if part is None else part + term
            acc = part if acc is None else acc + part
        y_ref[s] = acc
        return carry

    lax.fori_loop(0, nslab, slab_conv, 0)

    for s in range(nslab):
        cs = slice(s * LANES, (s + 1) * LANES)
        y_ref[s] = y_ref[s] + cb_ref[:, cs]
    y = y_ref[...]
    inv_c = 1.0 / (nslab * LANES)
    mu = jnp.sum(jnp.sum(y, axis=0), axis=-1, keepdims=True) * inv_c
    yc = y - mu[None]
    var = jnp.sum(jnp.sum(yc * yc, axis=0), axis=-1, keepdims=True) * inv_c
    rstd = lax.rsqrt(var + LN_EPS)
    for s in range(nslab):
        cs = slice(s * LANES, (s + 1) * LANES)
        z = yc[s] * rstd * lg_ref[:, cs] + lb_ref[:, cs]
        c_ref[:, cs] = z * jax.nn.sigmoid(z)

    @pl.when(t_idx == nsteps - 1)
    def _():
        for s in range(nslab):
            cnew_ref[0, 0, :, s * LANES:(s + 1) * LANES] = slab_ref[s, hist + tt - (CONV_W - 1):hist + tt, :]


def _conv_sample_kernel(sc_ref, glu_ref, cw_ref, cb_ref, lg_ref, lb_ref, c_ref, cnew_ref):
    nh = CONV_W - 1
    g = glu_ref[...]
    y = g * cw_ref[nh:nh + 1, :] + cb_ref[...]
    for w in range(nh):
        y = y + sc_ref[w] * cw_ref[w:w + 1, :]
    c_ref[...] = _ln_silu(y, lg_ref[...], lb_ref[...])
    for w in range(nh - 1):
        cnew_ref[w] = sc_ref[w + 1]
    cnew_ref[nh - 1] = g


def _fixed(shape, idx=None):
    idx = (0,) * len(shape) if idx is None else idx
    return pl.BlockSpec(shape, lambda *_: idx)


def kernel(x_prompt, x_sample, state_gla, state_conv, norm_mix, w_in, w_gate_up, b_gate, gla_norm,
           conv_w, conv_b, conv_ln_g, conv_ln_b, w_out, norm_ffn, w_ffn_in, w_ffn_out, norm_final):
    B, T, D = x_prompt.shape
    BS = x_sample.shape[0]
    assert state_gla.shape[0] == 1 and x_sample.shape[1] == 1
    d_conv = state_conv.shape[-1]
    d_ff = w_ffn_out.shape[1]
    in_cols = w_in.shape[-1]
    assert d_conv == D_A and in_cols == 2 * D_K + D_A + GATE_RANK + D_A + 2 * d_conv
    MP = B * T
    n_tiles = 8
    tmp, tms = MP // n_tiles, BS // n_tiles
    assert tmp * n_tiles == MP and tms * n_tiles == BS and tms % 8 == 0 and tmp % 256 == 0
    tm = tmp + tms

    xp = x_prompt.reshape(MP, D)
    xs = x_sample.reshape(BS, D)

    def pair_shapes(cols):
        return (jax.ShapeDtypeStruct((MP, cols), F32), jax.ShapeDtypeStruct((BS, cols), F32))

    wt = jnp.swapaxes(w_in, 1, 2)[0]
    nw = pl.cdiv(in_cols, MXU_N)
    n_res = (in_cols - GATE_RANK) // MXU_N
    assert n_res * MXU_N == in_cols - GATE_RANK and nw == n_res + 1
    nj = Z_COLS // ZB
    row_i = lambda s: jnp.maximum(s - nw, 0) // nj
    col_j = lambda s: lax.rem(jnp.maximum(s - nw, 0), nj)
    z_p, z_s = pl.pallas_call(
        functools.partial(_in_proj_kernel, nw=nw, nj=nj, tmp=tmp, tms=tms),
        grid=(nw + n_tiles * nj,),
        in_specs=[pl.BlockSpec((MXU_N, D), lambda s: (jnp.minimum(s, nw - 1), 0)),
                  pl.BlockSpec((tmp, D), lambda s: (row_i(s), 0)),
                  pl.BlockSpec((tms, D), lambda s: (row_i(s), 0)),
                  _fixed((1, D)), _fixed((GATE_RANK, D_K)), _fixed((1, D_K))],
        out_specs=(pl.BlockSpec((tmp, ZB), lambda s: (row_i(s), col_j(s))),
                   pl.BlockSpec((tms, ZB), lambda s: (row_i(s), col_j(s)))),
        out_shape=pair_shapes(Z_COLS),
        scratch_shapes=[pltpu.VMEM((n_res, D, MXU_N), BF16), pltpu.VMEM((D, LANES), BF16),
                        pltpu.VMEM((MXU_N, D), F32), pltpu.VMEM((tm, D), BF16)],
        compiler_params=_cparams(("arbitrary",)),
        name="in_proj",
    )(wt, xp, xs, norm_mix[0].reshape(1, D), w_gate_up[0], b_gate[0].reshape(1, D_K))

    L = GLA_CHUNK
    nchunks = T // L
    assert nchunks * L == T
    nlev, mall, uq, uk, wm = _gla_constants(L)
    gn = gla_norm[0].reshape(1, HEAD_V)
    zrow = lambda b, c: b * nchunks + c
    oa_p, gla_new_p = pl.pallas_call(
        functools.partial(_gla_prompt_kernel, nlev=nlev, nchunks=nchunks),
        grid=(B, nchunks),
        in_specs=[pl.BlockSpec((L, Z_SG), lambda b, c: (zrow(b, c), 0)),
                  pl.BlockSpec((L, D_K), lambda b, c: (zrow(b, c), Z_LA // D_K)),
                  pl.BlockSpec((L, D_A), lambda b, c: (zrow(b, c), Z_SG // D_A)),
                  _fixed(mall.shape), _fixed(uq.shape), _fixed(uk.shape), _fixed(wm.shape),
                  _fixed((1, HEAD_V))],
        out_specs=(pl.BlockSpec((L, D_A), lambda b, c: (zrow(b, c), 0)),
                   pl.BlockSpec((1, 1, N_HEADS, HEAD_K, HEAD_V), lambda b, c: (0, b, 0, 0, 0))),
        out_shape=(jax.ShapeDtypeStruct((MP, D_A), F32),
                   jax.ShapeDtypeStruct((1, B, N_HEADS, HEAD_K, HEAD_V), F32)),
        scratch_shapes=[pltpu.VMEM((N_HEADS, HEAD_K, HEAD_V), F32)],
        compiler_params=_cparams(("arbitrary", "arbitrary")),
        name="gla_prompt",
    )(z_p, z_p, z_p, jnp.asarray(mall, BF16), jnp.asarray(uq), jnp.asarray(uk), jnp.asarray(wm), gn)

    bblk = 8
    assert BS == LANES and BS % bblk == 0
    oa_s, gla_new_s = pl.pallas_call(
        functools.partial(_gla_sample_kernel, bblk=bblk),
        grid=(BS // bblk,),
        in_specs=[_fixed((BS, Z_SG)), _fixed((BS, D_K), (0, Z_LA // D_K)), _fixed((BS, D_A), (0, Z_SG // D_A)),
                  pl.BlockSpec((bblk, N_HEADS, HEAD_K, HEAD_V), lambda s: (s, 0, 0, 0)),
                  _fixed((1, HEAD_V))],
        out_specs=(_fixed((BS, D_A)),
                   pl.BlockSpec((1, bblk, N_HEADS, HEAD_K, HEAD_V), lambda s: (0, s, 0, 0, 0))),
        out_shape=(jax.ShapeDtypeStruct((BS, D_A), F32),
                   jax.ShapeDtypeStruct((1, BS, N_HEADS, HEAD_K, HEAD_V), F32)),
        scratch_shapes=[pltpu.VMEM((N_HEADS, HEAD_K, BS), F32)] * 3,
        compiler_params=_cparams(("arbitrary",)),
        name="gla_sample",
    )(z_s, z_s, z_s, state_gla[0], gn)

    tt = 512
    hist = 32
    nsteps = T // tt
    nslab = d_conv // LANES
    vec = lambda a: a.reshape(1, d_conv)
    c_p, conv_new_p = pl.pallas_call(
        functools.partial(_conv_prompt_kernel, tt=tt, nsteps=nsteps, hist=hist),
        grid=(B, nsteps),
        in_specs=[pl.BlockSpec((tt, d_conv), lambda b, t: (b * nsteps + t, Z_GLU // d_conv)),
                  _fixed((CONV_W, nslab, LANES)), _fixed((1, d_conv)),
                  _fixed((1, d_conv)), _fixed((1, d_conv))],
        out_specs=(pl.BlockSpec((tt, d_conv), lambda b, t: (b * nsteps + t, 0)),
                   pl.BlockSpec((1, 1, CONV_W - 1, d_conv), lambda b, t: (0, b, 0, 0))),
        out_shape=(jax.ShapeDtypeStruct((MP, d_conv), F32),
                   jax.ShapeDtypeStruct((1, B, CONV_W - 1, d_conv), F32)),
        scratch_shapes=[pltpu.VMEM((nslab, hist + tt, LANES), F32), pltpu.VMEM((nslab, tt, LANES), F32)],
        compiler_params=_cparams(("arbitrary", "arbitrary")),
        name="conv_prompt",
    )(z_p, conv_w[0].reshape(CONV_W, nslab, LANES), vec(conv_b[0]), vec(conv_ln_g[0]), vec(conv_ln_b[0]))

    cb_blk = 32
    sc_t = jnp.transpose(state_conv[0], (1, 0, 2))
    c_s, conv_new_s_t = pl.pallas_call(
        _conv_sample_kernel,
        grid=(BS // cb_blk,),
        in_specs=[pl.BlockSpec((CONV_W - 1, cb_blk, d_conv), lambda s: (0, s, 0)),
                  pl.BlockSpec((cb_blk, d_conv), lambda s: (s, Z_GLU // d_conv)),
                  _fixed((CONV_W, d_conv)), _fixed((1, d_conv)), _fixed((1, d_conv)), _fixed((1, d_conv))],
        out_specs=(pl.BlockSpec((cb_blk, d_conv), lambda s: (s, 0)),
                   pl.BlockSpec((CONV_W - 1, cb_blk, d_conv), lambda s: (0, s, 0))),
        out_shape=(jax.ShapeDtypeStruct((BS, d_conv), F32),
                   jax.ShapeDtypeStruct((CONV_W - 1, BS, d_conv), F32)),
        compiler_params=_cparams(("arbitrary",)),
        name="conv_sample",
    )(sc_t, z_s, conv_w[0], vec(conv_b[0]), vec(conv_ln_g[0]), vec(conv_ln_b[0]))
    conv_new_s = jnp.transpose(conv_new_s_t, (1, 0, 2))[None]

    nwo = D // MXU_N
    njo = D // ZB
    row_o = lambda s: jnp.maximum(s - nwo, 0) // njo
    col_o = lambda s: lax.rem(jnp.maximum(s - nwo, 0), njo)
    x1_p, x1_s = pl.pallas_call(
        functools.partial(_outproj_kernel, nw=nwo, nj=njo, tmp=tmp, tms=tms),
        grid=(nwo + n_tiles * njo,),
        in_specs=[pl.BlockSpec((D_A + d_conv, MXU_N), lambda s: (0, jnp.minimum(s, nwo - 1))),
                  pl.BlockSpec((tmp, ZB), lambda s: (row_o(s), col_o(s))),
                  pl.BlockSpec((tms, ZB), lambda s: (row_o(s), col_o(s))),
                  pl.BlockSpec((tmp, D_A), lambda s: (row_o(s), 0)),
                  pl.BlockSpec((tms, D_A), lambda s: (row_o(s), 0)),
                  pl.BlockSpec((tmp, d_conv), lambda s: (row_o(s), 0)),
                  pl.BlockSpec((tms, d_conv), lambda s: (row_o(s), 0))],
        out_specs=(pl.BlockSpec((tmp, ZB), lambda s: (row_o(s), col_o(s))),
                   pl.BlockSpec((tms, ZB), lambda s: (row_o(s), col_o(s)))),
        out_shape=pair_shapes(D),
        scratch_shapes=[pltpu.VMEM((nwo, D_A + d_conv, MXU_N), BF16),
                        pltpu.VMEM((tm, D_A), BF16), pltpu.VMEM((tm, d_conv), BF16)],
        compiler_params=_cparams(("arbitrary",)),
        name="out_proj",
    )(w_out[0], xp, xs, oa_p, oa_s, c_p, c_s)

    tf = 256
    nf = d_ff // tf
    assert nf * tf == d_ff
    y_p, y_s = pl.pallas_call(
        functools.partial(_ffn_kernel, tmp=tmp, tms=tms, nf=nf),
        grid=(n_tiles, nf),
        in_specs=[pl.BlockSpec((tmp, D), lambda i, j: (i, 0)), pl.BlockSpec((tms, D), lambda i, j: (i, 0)),
                  _fixed((1, D)), _fixed((1, D)),
                  pl.BlockSpec((D, tf), lambda i, j: (0, j)),
                  pl.BlockSpec((D, tf), lambda i, j: (0, j + nf)),
                  pl.BlockSpec((tf, D), lambda i, j: (j, 0))],
        out_specs=(pl.BlockSpec((tmp, D), lambda i, j: (i, 0)), pl.BlockSpec((tms, D), lambda i, j: (i, 0))),
        out_shape=pair_shapes(D),
        scratch_shapes=[pltpu.VMEM((tm, D), BF16)],
        compiler_params=_cparams(("arbitrary", "arbitrary")),
        name="ffn",
    )(x1_p, x1_s, norm_ffn[0].reshape(1, D), norm_final.reshape(1, D), w_ffn_in[0], w_ffn_in[0], w_ffn_out[0])

    return (y_p.reshape(B, T, D), y_s.reshape(BS, 1, D), gla_new_p, conv_new_p, gla_new_s, conv_new_s)
```

```python
import functools

import numpy as np
import jax
import jax.numpy as jnp
from jax import lax
from jax.experimental import pallas as pl
from jax.experimental.pallas import tpu as pltpu

F32 = jnp.float32
BF16 = jnp.bfloat16

RMS_EPS = 1e-6
LN_EPS = 1e-5
GATE_NORM = 16.0
N_HEADS = 4
HEAD_K = 128
HEAD_V = 256
D_K = N_HEADS * HEAD_K
D_A = N_HEADS * HEAD_V
GATE_RANK = 16
CONV_W = 31

LANES = 128
MXU_N = 256
GLA_CHUNK = 128
VMEM_LIMIT = 58 * 1024 * 1024
CONV_ROWS = 64
CONV_HIST = 32

ZB = 512
Z_QKV, Z_SG, Z_LA = 0, 2 * D_K + D_A, 2 * D_K + 2 * D_A
Z_COLS = Z_LA + D_K


def _cparams(sem):
    return pltpu.CompilerParams(dimension_semantics=sem, vmem_limit_bytes=VMEM_LIMIT)


def _dot(a, b):
    return jnp.dot(a, b, preferred_element_type=F32)


def _dot_nt(a, b):
    return lax.dot_general(a, b, (((1,), (1,)), ((), ())), preferred_element_type=F32)


def _rms_rows(x, gain):
    ms = jnp.mean(x * x, axis=-1, keepdims=True)
    return x * lax.rsqrt(ms + RMS_EPS) * gain


def _log_sigmoid(x):
    return jnp.minimum(x, 0.0) - jnp.log(1.0 + jnp.exp(-jnp.abs(x)))


def _norm_to_scratch(h_ref, xp_ref, xs_ref, gain, tmp, tms, rows_per_pass=256):
    for r0 in range(0, tmp, rows_per_pass):
        h_ref[r0:r0 + rows_per_pass, :] = _rms_rows(xp_ref[r0:r0 + rows_per_pass, :], gain).astype(BF16)
    h_ref[tmp:tmp + tms, :] = _rms_rows(xs_ref[...], gain).astype(BF16)


def _shifted_weight_block(wt_ref, wnext_ref):
    rows = jnp.concatenate([wt_ref[GATE_RANK:MXU_N, :], wnext_ref[...]], axis=0)
    return rows.T.astype(BF16)


def _glu_proj_kernel(wt_ref, wnext_ref, xp_ref, xs_ref, g_ref, gp_ref, gs_ref, wb_ref, h_ref,
                     *, nw, tmp, tms):
    s = pl.program_id(0)
    rb = MXU_N
    half_blocks = wb_ref.shape[0] // 4

    @pl.when(s < nw)
    def _load_weight():
        wb_ref[s] = _shifted_weight_block(wt_ref, wnext_ref)

    @pl.when(s >= nw)
    def _project():
        j = lax.rem(s - nw, 2)

        @pl.when(j == 0)
        def _():
            _norm_to_scratch(h_ref, xp_ref, xs_ref, g_ref[...], tmp, tms)

        for q in range(half_blocks):
            t = j * half_blocks + q
            h = h_ref[...]
            glu = _dot(h, wb_ref[t]) * jax.nn.sigmoid(_dot(h, wb_ref[2 * half_blocks + t]))
            gp_ref[:, q * rb:(q + 1) * rb] = glu[:tmp]
            gs_ref[:, q * rb:(q + 1) * rb] = glu[tmp:]


def _in_proj_kernel(wt_ref, wnext_ref, xp_ref, xs_ref, g_ref, wup_ref, bg_ref,
                    ga_ref, gb_ref, ha_ref, hb_ref, cw_ref, cb_ref, lg_ref, lb_ref,
                    zp_ref, zs_ref, c_ref, cnew_ref,
                    wb_ref, wlr_ref, h_ref, slab_ref, y_ref, tail_ref,
                    *, nw, nj, tiles_per_seq, tmp, tms):
    s = pl.program_id(0)
    rb = MXU_N
    n_direct = (2 * D_K + D_A) // rb
    nslab = y_ref.shape[0]
    hist = CONV_HIST
    pad = hist - (CONV_W - 1)

    @pl.when(s < n_direct)
    def _():
        wb_ref[s] = wt_ref[...].T.astype(BF16)

    @pl.when((s >= n_direct) & (s < nw))
    def _():
        wb_ref[s] = _shifted_weight_block(wt_ref, wnext_ref)

    @pl.when(s == n_direct)
    def _():
        wlr_ref[...] = wt_ref[0:LANES, :].T.astype(BF16)

    @pl.when(s >= nw)
    def _project():
        r = s - nw
        i = r // nj
        j = lax.rem(r, nj)
        pos = lax.rem(i, tiles_per_seq)

        @pl.when(j == 0)
        def _():
            _norm_to_scratch(h_ref, xp_ref, xs_ref, g_ref[...], tmp, tms)

        def put(z):
            zp_ref[...] = z[:tmp]
            zs_ref[...] = z[tmp:]

        nqkv, nsg = Z_SG // ZB, (Z_LA - Z_SG) // ZB
        bpz = ZB // rb
        assert nslab == 2 * nqkv and tmp % (nsg * 256) == 0

        def conv_task(q, r0):
            def run():
                sl = 2 * j + q
                acc = None
                for res in range(8):
                    part = None
                    for w in range(res, CONV_W, 8):
                        term = (slab_ref[q, pad + w + r0:pad + w + r0 + CONV_ROWS, :]
                                * cw_ref[w, pl.ds(sl, 1), :])
                        part = term if part is None else part + term
                    acc = part if acc is None else acc + part
                y_ref[sl, r0:r0 + CONV_ROWS, :] = acc + cb_ref[pl.ds(sl, 1), :]
            return run

        @pl.when(j < nqkv)
        def _():
            @pl.when(pos == 0)
            def _():
                slab_ref[:, 0:hist, :] = jnp.zeros((2, hist, LANES), F32)

            @pl.when(pos != 0)
            def _():
                slab_ref[0, 0:hist, :] = ha_ref[...]
                slab_ref[1, 0:hist, :] = hb_ref[...]

            for q, glu_ref in enumerate((ga_ref, gb_ref)):
                slab_ref[q, hist:hist + tmp, :] = glu_ref[...]
                tail_ref[2 * j + q] = glu_ref[tmp - hist:tmp, :]

            conv = [conv_task(q, r0) for q in range(2) for r0 in range(0, tmp, CONV_ROWS)]
            per = -(-len(conv) // bpz)
            outs = []
            for t in range(bpz):
                for task in conv[t * per:(t + 1) * per]:
                    task()
                outs.append(_dot(h_ref[...], wb_ref[bpz * j + t]))
            put(jnp.concatenate(outs, axis=1))

        @pl.when((j >= nqkv) & (j < nqkv + nsg))
        def _():
            half = j - nqkv
            rows_c = tmp // nsg
            inv_c = 1.0 / (nslab * LANES)
            chunks = list(range(0, rows_c, 256))
            per = -(-len(chunks) // bpz)
            outs = []
            for t in range(bpz):
                outs.append(_dot(h_ref[...], wb_ref[bpz * j + t]))
                for r0 in chunks[t * per:(t + 1) * per]:
                    y = y_ref[:, pl.ds(pl.multiple_of(half * rows_c + r0, 256), 256), :]
                    mu = jnp.sum(jnp.sum(y, axis=0), axis=-1, keepdims=True) * inv_c
                    yc = y - mu[None]
                    var = jnp.sum(jnp.sum(yc * yc, axis=0), axis=-1, keepdims=True) * inv_c
                    rstd = lax.rsqrt(var + LN_EPS)
                    for sl in range(nslab):
                        cs = slice(sl * LANES, (sl + 1) * LANES)
                        v = yc[sl] * rstd * lg_ref[:, cs] + lb_ref[:, cs]
                        c_ref[r0:r0 + 256, cs] = (v * jax.nn.sigmoid(v)).astype(c_ref.dtype)
            z = jnp.concatenate(outs, axis=1)
            put(z * jax.nn.sigmoid(z))

        @pl.when(j == nqkv + nsg)
        def _():
            glr = _dot(h_ref[...], wlr_ref[...])[:, 0:GATE_RANK]
            logit = _dot(glr, wup_ref[...]) + bg_ref[...]
            put(_log_sigmoid(logit) * (1.0 / GATE_NORM))

            @pl.when(pos == tiles_per_seq - 1)
            def _():
                for sl in range(nslab):
                    cnew_ref[0, 0, :, sl * LANES:(sl + 1) * LANES] = tail_ref[sl, hist - (CONV_W - 1):hist, :]


def _outproj_kernel(w_ref, xp_ref, xs_ref, oap_ref, oas_ref, cp_ref, cs_ref, op_ref, os_ref,
                    wb_ref, a_ref, c_ref, *, nw, nj, tmp, tms):
    s = pl.program_id(0)
    rb = MXU_N

    @pl.when(s < nw)
    def _load_weight():
        wb_ref[s] = w_ref[...].astype(BF16)

    @pl.when(s >= nw)
    def _project():
        j = lax.rem(s - nw, nj)

        @pl.when(j == 0)
        def _():
            a_ref[0:tmp, :] = oap_ref[...]
            a_ref[tmp:tmp + tms, :] = oas_ref[...].astype(BF16)
            c_ref[0:tmp, :] = cp_ref[...]
            c_ref[tmp:tmp + tms, :] = cs_ref[...].astype(BF16)

        for half in range(ZB // rb):
            t = (ZB // rb) * j + half
            cols = slice(half * rb, (half + 1) * rb)
            z = _dot(a_ref[...], wb_ref[t, 0:D_A, :]) + _dot(c_ref[...], wb_ref[t, D_A:, :])
            op_ref[:, cols] = xp_ref[:, cols] + z[:tmp]
            os_ref[:, cols] = xs_ref[:, cols] + z[tmp:]


def _ffn_kernel(xp_ref, xs_ref, g_ref, gf_ref, wg_ref, wu_ref, wo_ref, op_ref, os_ref, h_ref,
                *, tmp, tms, nf):
    j = pl.program_id(1)

    @pl.when(j == 0)
    def _():
        _norm_to_scratch(h_ref, xp_ref, xs_ref, g_ref[...], tmp, tms)
        op_ref[...] = xp_ref[...]
        os_ref[...] = xs_ref[...]

    h = h_ref[...]
    fg = _dot(h, wg_ref[...].astype(BF16))
    fu = _dot(h, wu_ref[...].astype(BF16))
    act = (fg * jax.nn.sigmoid(fg) * fu).astype(BF16)
    wo = wo_ref[...].astype(BF16)
    half = tmp // 2
    op_ref[0:half, :] += _dot(act[0:half], wo)
    z = _dot(act[half:], wo)
    op_ref[half:tmp, :] += z[:tmp - half]
    os_ref[...] += z[tmp - half:]

    @pl.when(j == nf - 1)
    def _():
        gf = gf_ref[...]
        for r0 in range(0, tmp, 256):
            op_ref[r0:r0 + 256, :] = _rms_rows(op_ref[r0:r0 + 256, :], gf)
        os_ref[...] = _rms_rows(os_ref[...], gf)


@functools.lru_cache(maxsize=None)
def _gla_constants(L):
    nlev = int(np.log2(L)) + 1
    m = np.zeros((nlev + 2, L, L), np.float32)
    wm = np.zeros((nlev, L, L), np.float32)
    idx = np.arange(L)
    wm[0] = np.eye(L, dtype=np.float32)
    for lev in range(1, nlev):
        half = L >> lev
        pos = idx % (2 * half)
        upper = pos >= half
        r = idx - pos + half - 1
        t = idx[None, :]
        up_rows = (t > r[:, None]) & (t <= idx[:, None]) & upper[:, None]
        lo_rows = (t > idx[:, None]) & (t <= r[:, None]) & (~upper)[:, None]
        m[lev] = (up_rows | lo_rows).astype(np.float32)
        same = (idx[:, None] // (2 * half)) == (idx[None, :] // (2 * half))
        wm[lev] = (same & upper[:, None] & (~upper)[None, :]).astype(np.float32)
    m[nlev] = (idx[None, :] <= idx[:, None]).astype(np.float32)
    m[nlev + 1] = (idx[None, :] > idx[:, None]).astype(np.float32)
    m = m.reshape((nlev + 2) * L, L)
    return nlev, np.concatenate([m, m], axis=1), wm


def _gla_prompt_kernel(qkv_ref, la_ref, sg_ref, mall_ref, wm_ref, gn_ref,
                       o_ref, snew_ref, s_ref, *, nlev, nsteps, chunks_per_step):
    c = pl.program_id(1)

    @pl.when(c == 0)
    def _():
        s_ref[...] = jnp.zeros_like(s_ref)

    L = GLA_CHUNK
    scale = HEAD_K ** -0.5
    gn = gn_ref[...]
    chunks = [slice(sub * L, (sub + 1) * L) for sub in range(chunks_per_step)]
    pairs = [(rows, h) for rows in chunks for h in range(N_HEADS)]

    def q_of(rows, h):
        return qkv_ref[rows, h * HEAD_K:(h + 1) * HEAD_K] * scale

    def k_of(rows, h):
        return qkv_ref[rows, D_K + h * HEAD_K:D_K + (h + 1) * HEAD_K]

    mall2 = mall_ref[...]
    e_all = {}
    for rows in chunks:
        la = la_ref[rows, :]
        la_hi = la.astype(BF16)
        la_lo = (la - la_hi.astype(F32)).astype(BF16)
        e_all[rows.start] = _dot(mall2, jnp.concatenate([la_hi, la_lo], axis=0))

    def scaled(rows, h, lev):
        if lev == 0:
            return q_of(rows, h).astype(BF16), k_of(rows, h).astype(BF16)
        e = jnp.exp(e_all[rows.start][lev * L:(lev + 1) * L, h * HEAD_K:(h + 1) * HEAD_K])
        return (q_of(rows, h) * e).astype(BF16), (k_of(rows, h) * e).astype(BF16)

    assert len(pairs) % 2 == 0
    zero = jnp.zeros((L, HEAD_K), BF16)
    amat = {}
    for lev in range(nlev):
        for pa, pb in zip(pairs[0::2], pairs[1::2]):
            qa, ka = scaled(*pa, lev)
            qb, kb = scaled(*pb, lev)
            lhs = jnp.concatenate([qa, qb], axis=1)
            rhs_t = jnp.concatenate([jnp.concatenate([ka, zero], axis=1),
                                     jnp.concatenate([zero, kb], axis=1)], axis=0)
            p = _dot_nt(lhs, rhs_t)
            for key, blk in (((pa[0].start, pa[1]), p[:, 0:L]), ((pb[0].start, pb[1]), p[:, L:2 * L])):
                term = wm_ref[lev] * blk
                amat[key] = term if lev == 0 else amat[key] + term

    for rows, h in pairs:
        ks = slice(h * HEAD_K, (h + 1) * HEAD_K)
        vs = slice(h * HEAD_V, (h + 1) * HEAD_V)
        q, k, a = q_of(rows, h), k_of(rows, h), amat[rows.start, h]
        vb = qkv_ref[rows, 2 * D_K + h * HEAD_V:2 * D_K + (h + 1) * HEAD_V].astype(BF16)

        eb = e_all[rows.start][nlev * L:(nlev + 1) * L, ks]
        ek = e_all[rows.start][(nlev + 1) * L:(nlev + 2) * L, ks]
        qh = (q * jnp.exp(eb)).astype(BF16)
        kh = k * jnp.exp(ek)
        s = s_ref[h]
        o = _dot(jnp.concatenate([a.astype(BF16), qh], axis=1),
                 jnp.concatenate([vb, s.astype(BF16)], axis=0))

        blast = jnp.broadcast_to(eb[L - 1:L, :], (HEAD_K, HEAD_K))
        acol = jnp.exp(blast.T)
        acol = jnp.concatenate([acol, acol], axis=1)
        s_ref[h] = acol * s + _dot(kh.T.astype(BF16), vb)

        y = _rms_rows(o, gn) * sg_ref[rows, vs]
        o_ref[rows, vs] = y.astype(o_ref.dtype)

    @pl.when(c == nsteps - 1)
    def _():
        snew_ref[0, 0] = s_ref[...]


def _gla_sample_kernel(la_ref, qkv_ref, sg_ref, st_ref, gn_ref, o_ref, snew_ref, at_ref, *, bblk):
    s_idx = pl.program_id(0)

    @pl.when(s_idx == 0)
    def _():
        for h in range(N_HEADS):
            at_ref[h] = jnp.exp(la_ref[:, h * HEAD_K:(h + 1) * HEAD_K]).T

    gn = gn_ref[...]
    scale = HEAD_K ** -0.5
    lane = lax.broadcasted_iota(jnp.int32, (HEAD_K, LANES), 1)
    row = lax.broadcasted_iota(jnp.int32, (bblk, HEAD_V), 0)
    for h in range(N_HEADS):
        vs = slice(h * HEAD_V, (h + 1) * HEAD_V)
        q = (qkv_ref[:, h * HEAD_K:(h + 1) * HEAD_K] * scale).astype(BF16)
        kt = qkv_ref[:, D_K + h * HEAD_K:D_K + (h + 1) * HEAD_K].T.astype(BF16)
        v = qkv_ref[:, 2 * D_K + h * HEAD_V:2 * D_K + (h + 1) * HEAD_V].astype(BF16)
        vdiag = jnp.concatenate([jnp.where(row == j, v, jnp.zeros_like(v)) for j in range(bblk)], axis=1)
        kv = _dot(kt, vdiag)
        outs = []
        for j in range(bblk):
            b = s_idx * bblk + j
            acol = jnp.sum(jnp.where(lane == b, at_ref[h], 0.0), axis=1, keepdims=True)
            s_new = acol * st_ref[j, h] + kv[:, j * HEAD_V:(j + 1) * HEAD_V]
            snew_ref[0, j, h] = s_new
            outs.append(_dot(q, s_new.astype(BF16))[j:j + 1, :])
        for j in range(bblk):
            o_ref[j:j + 1, vs] = _rms_rows(outs[j], gn) * sg_ref[j:j + 1, vs]


def _ln_silu(y, g, b):
    mu = jnp.mean(y, axis=-1, keepdims=True)
    yc = y - mu
    var = jnp.mean(yc * yc, axis=-1, keepdims=True)
    z = yc * lax.rsqrt(var + LN_EPS) * g + b
    return z * jax.nn.sigmoid(z)


def _conv_sample_kernel(sc_ref, glu_ref, cw_ref, cb_ref, lg_ref, lb_ref, c_ref, cnew_ref):
    nh = CONV_W - 1
    g = glu_ref[...]
    y = g * cw_ref[nh:nh + 1, :] + cb_ref[...]
    for w in range(nh):
        y = y + sc_ref[w] * cw_ref[w:w + 1, :]
    c_ref[...] = _ln_silu(y, lg_ref[...], lb_ref[...])
    for w in range(nh - 1):
        cnew_ref[w] = sc_ref[w + 1]
    cnew_ref[nh - 1] = g


def _fixed(shape, idx=None):
    idx = (0,) * len(shape) if idx is None else idx
    return pl.BlockSpec(shape, lambda *_: idx)


def kernel(x_prompt, x_sample, state_gla, state_conv, norm_mix, w_in, w_gate_up, b_gate, gla_norm,
           conv_w, conv_b, conv_ln_g, conv_ln_b, w_out, norm_ffn, w_ffn_in, w_ffn_out, norm_final):
    B, T, D = x_prompt.shape
    BS = x_sample.shape[0]
    assert state_gla.shape[0] == 1 and x_sample.shape[1] == 1
    d_conv = state_conv.shape[-1]
    d_ff = w_ffn_out.shape[1]
    in_cols = w_in.shape[-1]
    assert d_conv == D_A and in_cols == 2 * D_K + D_A + GATE_RANK + D_A + 2 * d_conv
    MP = B * T
    n_tiles = 8
    tmp, tms = MP // n_tiles, BS // n_tiles
    assert tmp * n_tiles == MP and tms * n_tiles == BS and tms % 8 == 0 and tmp % 256 == 0
    assert T % tmp == 0 and tmp % CONV_HIST == 0 and MXU_N % GATE_RANK == 0
    tm = tmp + tms
    tiles_per_seq = T // tmp
    nslab = d_conv // LANES

    xp = x_prompt.reshape(MP, D)
    xs = x_sample.reshape(BS, D)
    gain_mix = norm_mix[0].reshape(1, D)
    vec = lambda a: a.reshape(1, d_conv)

    def pair_shapes(cols):
        return (jax.ShapeDtypeStruct((MP, cols), F32), jax.ShapeDtypeStruct((BS, cols), F32))

    wt = jnp.swapaxes(w_in, 1, 2)[0]
    n_all = (in_cols - GATE_RANK) // MXU_N
    nw = (2 * D_K + 2 * D_A) // MXU_N
    nwg = n_all - nw
    assert n_all * MXU_N == in_cols - GATE_RANK and nwg * MXU_N == 2 * d_conv
    next_rows = lambda blk: (blk + 1) * (MXU_N // GATE_RANK)

    g_blk = lambda s: nw + jnp.minimum(s, nwg - 1)
    g_row = lambda s: jnp.maximum(s - nwg, 0) // 2
    g_col = lambda s: lax.rem(jnp.maximum(s - nwg, 0), 2)
    glu_p, glu_s = pl.pallas_call(
        functools.partial(_glu_proj_kernel, nw=nwg, tmp=tmp, tms=tms),
        grid=(nwg + n_tiles * 2,),
        in_specs=[pl.BlockSpec((MXU_N, D), lambda s: (g_blk(s), 0)),
                  pl.BlockSpec((GATE_RANK, D), lambda s: (next_rows(g_blk(s)), 0)),
                  pl.BlockSpec((tmp, D), lambda s: (g_row(s), 0)),
                  pl.BlockSpec((tms, D), lambda s: (g_row(s), 0)),
                  _fixed((1, D))],
        out_specs=(pl.BlockSpec((tmp, ZB), lambda s: (g_row(s), g_col(s))),
                   pl.BlockSpec((tms, ZB), lambda s: (g_row(s), g_col(s)))),
        out_shape=pair_shapes(d_conv),
        scratch_shapes=[pltpu.VMEM((nwg, D, MXU_N), BF16), pltpu.VMEM((tm, D), BF16)],
        compiler_params=_cparams(("arbitrary",)),
        name="glu_proj",
    )(wt, wt, xp, xs, gain_mix)

    nj = Z_COLS // ZB
    nqkv = Z_SG // ZB
    nsg = (Z_LA - Z_SG) // ZB
    step_r = lambda s: jnp.maximum(s - nw, 0)
    row_i = lambda s: step_r(s) // nj
    col_j = lambda s: lax.rem(step_r(s), nj)
    wblk = lambda s: jnp.minimum(s, nw - 1)
    slab_a = lambda s: 2 * jnp.minimum(col_j(s), nqkv - 1)
    hist_row = lambda s: jnp.maximum(row_i(s) * (tmp // CONV_HIST) - 1, 0)
    c_row = lambda s: nsg * row_i(s) + jnp.clip(col_j(s) - nqkv, 0, nsg - 1)
    z_p, z_s, c_p, conv_new_p = pl.pallas_call(
        functools.partial(_in_proj_kernel, nw=nw, nj=nj, tiles_per_seq=tiles_per_seq, tmp=tmp, tms=tms),
        grid=(nw + n_tiles * nj,),
        in_specs=[pl.BlockSpec((MXU_N, D), lambda s: (wblk(s), 0), pipeline_mode=pl.Buffered(1)),
                  pl.BlockSpec((GATE_RANK, D), lambda s: (next_rows(wblk(s)), 0)),
                  pl.BlockSpec((tmp, D), lambda s: (row_i(s), 0)),
                  pl.BlockSpec((tms, D), lambda s: (row_i(s), 0)),
                  _fixed((1, D)), _fixed((GATE_RANK, D_K)), _fixed((1, D_K)),
                  pl.BlockSpec((tmp, LANES), lambda s: (row_i(s), slab_a(s))),
                  pl.BlockSpec((tmp, LANES), lambda s: (row_i(s), slab_a(s) + 1)),
                  pl.BlockSpec((CONV_HIST, LANES), lambda s: (hist_row(s), slab_a(s))),
                  pl.BlockSpec((CONV_HIST, LANES), lambda s: (hist_row(s), slab_a(s) + 1)),
                  _fixed((CONV_W, nslab, LANES)), _fixed((nslab, LANES)),
                  _fixed((1, d_conv)), _fixed((1, d_conv))],
        out_specs=(pl.BlockSpec((tmp, ZB), lambda s: (row_i(s), col_j(s))),
                   pl.BlockSpec((tms, ZB), lambda s: (row_i(s), col_j(s))),
                   pl.BlockSpec((tmp // nsg, d_conv), lambda s: (c_row(s), 0)),
                   pl.BlockSpec((1, 1, CONV_W - 1, d_conv), lambda s: (0, row_i(s) // tiles_per_seq, 0, 0))),
        out_shape=(*pair_shapes(Z_COLS), jax.ShapeDtypeStruct((MP, d_conv), BF16),
                   jax.ShapeDtypeStruct((1, B, CONV_W - 1, d_conv), F32)),
        scratch_shapes=[pltpu.VMEM((nw, D, MXU_N), BF16), pltpu.VMEM((D, LANES), BF16),
                        pltpu.VMEM((tm, D), BF16),
                        pltpu.VMEM((2, CONV_HIST + tmp, LANES), F32), pltpu.VMEM((nslab, tmp, LANES), F32),
                        pltpu.VMEM((nslab, CONV_HIST, LANES), F32)],
        compiler_params=_cparams(("arbitrary",)),
        name="in_proj",
    )(wt, wt, xp, xs, gain_mix, w_gate_up[0], b_gate[0].reshape(1, D_K),
      glu_p, glu_p, glu_p, glu_p, conv_w[0].reshape(CONV_W, nslab, LANES), conv_b[0].reshape(nslab, LANES),
      vec(conv_ln_g[0]), vec(conv_ln_b[0]))

    chunks_per_step = 4
    L = GLA_CHUNK * chunks_per_step
    nchunks = T // L
    assert nchunks * L == T
    nlev, mall, wm = _gla_constants(GLA_CHUNK)
    gn = gla_norm[0].reshape(1, HEAD_V)
    zrow = lambda b, c: b * nchunks + c
    oa_p, gla_new_p = pl.pallas_call(
        functools.partial(_gla_prompt_kernel, nlev=nlev, nsteps=nchunks, chunks_per_step=chunks_per_step),
        grid=(B, nchunks),
        in_specs=[pl.BlockSpec((L, Z_SG), lambda b, c: (zrow(b, c), 0)),
                  pl.BlockSpec((L, D_K), lambda b, c: (zrow(b, c), Z_LA // D_K)),
                  pl.BlockSpec((L, D_A), lambda b, c: (zrow(b, c), Z_SG // D_A)),
                  _fixed(mall.shape), _fixed(wm.shape),
                  _fixed((1, HEAD_V))],
        out_specs=(pl.BlockSpec((L, D_A), lambda b, c: (zrow(b, c), 0)),
                   pl.BlockSpec((1, 1, N_HEADS, HEAD_K, HEAD_V), lambda b, c: (0, b, 0, 0, 0))),
        out_shape=(jax.ShapeDtypeStruct((MP, D_A), BF16),
                   jax.ShapeDtypeStruct((1, B, N_HEADS, HEAD_K, HEAD_V), F32)),
        scratch_shapes=[pltpu.VMEM((N_HEADS, HEAD_K, HEAD_V), F32)],
        compiler_params=_cparams(("arbitrary", "arbitrary")),
        name="gla_prompt",
    )(z_p, z_p, z_p, jnp.asarray(mall, BF16), jnp.asarray(wm), gn)

    bblk = 8
    assert BS == LANES and BS % bblk == 0
    oa_s, gla_new_s = pl.pallas_call(
        functools.partial(_gla_sample_kernel, bblk=bblk),
        grid=(BS // bblk,),
        in_specs=[_fixed((BS, D_K), (0, Z_LA // D_K)),
                  pl.BlockSpec((bblk, Z_SG), lambda s: (s, 0)),
                  pl.BlockSpec((bblk, D_A), lambda s: (s, Z_SG // D_A)),
                  pl.BlockSpec((bblk, N_HEADS, HEAD_K, HEAD_V), lambda s: (s, 0, 0, 0)),
                  _fixed((1, HEAD_V))],
        out_specs=(pl.BlockSpec((bblk, D_A), lambda s: (s, 0)),
                   pl.BlockSpec((1, bblk, N_HEADS, HEAD_K, HEAD_V), lambda s: (0, s, 0, 0, 0))),
        out_shape=(jax.ShapeDtypeStruct((BS, D_A), F32),
                   jax.ShapeDtypeStruct((1, BS, N_HEADS, HEAD_K, HEAD_V), F32)),
        scratch_shapes=[pltpu.VMEM((N_HEADS, HEAD_K, BS), F32)],
        compiler_params=_cparams(("arbitrary",)),
        name="gla_sample",
    )(z_s, z_s, z_s, state_gla[0], gn)

    cb_blk = 32
    sc_t = jnp.transpose(state_conv[0], (1, 0, 2))
    c_s, conv_new_s_t = pl.pallas_call(
        _conv_sample_kernel,
        grid=(BS // cb_blk,),
        in_specs=[pl.BlockSpec((CONV_W - 1, cb_blk, d_conv), lambda s: (0, s, 0)),
                  pl.BlockSpec((cb_blk, d_conv), lambda s: (s, 0)),
                  _fixed((CONV_W, d_conv)), _fixed((1, d_conv)), _fixed((1, d_conv)), _fixed((1, d_conv))],
        out_specs=(pl.BlockSpec((cb_blk, d_conv), lambda s: (s, 0)),
                   pl.BlockSpec((CONV_W - 1, cb_blk, d_conv), lambda s: (0, s, 0))),
        out_shape=(jax.ShapeDtypeStruct((BS, d_conv), F32),
                   jax.ShapeDtypeStruct((CONV_W - 1, BS, d_conv), F32)),
        compiler_params=_cparams(("arbitrary",)),
        name="conv_sample",
    )(sc_t, glu_s, conv_w[0], vec(conv_b[0]), vec(conv_ln_g[0]), vec(conv_ln_b[0]))
    conv_new_s = jnp.transpose(conv_new_s_t, (1, 0, 2))[None]

    nwo = D // MXU_N
    njo = D // ZB
    row_o = lambda s: jnp.maximum(s - nwo, 0) // njo
    col_o = lambda s: lax.rem(jnp.maximum(s - nwo, 0), njo)
    x1_p, x1_s = pl.pallas_call(
        functools.partial(_outproj_kernel, nw=nwo, nj=njo, tmp=tmp, tms=tms),
        grid=(nwo + n_tiles * njo,),
        in_specs=[pl.BlockSpec((D_A + d_conv, MXU_N), lambda s: (0, jnp.minimum(s, nwo - 1))),
                  pl.BlockSpec((tmp, ZB), lambda s: (row_o(s), col_o(s))),
                  pl.BlockSpec((tms, ZB), lambda s: (row_o(s), col_o(s))),
                  pl.BlockSpec((tmp, D_A), lambda s: (row_o(s), 0)),
                  pl.BlockSpec((tms, D_A), lambda s: (row_o(s), 0)),
                  pl.BlockSpec((tmp, d_conv), lambda s: (row_o(s), 0)),
                  pl.BlockSpec((tms, d_conv), lambda s: (row_o(s), 0))],
        out_specs=(pl.BlockSpec((tmp, ZB), lambda s: (row_o(s), col_o(s))),
                   pl.BlockSpec((tms, ZB), lambda s: (row_o(s), col_o(s)))),
        out_shape=pair_shapes(D),
        scratch_shapes=[pltpu.VMEM((nwo, D_A + d_conv, MXU_N), BF16),
                        pltpu.VMEM((tm, D_A), BF16), pltpu.VMEM((tm, d_conv), BF16)],
        compiler_params=_cparams(("arbitrary",)),
        name="out_proj",
    )(w_out[0], xp, xs, oa_p, oa_s, c_p, c_s)

    tf = 256
    nf = d_ff // tf
    assert nf * tf == d_ff
    y_p, y_s = pl.pallas_call(
        functools.partial(_ffn_kernel, tmp=tmp, tms=tms, nf=nf),
        grid=(n_tiles, nf),
        in_specs=[pl.BlockSpec((tmp, D), lambda i, j: (i, 0)), pl.BlockSpec((tms, D), lambda i, j: (i, 0)),
                  _fixed((1, D)), _fixed((1, D)),
                  pl.BlockSpec((D, tf), lambda i, j: (0, j)),
                  pl.BlockSpec((D, tf), lambda i, j: (0, j + nf)),
                  pl.BlockSpec((tf, D), lambda i, j: (j, 0))],
        out_specs=(pl.BlockSpec((tmp, D), lambda i, j: (i, 0)), pl.BlockSpec((tms, D), lambda i, j: (i, 0))),
        out_shape=pair_shapes(D),
        scratch_shapes=[pltpu.VMEM((tm, D), BF16)],
        compiler_params=_cparams(("arbitrary", "arbitrary")),
        name="ffn",
    )(x1_p, x1_s, norm_ffn[0].reshape(1, D), norm_final.reshape(1, D), w_ffn_in[0], w_ffn_in[0], w_ffn_out[0])

    return (y_p.reshape(B, T, D), y_s.reshape(BS, 1, D), gla_new_p, conv_new_p, gla_new_s, conv_new_s)
```

```python
import functools

import numpy as np
import jax
import jax.numpy as jnp
from jax import lax
from jax.experimental import pallas as pl
from jax.experimental.pallas import tpu as pltpu

F32 = jnp.float32
BF16 = jnp.bfloat16

RMS_EPS = 1e-6
LN_EPS = 1e-5
GATE_NORM = 16.0
N_HEADS = 4
HEAD_K = 128
HEAD_V = 256
D_K = N_HEADS * HEAD_K
D_A = N_HEADS * HEAD_V
GATE_RANK = 16
CONV_W = 31

LANES = 128
MXU_N = 256
GLA_CHUNK = 128
VMEM_LIMIT = 58 * 1024 * 1024
CONV_ROWS = 64
CONV_HIST = 32

ZB = 512
Z_QKV, Z_SG, Z_LA = 0, 2 * D_K + D_A, 2 * D_K + 2 * D_A
Z_COLS = Z_LA + D_K


def _cparams(sem):
    return pltpu.CompilerParams(dimension_semantics=sem, vmem_limit_bytes=VMEM_LIMIT)


def _dot(a, b):
    return jnp.dot(a, b, preferred_element_type=F32)


def _dot_nt(a, b):
    return lax.dot_general(a, b, (((1,), (1,)), ((), ())), preferred_element_type=F32)


def _rms_rows(x, gain):
    ms = jnp.mean(x * x, axis=-1, keepdims=True)
    return x * lax.rsqrt(ms + RMS_EPS) * gain


def _log_sigmoid(x):
    return jnp.minimum(x, 0.0) - jnp.log(1.0 + jnp.exp(-jnp.abs(x)))


def _norm_to_scratch(h_ref, xp_ref, xs_ref, gain, tmp, tms, rows_per_pass=256):
    for r0 in range(0, tmp, rows_per_pass):
        h_ref[r0:r0 + rows_per_pass, :] = _rms_rows(xp_ref[r0:r0 + rows_per_pass, :], gain).astype(BF16)
    h_ref[tmp:tmp + tms, :] = _rms_rows(xs_ref[...], gain).astype(BF16)


def _shifted_weight_block(wt_ref, wnext_ref):
    rows = jnp.concatenate([wt_ref[GATE_RANK:MXU_N, :], wnext_ref[...]], axis=0)
    return rows.T.astype(BF16)


def _glu_proj_kernel(wt_ref, wnext_ref, xp_ref, xs_ref, g_ref, gp_ref, gs_ref, wb_ref, h_ref,
                     *, nw, tmp, tms):
    s = pl.program_id(0)
    rb = MXU_N
    half_blocks = wb_ref.shape[0] // 4

    @pl.when(s < nw)
    def _load_weight():
        wb_ref[s] = _shifted_weight_block(wt_ref, wnext_ref)

    @pl.when(s >= nw)
    def _project():
        j = lax.rem(s - nw, 2)

        @pl.when(j == 0)
        def _():
            _norm_to_scratch(h_ref, xp_ref, xs_ref, g_ref[...], tmp, tms)

        for q in range(half_blocks):
            t = j * half_blocks + q
            h = h_ref[...]
            glu = _dot(h, wb_ref[t]) * jax.nn.sigmoid(_dot(h, wb_ref[2 * half_blocks + t]))
            for u in range(rb // LANES):
                gp_ref[q * (rb // LANES) + u] = glu[:tmp, u * LANES:(u + 1) * LANES]
            gs_ref[:, q * rb:(q + 1) * rb] = glu[tmp:]


def _in_proj_kernel(wt_ref, wnext_ref, xp_ref, xs_ref, g_ref, wup_ref, bg_ref,
                    ga_ref, gb_ref, ha_ref, hb_ref, cw_ref, cb_ref, lg_ref, lb_ref,
                    zp_ref, zs_ref, c_ref, cnew_ref,
                    wb_ref, wlr_ref, h_ref, slab_ref, y_ref, tail_ref,
                    *, nw, nj, tiles_per_seq, tmp, tms):
    s = pl.program_id(0)
    rb = MXU_N
    n_direct = (2 * D_K + D_A) // rb
    nslab = y_ref.shape[0]
    hist = CONV_HIST
    pad = hist - (CONV_W - 1)

    @pl.when(s < n_direct)
    def _():
        wb_ref[s] = wt_ref[...].T.astype(BF16)

    @pl.when((s >= n_direct) & (s < nw))
    def _():
        wb_ref[s] = _shifted_weight_block(wt_ref, wnext_ref)

    @pl.when(s == n_direct)
    def _():
        wlr_ref[...] = wt_ref[0:LANES, :].T.astype(BF16)

    @pl.when(s >= nw)
    def _project():
        r = s - nw
        i = r // nj
        j = lax.rem(r, nj)
        pos = lax.rem(i, tiles_per_seq)

        @pl.when(j == 0)
        def _():
            _norm_to_scratch(h_ref, xp_ref, xs_ref, g_ref[...], tmp, tms)

        def put(z):
            zp_ref[...] = z[:tmp]
            zs_ref[...] = z[tmp:]

        nqkv, nsg = Z_SG // ZB, (Z_LA - Z_SG) // ZB
        bpz = ZB // rb
        assert nslab == 2 * nqkv and tmp % (nsg * 256) == 0

        def conv_task(q, r0):
            def run():
                sl = 2 * j + q
                acc = None
                for res in range(8):
                    part = None
                    for w in range(res, CONV_W, 8):
                        term = (slab_ref[q, pad + w + r0:pad + w + r0 + CONV_ROWS, :]
                                * cw_ref[w, pl.ds(sl, 1), :])
                        part = term if part is None else part + term
                    acc = part if acc is None else acc + part
                y_ref[sl, r0:r0 + CONV_ROWS, :] = acc + cb_ref[pl.ds(sl, 1), :]
            return run

        @pl.when(j < nqkv)
        def _():
            @pl.when(pos == 0)
            def _():
                slab_ref[:, 0:hist, :] = jnp.zeros((2, hist, LANES), F32)

            @pl.when(pos != 0)
            def _():
                slab_ref[0, 0:hist, :] = ha_ref[...]
                slab_ref[1, 0:hist, :] = hb_ref[...]

            for q, glu_ref in enumerate((ga_ref, gb_ref)):
                slab_ref[q, hist:hist + tmp, :] = glu_ref[...]
                tail_ref[2 * j + q] = glu_ref[tmp - hist:tmp, :]

            conv = [conv_task(q, r0) for q in range(2) for r0 in range(0, tmp, CONV_ROWS)]
            per = -(-len(conv) // bpz)
            outs = []
            for t in range(bpz):
                for task in conv[t * per:(t + 1) * per]:
                    task()
                outs.append(_dot(h_ref[...], wb_ref[bpz * j + t]))
            put(jnp.concatenate(outs, axis=1))

        @pl.when((j >= nqkv) & (j < nqkv + nsg))
        def _():
            half = j - nqkv
            rows_c = tmp // nsg
            inv_c = 1.0 / (nslab * LANES)
            chunks = list(range(0, rows_c, 256))
            per = -(-len(chunks) // bpz)
            outs = []
            for t in range(bpz):
                outs.append(_dot(h_ref[...], wb_ref[bpz * j + t]))
                for r0 in chunks[t * per:(t + 1) * per]:
                    y = y_ref[:, pl.ds(pl.multiple_of(half * rows_c + r0, 256), 256), :]
                    mu = jnp.sum(jnp.sum(y, axis=0), axis=-1, keepdims=True) * inv_c
                    yc = y - mu[None]
                    var = jnp.sum(jnp.sum(yc * yc, axis=0), axis=-1, keepdims=True) * inv_c
                    rstd = lax.rsqrt(var + LN_EPS)
                    for sl in range(nslab):
                        cs = slice(sl * LANES, (sl + 1) * LANES)
                        v = yc[sl] * rstd * lg_ref[:, cs] + lb_ref[:, cs]
                        c_ref[r0:r0 + 256, cs] = (v * jax.nn.sigmoid(v)).astype(c_ref.dtype)
            z = jnp.concatenate(outs, axis=1)
            put(z * jax.nn.sigmoid(z))

        @pl.when(j == nqkv + nsg)
        def _():
            glr = _dot(h_ref[...], wlr_ref[...])[:, 0:GATE_RANK]
            logit = _dot(glr, wup_ref[...]) + bg_ref[...]
            put(_log_sigmoid(logit) * (1.0 / GATE_NORM))

            @pl.when(pos == tiles_per_seq - 1)
            def _():
                for sl in range(nslab):
                    cnew_ref[0, 0, :, sl * LANES:(sl + 1) * LANES] = tail_ref[sl, hist - (CONV_W - 1):hist, :]


def _outproj_kernel(w_ref, xp_ref, xs_ref, oap_ref, oas_ref, cp_ref, cs_ref, op_ref, os_ref,
                    wb_ref, a_ref, c_ref, *, nw, nj, tmp, tms):
    s = pl.program_id(0)
    rb = MXU_N

    @pl.when(s < nw)
    def _load_weight():
        wb_ref[s] = w_ref[...].astype(BF16)

    @pl.when(s >= nw)
    def _project():
        j = lax.rem(s - nw, nj)

        @pl.when(j == 0)
        def _():
            a_ref[0:tmp, :] = oap_ref[...]
            a_ref[tmp:tmp + tms, :] = oas_ref[...].astype(BF16)
            c_ref[0:tmp, :] = cp_ref[...]
            c_ref[tmp:tmp + tms, :] = cs_ref[...].astype(BF16)

        for half in range(ZB // rb):
            t = (ZB // rb) * j + half
            cols = slice(half * rb, (half + 1) * rb)
            z = _dot(a_ref[...], wb_ref[t, 0:D_A, :]) + _dot(c_ref[...], wb_ref[t, D_A:, :])
            op_ref[:, cols] = xp_ref[:, cols] + z[:tmp]
            os_ref[:, cols] = xs_ref[:, cols] + z[tmp:]


def _ffn_kernel(xp_ref, xs_ref, g_ref, gf_ref, wg_ref, wu_ref, wo_ref, op_ref, os_ref, h_ref,
                *, tmp, tms, nf):
    j = pl.program_id(1)

    @pl.when(j == 0)
    def _():
        _norm_to_scratch(h_ref, xp_ref, xs_ref, g_ref[...], tmp, tms)
        op_ref[...] = xp_ref[...]
        os_ref[...] = xs_ref[...]

    h = h_ref[...]
    fg = _dot(h, wg_ref[...].astype(BF16))
    fu = _dot(h, wu_ref[...].astype(BF16))
    act = (fg * jax.nn.sigmoid(fg) * fu).astype(BF16)
    wo = wo_ref[...].astype(BF16)
    half = tmp // 2
    op_ref[0:half, :] += _dot(act[0:half], wo)
    z = _dot(act[half:], wo)
    op_ref[half:tmp, :] += z[:tmp - half]
    os_ref[...] += z[tmp - half:]

    @pl.when(j == nf - 1)
    def _():
        gf = gf_ref[...]
        for r0 in range(0, tmp, 256):
            op_ref[r0:r0 + 256, :] = _rms_rows(op_ref[r0:r0 + 256, :], gf)
        os_ref[...] = _rms_rows(os_ref[...], gf)


@functools.lru_cache(maxsize=None)
def _gla_constants(L):
    nlev = int(np.log2(L)) + 1
    m = np.zeros((nlev + 2, L, L), np.float32)
    wm = np.zeros((nlev, L, L), np.float32)
    idx = np.arange(L)
    wm[0] = np.eye(L, dtype=np.float32)
    for lev in range(1, nlev):
        half = L >> lev
        pos = idx % (2 * half)
        upper = pos >= half
        r = idx - pos + half - 1
        t = idx[None, :]
        up_rows = (t > r[:, None]) & (t <= idx[:, None]) & upper[:, None]
        lo_rows = (t > idx[:, None]) & (t <= r[:, None]) & (~upper)[:, None]
        m[lev] = (up_rows | lo_rows).astype(np.float32)
        same = (idx[:, None] // (2 * half)) == (idx[None, :] // (2 * half))
        wm[lev] = (same & upper[:, None] & (~upper)[None, :]).astype(np.float32)
    m[nlev] = (idx[None, :] <= idx[:, None]).astype(np.float32)
    m[nlev + 1] = (idx[None, :] > idx[:, None]).astype(np.float32)
    m = m.reshape((nlev + 2) * L, L)
    return nlev, np.concatenate([m, m], axis=1), wm


def _gla_prompt_kernel(qkv_ref, la_ref, sg_ref, mall_ref, wm_ref, gn_ref,
                       o_ref, snew_ref, s_ref, *, nlev, nsteps, chunks_per_step):
    c = pl.program_id(1)

    @pl.when(c == 0)
    def _():
        s_ref[...] = jnp.zeros_like(s_ref)

    L = GLA_CHUNK
    scale = HEAD_K ** -0.5
    gn = gn_ref[...]
    chunks = [slice(sub * L, (sub + 1) * L) for sub in range(chunks_per_step)]
    pairs = [(rows, h) for rows in chunks for h in range(N_HEADS)]

    def q_of(rows, h):
        return qkv_ref[rows, h * HEAD_K:(h + 1) * HEAD_K] * scale

    def k_of(rows, h):
        return qkv_ref[rows, D_K + h * HEAD_K:D_K + (h + 1) * HEAD_K]

    mall2 = mall_ref[...]
    e_all = {}
    for rows in chunks:
        la = la_ref[rows, :]
        la_hi = la.astype(BF16)
        la_lo = (la - la_hi.astype(F32)).astype(BF16)
        e_all[rows.start] = _dot(mall2, jnp.concatenate([la_hi, la_lo], axis=0))

    def scaled(rows, h, lev):
        if lev == 0:
            return q_of(rows, h).astype(BF16), k_of(rows, h).astype(BF16)
        e = jnp.exp(e_all[rows.start][lev * L:(lev + 1) * L, h * HEAD_K:(h + 1) * HEAD_K])
        return (q_of(rows, h) * e).astype(BF16), (k_of(rows, h) * e).astype(BF16)

    assert len(pairs) % 2 == 0
    zero = jnp.zeros((L, HEAD_K), BF16)
    amat = {}
    for lev in range(nlev):
        for pa, pb in zip(pairs[0::2], pairs[1::2]):
            qa, ka = scaled(*pa, lev)
            qb, kb = scaled(*pb, lev)
            lhs = jnp.concatenate([qa, qb], axis=1)
            rhs_t = jnp.concatenate([jnp.concatenate([ka, zero], axis=1),
                                     jnp.concatenate([zero, kb], axis=1)], axis=0)
            p = _dot_nt(lhs, rhs_t)
            for key, blk in (((pa[0].start, pa[1]), p[:, 0:L]), ((pb[0].start, pb[1]), p[:, L:2 * L])):
                term = wm_ref[lev] * blk
                amat[key] = term if lev == 0 else amat[key] + term

    for rows, h in pairs:
        ks = slice(h * HEAD_K, (h + 1) * HEAD_K)
        vs = slice(h * HEAD_V, (h + 1) * HEAD_V)
        q, k, a = q_of(rows, h), k_of(rows, h), amat[rows.start, h]
        vb = qkv_ref[rows, 2 * D_K + h * HEAD_V:2 * D_K + (h + 1) * HEAD_V].astype(BF16)

        eb = e_all[rows.start][nlev * L:(nlev + 1) * L, ks]
        ek = e_all[rows.start][(nlev + 1) * L:(nlev + 2) * L, ks]
        qh = (q * jnp.exp(eb)).astype(BF16)
        kh = k * jnp.exp(ek)
        s = s_ref[h]
        o = _dot(jnp.concatenate([a.astype(BF16), qh], axis=1),
                 jnp.concatenate([vb, s.astype(BF16)], axis=0))

        blast = jnp.broadcast_to(eb[L - 1:L, :], (HEAD_K, HEAD_K))
        acol = jnp.exp(blast.T)
        acol = jnp.concatenate([acol, acol], axis=1)
        s_ref[h] = acol * s + _dot(kh.T.astype(BF16), vb)

        y = _rms_rows(o, gn) * sg_ref[rows, vs]
        o_ref[rows, vs] = y.astype(o_ref.dtype)

    @pl.when(c == nsteps - 1)
    def _():
        snew_ref[0, 0] = s_ref[...]


def _gla_sample_kernel(la_ref, qkv_ref, sg_ref, st_ref, gn_ref, o_ref, snew_ref, at_ref, *, bblk):
    s_idx = pl.program_id(0)

    @pl.when(s_idx == 0)
    def _():
        for h in range(N_HEADS):
            at_ref[h] = jnp.exp(la_ref[:, h * HEAD_K:(h + 1) * HEAD_K]).T

    gn = gn_ref[...]
    scale = HEAD_K ** -0.5
    lane = lax.broadcasted_iota(jnp.int32, (HEAD_K, LANES), 1)
    row = lax.broadcasted_iota(jnp.int32, (bblk, HEAD_V), 0)
    for h in range(N_HEADS):
        vs = slice(h * HEAD_V, (h + 1) * HEAD_V)
        q = (qkv_ref[:, h * HEAD_K:(h + 1) * HEAD_K] * scale).astype(BF16)
        kt = qkv_ref[:, D_K + h * HEAD_K:D_K + (h + 1) * HEAD_K].T.astype(BF16)
        v = qkv_ref[:, 2 * D_K + h * HEAD_V:2 * D_K + (h + 1) * HEAD_V].astype(BF16)
        vdiag = jnp.concatenate([jnp.where(row == j, v, jnp.zeros_like(v)) for j in range(bblk)], axis=1)
        kv = _dot(kt, vdiag)
        outs = []
        for j in range(bblk):
            b = s_idx * bblk + j
            acol = jnp.sum(jnp.where(lane == b, at_ref[h], 0.0), axis=1, keepdims=True)
            s_new = acol * st_ref[j, h] + kv[:, j * HEAD_V:(j + 1) * HEAD_V]
            snew_ref[0, j, h] = s_new
            outs.append(_dot(q, s_new.astype(BF16))[j:j + 1, :])
        for j in range(bblk):
            o_ref[j:j + 1, vs] = _rms_rows(outs[j], gn) * sg_ref[j:j + 1, vs]


def _ln_silu(y, g, b):
    mu = jnp.mean(y, axis=-1, keepdims=True)
    yc = y - mu
    var = jnp.mean(yc * yc, axis=-1, keepdims=True)
    z = yc * lax.rsqrt(var + LN_EPS) * g + b
    return z * jax.nn.sigmoid(z)


def _conv_sample_kernel(sc_ref, glu_ref, cw_ref, cb_ref, lg_ref, lb_ref, c_ref, cnew_ref):
    nh = CONV_W - 1
    g = glu_ref[...]
    y = g * cw_ref[nh:nh + 1, :] + cb_ref[...]
    for w in range(nh):
        y = y + sc_ref[w] * cw_ref[w:w + 1, :]
    c_ref[...] = _ln_silu(y, lg_ref[...], lb_ref[...])
    for w in range(nh - 1):
        cnew_ref[w] = sc_ref[w + 1]
    cnew_ref[nh - 1] = g


def _fixed(shape, idx=None):
    idx = (0,) * len(shape) if idx is None else idx
    return pl.BlockSpec(shape, lambda *_: idx)


def kernel(x_prompt, x_sample, state_gla, state_conv, norm_mix, w_in, w_gate_up, b_gate, gla_norm,
           conv_w, conv_b, conv_ln_g, conv_ln_b, w_out, norm_ffn, w_ffn_in, w_ffn_out, norm_final):
    B, T, D = x_prompt.shape
    BS = x_sample.shape[0]
    assert state_gla.shape[0] == 1 and x_sample.shape[1] == 1
    d_conv = state_conv.shape[-1]
    d_ff = w_ffn_out.shape[1]
    in_cols = w_in.shape[-1]
    assert d_conv == D_A and in_cols == 2 * D_K + D_A + GATE_RANK + D_A + 2 * d_conv
    MP = B * T
    n_tiles = 8
    tmp, tms = MP // n_tiles, BS // n_tiles
    assert tmp * n_tiles == MP and tms * n_tiles == BS and tms % 8 == 0 and tmp % 256 == 0
    assert T % tmp == 0 and tmp % CONV_HIST == 0 and MXU_N % GATE_RANK == 0
    tm = tmp + tms
    tiles_per_seq = T // tmp
    nslab = d_conv // LANES

    xp = x_prompt.reshape(MP, D)
    xs = x_sample.reshape(BS, D)
    gain_mix = norm_mix[0].reshape(1, D)
    vec = lambda a: a.reshape(1, d_conv)

    def pair_shapes(cols):
        return (jax.ShapeDtypeStruct((MP, cols), F32), jax.ShapeDtypeStruct((BS, cols), F32))

    wt = jnp.swapaxes(w_in, 1, 2)[0]
    n_all = (in_cols - GATE_RANK) // MXU_N
    nw = (2 * D_K + 2 * D_A) // MXU_N
    nwg = n_all - nw
    assert n_all * MXU_N == in_cols - GATE_RANK and nwg * MXU_N == 2 * d_conv
    next_rows = lambda blk: (blk + 1) * (MXU_N // GATE_RANK)

    g_blk = lambda s: nw + jnp.minimum(s, nwg - 1)
    g_row = lambda s: jnp.maximum(s - nwg, 0) // 2
    g_col = lambda s: lax.rem(jnp.maximum(s - nwg, 0), 2)
    glu_p, glu_s = pl.pallas_call(
        functools.partial(_glu_proj_kernel, nw=nwg, tmp=tmp, tms=tms),
        grid=(nwg + n_tiles * 2,),
        in_specs=[pl.BlockSpec((MXU_N, D), lambda s: (g_blk(s), 0)),
                  pl.BlockSpec((GATE_RANK, D), lambda s: (next_rows(g_blk(s)), 0)),
                  pl.BlockSpec((tmp, D), lambda s: (g_row(s), 0)),
                  pl.BlockSpec((tms, D), lambda s: (g_row(s), 0)),
                  _fixed((1, D))],
        out_specs=(pl.BlockSpec((ZB // LANES, tmp, LANES), lambda s: (g_col(s), g_row(s), 0)),
                   pl.BlockSpec((tms, ZB), lambda s: (g_row(s), g_col(s)))),
        out_shape=(jax.ShapeDtypeStruct((nslab, MP, LANES), F32),
                   jax.ShapeDtypeStruct((BS, d_conv), F32)),
        scratch_shapes=[pltpu.VMEM((nwg, D, MXU_N), BF16), pltpu.VMEM((tm, D), BF16)],
        compiler_params=_cparams(("arbitrary",)),
        name="glu_proj",
    )(wt, wt, xp, xs, gain_mix)

    nj = Z_COLS // ZB
    nqkv = Z_SG // ZB
    nsg = (Z_LA - Z_SG) // ZB
    step_r = lambda s: jnp.maximum(s - nw, 0)
    row_i = lambda s: step_r(s) // nj
    col_j = lambda s: lax.rem(step_r(s), nj)
    wblk = lambda s: jnp.minimum(s, nw - 1)
    slab_a = lambda s: 2 * jnp.minimum(col_j(s), nqkv - 1)
    hist_row = lambda s: jnp.maximum(row_i(s) * (tmp // CONV_HIST) - 1, 0)
    c_row = lambda s: nsg * row_i(s) + jnp.clip(col_j(s) - nqkv, 0, nsg - 1)
    z_p, z_s, c_p, conv_new_p = pl.pallas_call(
        functools.partial(_in_proj_kernel, nw=nw, nj=nj, tiles_per_seq=tiles_per_seq, tmp=tmp, tms=tms),
        grid=(nw + n_tiles * nj,),
        in_specs=[pl.BlockSpec((MXU_N, D), lambda s: (wblk(s), 0), pipeline_mode=pl.Buffered(1)),
                  pl.BlockSpec((GATE_RANK, D), lambda s: (next_rows(wblk(s)), 0)),
                  pl.BlockSpec((tmp, D), lambda s: (row_i(s), 0)),
                  pl.BlockSpec((tms, D), lambda s: (row_i(s), 0)),
                  _fixed((1, D)), _fixed((GATE_RANK, D_K)), _fixed((1, D_K)),
                  pl.BlockSpec((None, tmp, LANES), lambda s: (slab_a(s), row_i(s), 0)),
                  pl.BlockSpec((None, tmp, LANES), lambda s: (slab_a(s) + 1, row_i(s), 0)),
                  pl.BlockSpec((None, CONV_HIST, LANES), lambda s: (slab_a(s), hist_row(s), 0)),
                  pl.BlockSpec((None, CONV_HIST, LANES), lambda s: (slab_a(s) + 1, hist_row(s), 0)),
                  _fixed((CONV_W, nslab, LANES)), _fixed((nslab, LANES)),
                  _fixed((1, d_conv)), _fixed((1, d_conv))],
        out_specs=(pl.BlockSpec((tmp, ZB), lambda s: (row_i(s), col_j(s))),
                   pl.BlockSpec((tms, ZB), lambda s: (row_i(s), col_j(s))),
                   pl.BlockSpec((tmp // nsg, d_conv), lambda s: (c_row(s), 0)),
                   pl.BlockSpec((1, 1, CONV_W - 1, d_conv), lambda s: (0, row_i(s) // tiles_per_seq, 0, 0))),
        out_shape=(*pair_shapes(Z_COLS), jax.ShapeDtypeStruct((MP, d_conv), BF16),
                   jax.ShapeDtypeStruct((1, B, CONV_W - 1, d_conv), F32)),
        scratch_shapes=[pltpu.VMEM((nw, D, MXU_N), BF16), pltpu.VMEM((D, LANES), BF16),
                        pltpu.VMEM((tm, D), BF16),
                        pltpu.VMEM((2, CONV_HIST + tmp, LANES), F32), pltpu.VMEM((nslab, tmp, LANES), F32),
                        pltpu.VMEM((nslab, CONV_HIST, LANES), F32)],
        compiler_params=_cparams(("arbitrary",)),
        name="in_proj",
    )(wt, wt, xp, xs, gain_mix, w_gate_up[0], b_gate[0].reshape(1, D_K),
      glu_p, glu_p, glu_p, glu_p, conv_w[0].reshape(CONV_W, nslab, LANES), conv_b[0].reshape(nslab, LANES),
      vec(conv_ln_g[0]), vec(conv_ln_b[0]))

    chunks_per_step = 4
    L = GLA_CHUNK * chunks_per_step
    nchunks = T // L
    assert nchunks * L == T
    nlev, mall, wm = _gla_constants(GLA_CHUNK)
    gn = gla_norm[0].reshape(1, HEAD_V)
    zrow = lambda b, c: b * nchunks + c
    oa_p, gla_new_p = pl.pallas_call(
        functools.partial(_gla_prompt_kernel, nlev=nlev, nsteps=nchunks, chunks_per_step=chunks_per_step),
        grid=(B, nchunks),
        in_specs=[pl.BlockSpec((L, Z_SG), lambda b, c: (zrow(b, c), 0)),
                  pl.BlockSpec((L, D_K), lambda b, c: (zrow(b, c), Z_LA // D_K)),
                  pl.BlockSpec((L, D_A), lambda b, c: (zrow(b, c), Z_SG // D_A)),
                  _fixed(mall.shape), _fixed(wm.shape),
                  _fixed((1, HEAD_V))],
        out_specs=(pl.BlockSpec((L, D_A), lambda b, c: (zrow(b, c), 0)),
                   pl.BlockSpec((1, 1, N_HEADS, HEAD_K, HEAD_V), lambda b, c: (0, b, 0, 0, 0))),
        out_shape=(jax.ShapeDtypeStruct((MP, D_A), BF16),
                   jax.ShapeDtypeStruct((1, B, N_HEADS, HEAD_K, HEAD_V), F32)),
        scratch_shapes=[pltpu.VMEM((N_HEADS, HEAD_K, HEAD_V), F32)],
        compiler_params=_cparams(("arbitrary", "arbitrary")),
        name="gla_prompt",
    )(z_p, z_p, z_p, jnp.asarray(mall, BF16), jnp.asarray(wm), gn)

    bblk = 8
    assert BS == LANES and BS % bblk == 0
    oa_s, gla_new_s = pl.pallas_call(
        functools.partial(_gla_sample_kernel, bblk=bblk),
        grid=(BS // bblk,),
        in_specs=[_fixed((BS, D_K), (0, Z_LA // D_K)),
                  pl.BlockSpec((bblk, Z_SG), lambda s: (s, 0)),
                  pl.BlockSpec((bblk, D_A), lambda s: (s, Z_SG // D_A)),
                  pl.BlockSpec((bblk, N_HEADS, HEAD_K, HEAD_V), lambda s: (s, 0, 0, 0)),
                  _fixed((1, HEAD_V))],
        out_specs=(pl.BlockSpec((bblk, D_A), lambda s: (s, 0)),
                   pl.BlockSpec((1, bblk, N_HEADS, HEAD_K, HEAD_V), lambda s: (0, s, 0, 0, 0))),
        out_shape=(jax.ShapeDtypeStruct((BS, D_A), F32),
                   jax.ShapeDtypeStruct((1, BS, N_HEADS, HEAD_K, HEAD_V), F32)),
        scratch_shapes=[pltpu.VMEM((N_HEADS, HEAD_K, BS), F32)],
        compiler_params=_cparams(("arbitrary",)),
        name="gla_sample",
    )(z_s, z_s, z_s, state_gla[0], gn)

    cb_blk = 32
    sc_t = jnp.transpose(state_conv[0], (1, 0, 2))
    c_s, conv_new_s_t = pl.pallas_call(
        _conv_sample_kernel,
        grid=(BS // cb_blk,),
        in_specs=[pl.BlockSpec((CONV_W - 1, cb_blk, d_conv), lambda s: (0, s, 0)),
                  pl.BlockSpec((cb_blk, d_conv), lambda s: (s, 0)),
                  _fixed((CONV_W, d_conv)), _fixed((1, d_conv)), _fixed((1, d_conv)), _fixed((1, d_conv))],
        out_specs=(pl.BlockSpec((cb_blk, d_conv), lambda s: (s, 0)),
                   pl.BlockSpec((CONV_W - 1, cb_blk, d_conv), lambda s: (0, s, 0))),
        out_shape=(jax.ShapeDtypeStruct((BS, d_conv), F32),
                   jax.ShapeDtypeStruct((CONV_W - 1, BS, d_conv), F32)),
        compiler_params=_cparams(("arbitrary",)),
        name="conv_sample",
    )(sc_t, glu_s, conv_w[0], vec(conv_b[0]), vec(conv_ln_g[0]), vec(conv_ln_b[0]))
    conv_new_s = jnp.transpose(conv_new_s_t, (1, 0, 2))[None]

    nwo = D // MXU_N
    njo = D // ZB
    row_o = lambda s: jnp.maximum(s - nwo, 0) // njo
    col_o = lambda s: lax.rem(jnp.maximum(s - nwo, 0), njo)
    x1_p, x1_s = pl.pallas_call(
        functools.partial(_outproj_kernel, nw=nwo, nj=njo, tmp=tmp, tms=tms),
        grid=(nwo + n_tiles * njo,),
        in_specs=[pl.BlockSpec((D_A + d_conv, MXU_N), lambda s: (0, jnp.minimum(s, nwo - 1))),
                  pl.BlockSpec((tmp, ZB), lambda s: (row_o(s), col_o(s))),
                  pl.BlockSpec((tms, ZB), lambda s: (row_o(s), col_o(s))),
                  pl.BlockSpec((tmp, D_A), lambda s: (row_o(s), 0)),
                  pl.BlockSpec((tms, D_A), lambda s: (row_o(s), 0)),
                  pl.BlockSpec((tmp, d_conv), lambda s: (row_o(s), 0)),
                  pl.BlockSpec((tms, d_conv), lambda s: (row_o(s), 0))],
        out_specs=(pl.BlockSpec((tmp, ZB), lambda s: (row_o(s), col_o(s))),
                   pl.BlockSpec((tms, ZB), lambda s: (row_o(s), col_o(s)))),
        out_shape=pair_shapes(D),
        scratch_shapes=[pltpu.VMEM((nwo, D_A + d_conv, MXU_N), BF16),
                        pltpu.VMEM((tm, D_A), BF16), pltpu.VMEM((tm, d_conv), BF16)],
        compiler_params=_cparams(("arbitrary",)),
        name="out_proj",
    )(w_out[0], xp, xs, oa_p, oa_s, c_p, c_s)

    tf = 256
    nf = d_ff // tf
    assert nf * tf == d_ff
    y_p, y_s = pl.pallas_call(
        functools.partial(_ffn_kernel, tmp=tmp, tms=tms, nf=nf),
        grid=(n_tiles, nf),
        in_specs=[pl.BlockSpec((tmp, D), lambda i, j: (i, 0)), pl.BlockSpec((tms, D), lambda i, j: (i, 0)),
                  _fixed((1, D)), _fixed((1, D)),
                  pl.BlockSpec((D, tf), lambda i, j: (0, j)),
                  pl.BlockSpec((D, tf), lambda i, j: (0, j + nf)),
                  pl.BlockSpec((tf, D), lambda i, j: (j, 0))],
        out_specs=(pl.BlockSpec((tmp, D), lambda i, j: (i, 0)), pl.BlockSpec((tms, D), lambda i, j: (i, 0))),
        out_shape=pair_shapes(D),
        scratch_shapes=[pltpu.VMEM((tm, D), BF16)],
        compiler_params=_cparams(("arbitrary", "arbitrary")),
        name="ffn",
    )(x1_p, x1_s, norm_ffn[0].reshape(1, D), norm_final.reshape(1, D), w_ffn_in[0], w_ffn_in[0], w_ffn_out[0])

    return (y_p.reshape(B, T, D), y_s.reshape(BS, 1, D), gla_new_p, conv_new_p, gla_new_s, conv_new_s)
```

```python
import functools

import numpy as np
import jax
import jax.numpy as jnp
from jax import lax
from jax.experimental import pallas as pl
from jax.experimental.pallas import tpu as pltpu

F32 = jnp.float32
BF16 = jnp.bfloat16

RMS_EPS = 1e-6
LN_EPS = 1e-5
GATE_NORM = 16.0
N_HEADS = 4
HEAD_K = 128
HEAD_V = 256
D_K = N_HEADS * HEAD_K
D_A = N_HEADS * HEAD_V
GATE_RANK = 16
CONV_W = 31

LANES = 128
MXU_N = 256
GLA_CHUNK = 128
VMEM_LIMIT = 58 * 1024 * 1024

ZB = 512
Z_QKV, Z_SG, Z_GLU, Z_LA = 0, 2 * D_K + D_A, 2 * D_K + 2 * D_A, 2 * D_K + 3 * D_A
Z_COLS = Z_LA + D_K


def _cparams(sem):
    return pltpu.CompilerParams(dimension_semantics=sem, vmem_limit_bytes=VMEM_LIMIT)


def _dot(a, b):
    return jnp.dot(a, b, preferred_element_type=F32)


def _dot_nt(a, b):
    return lax.dot_general(a, b, (((1,), (1,)), ((), ())), preferred_element_type=F32)


def _rms_rows(x, gain):
    ms = jnp.mean(x * x, axis=-1, keepdims=True)
    return x * lax.rsqrt(ms + RMS_EPS) * gain


def _log_sigmoid(x):
    return jnp.minimum(x, 0.0) - jnp.log(1.0 + jnp.exp(-jnp.abs(x)))


def _norm_to_scratch(h_ref, xp_ref, xs_ref, gain, tmp, tms, rows_per_pass=256):
    for r0 in range(0, tmp, rows_per_pass):
        h_ref[r0:r0 + rows_per_pass, :] = _rms_rows(xp_ref[r0:r0 + rows_per_pass, :], gain).astype(BF16)
    h_ref[tmp:tmp + tms, :] = _rms_rows(xs_ref[...], gain).astype(BF16)


def _in_proj_kernel(wt_ref, wnext_ref, xp_ref, xs_ref, g_ref, wup_ref, bg_ref, zp_ref, zs_ref,
                    wb_ref, wlr_ref, h_ref, *, nw, nj, tmp, tms):
    s = pl.program_id(0)
    rb = MXU_N
    n_direct = (2 * D_K + D_A) // rb

    @pl.when(s < n_direct)
    def _():
        wb_ref[s] = wt_ref[...].T.astype(BF16)

    @pl.when((s >= n_direct) & (s < nw))
    def _():
        rows = jnp.concatenate([wt_ref[GATE_RANK:rb, :], wnext_ref[...]], axis=0)
        wb_ref[s] = rows.T.astype(BF16)

    @pl.when(s == n_direct)
    def _():
        wlr_ref[...] = wt_ref[0:LANES, :].T.astype(BF16)

    @pl.when(s >= nw)
    def _project():
        j = lax.rem(s - nw, nj)

        @pl.when(j == 0)
        def _():
            _norm_to_scratch(h_ref, xp_ref, xs_ref, g_ref[...], tmp, tms)

        def pair(t0):
            h = h_ref[...]
            return jnp.concatenate([_dot(h, wb_ref[t0]), _dot(h, wb_ref[t0 + 1])], axis=1)

        def put(z):
            zp_ref[...] = z[:tmp]
            zs_ref[...] = z[tmp:]

        nqkv, nsg, nglu = Z_SG // ZB, (Z_GLU - Z_SG) // ZB, (Z_LA - Z_GLU) // ZB
        bpz = ZB // rb

        @pl.when(j < nqkv)
        def _():
            put(pair(bpz * j))

        @pl.when((j >= nqkv) & (j < nqkv + nsg))
        def _():
            z = pair(bpz * j)
            put(z * jax.nn.sigmoid(z))

        @pl.when((j >= nqkv + nsg) & (j < nqkv + nsg + nglu))
        def _():
            ta = bpz * j
            ua = pair(ta)
            ug = pair(ta + bpz * nglu)
            put(ua * jax.nn.sigmoid(ug))

        @pl.when(j == nqkv + nsg + nglu)
        def _():
            glr = _dot(h_ref[...], wlr_ref[...])[:, 0:GATE_RANK]
            logit = _dot(glr, wup_ref[...]) + bg_ref[...]
            put(_log_sigmoid(logit) * (1.0 / GATE_NORM))


def _conv_prompt_kernel(glu_ref, cw_ref, cb_ref, lg_ref, lb_ref, c_ref, cnew_ref, slab_ref, y_ref,
                        *, tt, nsteps, hist):
    t_idx = pl.program_id(1)
    nslab = slab_ref.shape[0]
    pad = hist - (CONV_W - 1)

    @pl.when(t_idx == 0)
    def _():
        slab_ref[:, 0:hist, :] = jnp.zeros((nslab, hist, LANES), F32)

    @pl.when(t_idx > 0)
    def _():
        slab_ref[:, 0:hist, :] = slab_ref[:, tt:tt + hist, :]

    for sl in range(nslab):
        slab_ref[sl, hist:hist + tt, :] = glu_ref[:, sl * LANES:(sl + 1) * LANES]

    def slab_conv(sl, carry):
        acc = None
        for res in range(8):
            part = None
            for w in range(res, CONV_W, 8):
                term = slab_ref[sl, pad + w:pad + w + tt, :] * cw_ref[w, pl.ds(sl, 1), :]
                part = term if part is None else part + term
            acc = part if acc is None else acc + part
        y_ref[sl] = acc + cb_ref[pl.ds(sl, 1), :]
        return carry

    lax.fori_loop(0, nslab, slab_conv, 0)

    y = y_ref[...]
    inv_c = 1.0 / (nslab * LANES)
    mu = jnp.sum(jnp.sum(y, axis=0), axis=-1, keepdims=True) * inv_c
    yc = y - mu[None]
    var = jnp.sum(jnp.sum(yc * yc, axis=0), axis=-1, keepdims=True) * inv_c
    rstd = lax.rsqrt(var + LN_EPS)
    for sl in range(nslab):
        cs = slice(sl * LANES, (sl + 1) * LANES)
        z = yc[sl] * rstd * lg_ref[:, cs] + lb_ref[:, cs]
        c_ref[:, cs] = (z * jax.nn.sigmoid(z)).astype(c_ref.dtype)

    @pl.when(t_idx == nsteps - 1)
    def _():
        for sl in range(nslab):
            cnew_ref[0, 0, :, sl * LANES:(sl + 1) * LANES] = slab_ref[sl, hist + tt - (CONV_W - 1):hist + tt, :]


def _outproj_kernel(w_ref, xp_ref, xs_ref, oap_ref, oas_ref, cp_ref, cs_ref, op_ref, os_ref,
                    wb_ref, a_ref, c_ref, *, nw, nj, tmp, tms):
    s = pl.program_id(0)
    rb = MXU_N

    @pl.when(s < nw)
    def _load_weight():
        wb_ref[s] = w_ref[...].astype(BF16)

    @pl.when(s >= nw)
    def _project():
        j = lax.rem(s - nw, nj)

        @pl.when(j == 0)
        def _():
            a_ref[0:tmp, :] = oap_ref[...]
            a_ref[tmp:tmp + tms, :] = oas_ref[...].astype(BF16)
            c_ref[0:tmp, :] = cp_ref[...]
            c_ref[tmp:tmp + tms, :] = cs_ref[...].astype(BF16)

        for half in range(ZB // rb):
            t = (ZB // rb) * j + half
            cols = slice(half * rb, (half + 1) * rb)
            z = _dot(a_ref[...], wb_ref[t, 0:D_A, :]) + _dot(c_ref[...], wb_ref[t, D_A:, :])
            op_ref[:, cols] = xp_ref[:, cols] + z[:tmp]
            os_ref[:, cols] = xs_ref[:, cols] + z[tmp:]


def _ffn_kernel(xp_ref, xs_ref, g_ref, gf_ref, wg_ref, wu_ref, wo_ref, op_ref, os_ref, h_ref,
                *, tmp, tms, nf):
    j = pl.program_id(1)

    @pl.when(j == 0)
    def _():
        _norm_to_scratch(h_ref, xp_ref, xs_ref, g_ref[...], tmp, tms)
        op_ref[...] = xp_ref[...]
        os_ref[...] = xs_ref[...]

    h = h_ref[...]
    fg = _dot(h, wg_ref[...].astype(BF16))
    fu = _dot(h, wu_ref[...].astype(BF16))
    act = (fg * jax.nn.sigmoid(fg) * fu).astype(BF16)
    wo = wo_ref[...].astype(BF16)
    half = tmp // 2
    op_ref[0:half, :] += _dot(act[0:half], wo)
    z = _dot(act[half:], wo)
    op_ref[half:tmp, :] += z[:tmp - half]
    os_ref[...] += z[tmp - half:]

    @pl.when(j == nf - 1)
    def _():
        gf = gf_ref[...]
        for r0 in range(0, tmp, 256):
            op_ref[r0:r0 + 256, :] = _rms_rows(op_ref[r0:r0 + 256, :], gf)
        os_ref[...] = _rms_rows(os_ref[...], gf)


@functools.lru_cache(maxsize=None)
def _gla_constants(L):
    nlev = int(np.log2(L)) + 1
    m = np.zeros((nlev + 2, L, L), np.float32)
    wm = np.zeros((nlev, L, L), np.float32)
    idx = np.arange(L)
    wm[0] = np.eye(L, dtype=np.float32)
    for lev in range(1, nlev):
        half = L >> lev
        pos = idx % (2 * half)
        upper = pos >= half
        r = idx - pos + half - 1
        t = idx[None, :]
        up_rows = (t > r[:, None]) & (t <= idx[:, None]) & upper[:, None]
        lo_rows = (t > idx[:, None]) & (t <= r[:, None]) & (~upper)[:, None]
        m[lev] = (up_rows | lo_rows).astype(np.float32)
        same = (idx[:, None] // (2 * half)) == (idx[None, :] // (2 * half))
        wm[lev] = (same & upper[:, None] & (~upper)[None, :]).astype(np.float32)
    m[nlev] = (idx[None, :] <= idx[:, None]).astype(np.float32)
    m[nlev + 1] = (idx[None, :] > idx[:, None]).astype(np.float32)
    m = m.reshape((nlev + 2) * L, L)
    return nlev, np.concatenate([m, m], axis=1), wm


def _gla_prompt_kernel(qkv_ref, la_ref, sg_ref, mall_ref, wm_ref, gn_ref,
                       o_ref, snew_ref, s_ref, *, nlev, nsteps, chunks_per_step):
    c = pl.program_id(1)

    @pl.when(c == 0)
    def _():
        s_ref[...] = jnp.zeros_like(s_ref)

    L = GLA_CHUNK
    scale = HEAD_K ** -0.5
    gn = gn_ref[...]
    chunks = [slice(sub * L, (sub + 1) * L) for sub in range(chunks_per_step)]
    pairs = [(rows, h) for rows in chunks for h in range(N_HEADS)]

    def q_of(rows, h):
        return qkv_ref[rows, h * HEAD_K:(h + 1) * HEAD_K] * scale

    def k_of(rows, h):
        return qkv_ref[rows, D_K + h * HEAD_K:D_K + (h + 1) * HEAD_K]

    mall2 = mall_ref[...]
    e_all = {}
    for rows in chunks:
        la = la_ref[rows, :]
        la_hi = la.astype(BF16)
        la_lo = (la - la_hi.astype(F32)).astype(BF16)
        e_all[rows.start] = _dot(mall2, jnp.concatenate([la_hi, la_lo], axis=0))

    def scaled(rows, h, lev):
        if lev == 0:
            return q_of(rows, h).astype(BF16), k_of(rows, h).astype(BF16)
        e = jnp.exp(e_all[rows.start][lev * L:(lev + 1) * L, h * HEAD_K:(h + 1) * HEAD_K])
        return (q_of(rows, h) * e).astype(BF16), (k_of(rows, h) * e).astype(BF16)

    assert len(pairs) % 2 == 0
    zero = jnp.zeros((L, HEAD_K), BF16)
    amat = {}
    for lev in range(nlev):
        for pa, pb in zip(pairs[0::2], pairs[1::2]):
            qa, ka = scaled(*pa, lev)
            qb, kb = scaled(*pb, lev)
            lhs = jnp.concatenate([qa, qb], axis=1)
            rhs_t = jnp.concatenate([jnp.concatenate([ka, zero], axis=1),
                                     jnp.concatenate([zero, kb], axis=1)], axis=0)
            p = _dot_nt(lhs, rhs_t)
            for key, blk in (((pa[0].start, pa[1]), p[:, 0:L]), ((pb[0].start, pb[1]), p[:, L:2 * L])):
                term = wm_ref[lev] * blk
                amat[key] = term if lev == 0 else amat[key] + term

    for rows, h in pairs:
        ks = slice(h * HEAD_K, (h + 1) * HEAD_K)
        vs = slice(h * HEAD_V, (h + 1) * HEAD_V)
        q, k, a = q_of(rows, h), k_of(rows, h), amat[rows.start, h]
        vb = qkv_ref[rows, 2 * D_K + h * HEAD_V:2 * D_K + (h + 1) * HEAD_V].astype(BF16)

        eb = e_all[rows.start][nlev * L:(nlev + 1) * L, ks]
        ek = e_all[rows.start][(nlev + 1) * L:(nlev + 2) * L, ks]
        qh = (q * jnp.exp(eb)).astype(BF16)
        kh = k * jnp.exp(ek)
        s = s_ref[h]
        o = _dot(jnp.concatenate([a.astype(BF16), qh], axis=1),
                 jnp.concatenate([vb, s.astype(BF16)], axis=0))

        blast = jnp.broadcast_to(eb[L - 1:L, :], (HEAD_K, HEAD_K))
        acol = jnp.exp(blast.T)
        acol = jnp.concatenate([acol, acol], axis=1)
        s_ref[h] = acol * s + _dot(kh.T.astype(BF16), vb)

        y = _rms_rows(o, gn) * sg_ref[rows, vs]
        o_ref[rows, vs] = y.astype(o_ref.dtype)

    @pl.when(c == nsteps - 1)
    def _():
        snew_ref[0, 0] = s_ref[...]


def _gla_sample_kernel(la_ref, qkv_ref, sg_ref, st_ref, gn_ref, o_ref, snew_ref, at_ref, *, bblk):
    s_idx = pl.program_id(0)

    @pl.when(s_idx == 0)
    def _():
        for h in range(N_HEADS):
            at_ref[h] = jnp.exp(la_ref[:, h * HEAD_K:(h + 1) * HEAD_K]).T

    gn = gn_ref[...]
    scale = HEAD_K ** -0.5
    lane = lax.broadcasted_iota(jnp.int32, (HEAD_K, LANES), 1)
    row = lax.broadcasted_iota(jnp.int32, (bblk, HEAD_V), 0)
    for h in range(N_HEADS):
        vs = slice(h * HEAD_V, (h + 1) * HEAD_V)
        q = (qkv_ref[:, h * HEAD_K:(h + 1) * HEAD_K] * scale).astype(BF16)
        kt = qkv_ref[:, D_K + h * HEAD_K:D_K + (h + 1) * HEAD_K].T.astype(BF16)
        v = qkv_ref[:, 2 * D_K + h * HEAD_V:2 * D_K + (h + 1) * HEAD_V].astype(BF16)
        vdiag = jnp.concatenate([jnp.where(row == j, v, jnp.zeros_like(v)) for j in range(bblk)], axis=1)
        kv = _dot(kt, vdiag)
        outs = []
        for j in range(bblk):
            b = s_idx * bblk + j
            acol = jnp.sum(jnp.where(lane == b, at_ref[h], 0.0), axis=1, keepdims=True)
            s_new = acol * st_ref[j, h] + kv[:, j * HEAD_V:(j + 1) * HEAD_V]
            snew_ref[0, j, h] = s_new
            outs.append(_dot(q, s_new.astype(BF16))[j:j + 1, :])
        for j in range(bblk):
            o_ref[j:j + 1, vs] = _rms_rows(outs[j], gn) * sg_ref[j:j + 1, vs]


def _ln_silu(y, g, b):
    mu = jnp.mean(y, axis=-1, keepdims=True)
    yc = y - mu
    var = jnp.mean(yc * yc, axis=-1, keepdims=True)
    z = yc * lax.rsqrt(var + LN_EPS) * g + b
    return z * jax.nn.sigmoid(z)


def _conv_sample_kernel(sc_ref, glu_ref, cw_ref, cb_ref, lg_ref, lb_ref, c_ref, cnew_ref):
    nh = CONV_W - 1
    g = glu_ref[...]
    y = g * cw_ref[nh:nh + 1, :] + cb_ref[...]
    for w in range(nh):
        y = y + sc_ref[w] * cw_ref[w:w + 1, :]
    c_ref[...] = _ln_silu(y, lg_ref[...], lb_ref[...])
    for w in range(nh - 1):
        cnew_ref[w] = sc_ref[w + 1]
    cnew_ref[nh - 1] = g


def _fixed(shape, idx=None):
    idx = (0,) * len(shape) if idx is None else idx
    return pl.BlockSpec(shape, lambda *_: idx)


def kernel(x_prompt, x_sample, state_gla, state_conv, norm_mix, w_in, w_gate_up, b_gate, gla_norm,
           conv_w, conv_b, conv_ln_g, conv_ln_b, w_out, norm_ffn, w_ffn_in, w_ffn_out, norm_final):
    B, T, D = x_prompt.shape
    BS = x_sample.shape[0]
    assert state_gla.shape[0] == 1 and x_sample.shape[1] == 1
    d_conv = state_conv.shape[-1]
    d_ff = w_ffn_out.shape[1]
    in_cols = w_in.shape[-1]
    assert d_conv == D_A and in_cols == 2 * D_K + D_A + GATE_RANK + D_A + 2 * d_conv
    MP = B * T
    n_tiles = 8
    tmp, tms = MP // n_tiles, BS // n_tiles
    assert tmp * n_tiles == MP and tms * n_tiles == BS and tms % 8 == 0 and tmp % 256 == 0
    tm = tmp + tms

    xp = x_prompt.reshape(MP, D)
    xs = x_sample.reshape(BS, D)

    def pair_shapes(cols):
        return (jax.ShapeDtypeStruct((MP, cols), F32), jax.ShapeDtypeStruct((BS, cols), F32))

    wt = jnp.swapaxes(w_in, 1, 2)[0]
    nw = (in_cols - GATE_RANK) // MXU_N
    assert nw * MXU_N == in_cols - GATE_RANK and MXU_N % GATE_RANK == 0
    nj = Z_COLS // ZB
    row_i = lambda s: jnp.maximum(s - nw, 0) // nj
    col_j = lambda s: lax.rem(jnp.maximum(s - nw, 0), nj)
    wblk = lambda s: jnp.minimum(s, nw - 1)
    z_p, z_s = pl.pallas_call(
        functools.partial(_in_proj_kernel, nw=nw, nj=nj, tmp=tmp, tms=tms),
        grid=(nw + n_tiles * nj,),
        in_specs=[pl.BlockSpec((MXU_N, D), lambda s: (wblk(s), 0)),
                  pl.BlockSpec((GATE_RANK, D), lambda s: ((wblk(s) + 1) * (MXU_N // GATE_RANK), 0)),
                  pl.BlockSpec((tmp, D), lambda s: (row_i(s), 0)),
                  pl.BlockSpec((tms, D), lambda s: (row_i(s), 0)),
                  _fixed((1, D)), _fixed((GATE_RANK, D_K)), _fixed((1, D_K))],
        out_specs=(pl.BlockSpec((tmp, ZB), lambda s: (row_i(s), col_j(s))),
                   pl.BlockSpec((tms, ZB), lambda s: (row_i(s), col_j(s)))),
        out_shape=pair_shapes(Z_COLS),
        scratch_shapes=[pltpu.VMEM((nw, D, MXU_N), BF16), pltpu.VMEM((D, LANES), BF16),
                        pltpu.VMEM((tm, D), BF16)],
        compiler_params=_cparams(("arbitrary",)),
        name="in_proj",
    )(wt, wt, xp, xs, norm_mix[0].reshape(1, D), w_gate_up[0], b_gate[0].reshape(1, D_K))

    chunks_per_step = 4
    L = GLA_CHUNK * chunks_per_step
    nchunks = T // L
    assert nchunks * L == T
    nlev, mall, wm = _gla_constants(GLA_CHUNK)
    gn = gla_norm[0].reshape(1, HEAD_V)
    zrow = lambda b, c: b * nchunks + c
    oa_p, gla_new_p = pl.pallas_call(
        functools.partial(_gla_prompt_kernel, nlev=nlev, nsteps=nchunks, chunks_per_step=chunks_per_step),
        grid=(B, nchunks),
        in_specs=[pl.BlockSpec((L, Z_SG), lambda b, c: (zrow(b, c), 0)),
                  pl.BlockSpec((L, D_K), lambda b, c: (zrow(b, c), Z_LA // D_K)),
                  pl.BlockSpec((L, D_A), lambda b, c: (zrow(b, c), Z_SG // D_A)),
                  _fixed(mall.shape), _fixed(wm.shape),
                  _fixed((1, HEAD_V))],
        out_specs=(pl.BlockSpec((L, D_A), lambda b, c: (zrow(b, c), 0)),
                   pl.BlockSpec((1, 1, N_HEADS, HEAD_K, HEAD_V), lambda b, c: (0, b, 0, 0, 0))),
        out_shape=(jax.ShapeDtypeStruct((MP, D_A), BF16),
                   jax.ShapeDtypeStruct((1, B, N_HEADS, HEAD_K, HEAD_V), F32)),
        scratch_shapes=[pltpu.VMEM((N_HEADS, HEAD_K, HEAD_V), F32)],
        compiler_params=_cparams(("arbitrary", "arbitrary")),
        name="gla_prompt",
    )(z_p, z_p, z_p, jnp.asarray(mall, BF16), jnp.asarray(wm), gn)

    bblk = 8
    assert BS == LANES and BS % bblk == 0
    oa_s, gla_new_s = pl.pallas_call(
        functools.partial(_gla_sample_kernel, bblk=bblk),
        grid=(BS // bblk,),
        in_specs=[_fixed((BS, D_K), (0, Z_LA // D_K)),
                  pl.BlockSpec((bblk, Z_SG), lambda s: (s, 0)),
                  pl.BlockSpec((bblk, D_A), lambda s: (s, Z_SG // D_A)),
                  pl.BlockSpec((bblk, N_HEADS, HEAD_K, HEAD_V), lambda s: (s, 0, 0, 0)),
                  _fixed((1, HEAD_V))],
        out_specs=(pl.BlockSpec((bblk, D_A), lambda s: (s, 0)),
                   pl.BlockSpec((1, bblk, N_HEADS, HEAD_K, HEAD_V), lambda s: (0, s, 0, 0, 0))),
        out_shape=(jax.ShapeDtypeStruct((BS, D_A), F32),
                   jax.ShapeDtypeStruct((1, BS, N_HEADS, HEAD_K, HEAD_V), F32)),
        scratch_shapes=[pltpu.VMEM((N_HEADS, HEAD_K, BS), F32)],
        compiler_params=_cparams(("arbitrary",)),
        name="gla_sample",
    )(z_s, z_s, z_s, state_gla[0], gn)

    nslab = d_conv // LANES
    vec = lambda a: a.reshape(1, d_conv)

    cb_blk = 32
    sc_t = jnp.transpose(state_conv[0], (1, 0, 2))
    c_s, conv_new_s_t = pl.pallas_call(
        _conv_sample_kernel,
        grid=(BS // cb_blk,),
        in_specs=[pl.BlockSpec((CONV_W - 1, cb_blk, d_conv), lambda s: (0, s, 0)),
                  pl.BlockSpec((cb_blk, d_conv), lambda s: (s, Z_GLU // d_conv)),
                  _fixed((CONV_W, d_conv)), _fixed((1, d_conv)), _fixed((1, d_conv)), _fixed((1, d_conv))],
        out_specs=(pl.BlockSpec((cb_blk, d_conv), lambda s: (s, 0)),
                   pl.BlockSpec((CONV_W - 1, cb_blk, d_conv), lambda s: (0, s, 0))),
        out_shape=(jax.ShapeDtypeStruct((BS, d_conv), F32),
                   jax.ShapeDtypeStruct((CONV_W - 1, BS, d_conv), F32)),
        compiler_params=_cparams(("arbitrary",)),
        name="conv_sample",
    )(sc_t, z_s, conv_w[0], vec(conv_b[0]), vec(conv_ln_g[0]), vec(conv_ln_b[0]))
    conv_new_s = jnp.transpose(conv_new_s_t, (1, 0, 2))[None]

    tt = 512
    hist = 32
    nsteps = T // tt
    c_p, conv_new_p = pl.pallas_call(
        functools.partial(_conv_prompt_kernel, tt=tt, nsteps=nsteps, hist=hist),
        grid=(B, nsteps),
        in_specs=[pl.BlockSpec((tt, d_conv), lambda b, t: (b * nsteps + t, Z_GLU // d_conv)),
                  _fixed((CONV_W, nslab, LANES)), _fixed((nslab, LANES)),
                  _fixed((1, d_conv)), _fixed((1, d_conv))],
        out_specs=(pl.BlockSpec((tt, d_conv), lambda b, t: (b * nsteps + t, 0)),
                   pl.BlockSpec((1, 1, CONV_W - 1, d_conv), lambda b, t: (0, b, 0, 0))),
        out_shape=(jax.ShapeDtypeStruct((MP, d_conv), BF16),
                   jax.ShapeDtypeStruct((1, B, CONV_W - 1, d_conv), F32)),
        scratch_shapes=[pltpu.VMEM((nslab, hist + tt, LANES), F32), pltpu.VMEM((nslab, tt, LANES), F32)],
        compiler_params=_cparams(("arbitrary", "arbitrary")),
        name="conv_prompt",
    )(z_p, conv_w[0].reshape(CONV_W, nslab, LANES), conv_b[0].reshape(nslab, LANES),
      vec(conv_ln_g[0]), vec(conv_ln_b[0]))

    nwo = D // MXU_N
    njo = D // ZB
    row_o = lambda s: jnp.maximum(s - nwo, 0) // njo
    col_o = lambda s: lax.rem(jnp.maximum(s - nwo, 0), njo)
    x1_p, x1_s = pl.pallas_call(
        functools.partial(_outproj_kernel, nw=nwo, nj=njo, tmp=tmp, tms=tms),
        grid=(nwo + n_tiles * njo,),
        in_specs=[pl.BlockSpec((D_A + d_conv, MXU_N), lambda s: (0, jnp.minimum(s, nwo - 1))),
                  pl.BlockSpec((tmp, ZB), lambda s: (row_o(s), col_o(s))),
                  pl.BlockSpec((tms, ZB), lambda s: (row_o(s), col_o(s))),
                  pl.BlockSpec((tmp, D_A), lambda s: (row_o(s), 0)),
                  pl.BlockSpec((tms, D_A), lambda s: (row_o(s), 0)),
                  pl.BlockSpec((tmp, d_conv), lambda s: (row_o(s), 0)),
                  pl.BlockSpec((tms, d_conv), lambda s: (row_o(s), 0))],
        out_specs=(pl.BlockSpec((tmp, ZB), lambda s: (row_o(s), col_o(s))),
                   pl.BlockSpec((tms, ZB), lambda s: (row_o(s), col_o(s)))),
        out_shape=pair_shapes(D),
        scratch_shapes=[pltpu.VMEM((nwo, D_A + d_conv, MXU_N), BF16),
                        pltpu.VMEM((tm, D_A), BF16), pltpu.VMEM((tm, d_conv), BF16)],
        compiler_params=_cparams(("arbitrary",)),
        name="out_proj",
    )(w_out[0], xp, xs, oa_p, oa_s, c_p, c_s)

    tf = 256
    nf = d_ff // tf
    assert nf * tf == d_ff
    y_p, y_s = pl.pallas_call(
        functools.partial(_ffn_kernel, tmp=tmp, tms=tms, nf=nf),
        grid=(n_tiles, nf),
        in_specs=[pl.BlockSpec((tmp, D), lambda i, j: (i, 0)), pl.BlockSpec((tms, D), lambda i, j: (i, 0)),
                  _fixed((1, D)), _fixed((1, D)),
                  pl.BlockSpec((D, tf), lambda i, j: (0, j)),
                  pl.BlockSpec((D, tf), lambda i, j: (0, j + nf)),
                  pl.BlockSpec((tf, D), lambda i, j: (j, 0))],
        out_specs=(pl.BlockSpec((tmp, D), lambda i, j: (i, 0)), pl.BlockSpec((tms, D), lambda i, j: (i, 0))),
        out_shape=pair_shapes(D),
        scratch_shapes=[pltpu.VMEM((tm, D), BF16)],
        compiler_params=_cparams(("arbitrary", "arbitrary")),
        name="ffn",
    )(x1_p, x1_s, norm_ffn[0].reshape(1, D), norm_final.reshape(1, D), w_ffn_in[0], w_ffn_in[0], w_ffn_out[0])

    return (y_p.reshape(B, T, D), y_s.reshape(BS, 1, D), gla_new_p, conv_new_p, gla_new_s, conv_new_s)
```

```python
import functools

import numpy as np
import jax
import jax.numpy as jnp
from jax import lax
from jax.experimental import pallas as pl
from jax.experimental.pallas import tpu as pltpu

F32 = jnp.float32
BF16 = jnp.bfloat16

RMS_EPS = 1e-6
LN_EPS = 1e-5
GATE_NORM = 16.0
N_HEADS = 4
HEAD_K = 128
HEAD_V = 256
D_K = N_HEADS * HEAD_K
D_A = N_HEADS * HEAD_V
GATE_RANK = 16
CONV_W = 31

LANES = 128
MXU_N = 256
GLA_CHUNK = 128
VMEM_LIMIT = 58 * 1024 * 1024

ZB = 512
Z_QKV, Z_SG, Z_GLU, Z_LA = 0, 2 * D_K + D_A, 2 * D_K + 2 * D_A, 2 * D_K + 3 * D_A
Z_COLS = Z_LA + D_K


def _cparams(sem):
    return pltpu.CompilerParams(dimension_semantics=sem, vmem_limit_bytes=VMEM_LIMIT)


def _dot(a, b):
    return jnp.dot(a, b, preferred_element_type=F32)


def _dot_nt(a, b):
    return lax.dot_general(a, b, (((1,), (1,)), ((), ())), preferred_element_type=F32)


def _rms_rows(x, gain):
    ms = jnp.mean(x * x, axis=-1, keepdims=True)
    return x * lax.rsqrt(ms + RMS_EPS) * gain


def _log_sigmoid(x):
    return jnp.minimum(x, 0.0) - jnp.log(1.0 + jnp.exp(-jnp.abs(x)))


def _norm_to_scratch(h_ref, xp_ref, xs_ref, gain, tmp, tms, rows_per_pass=256):
    for r0 in range(0, tmp, rows_per_pass):
        h_ref[r0:r0 + rows_per_pass, :] = _rms_rows(xp_ref[r0:r0 + rows_per_pass, :], gain).astype(BF16)
    h_ref[tmp:tmp + tms, :] = _rms_rows(xs_ref[...], gain).astype(BF16)


def _in_proj_kernel(wt_ref, wnext_ref, xp_ref, xs_ref, g_ref, wup_ref, bg_ref, zp_ref, zs_ref,
                    wb_ref, wlr_ref, h_ref, *, nw, nj, tmp, tms):
    s = pl.program_id(0)
    rb = MXU_N
    n_direct = (2 * D_K + D_A) // rb

    @pl.when(s < n_direct)
    def _():
        wb_ref[s] = wt_ref[...].T.astype(BF16)

    @pl.when((s >= n_direct) & (s < nw))
    def _():
        rows = jnp.concatenate([wt_ref[GATE_RANK:rb, :], wnext_ref[...]], axis=0)
        wb_ref[s] = rows.T.astype(BF16)

    @pl.when(s == n_direct)
    def _():
        wlr_ref[...] = wt_ref[0:LANES, :].T.astype(BF16)

    @pl.when(s >= nw)
    def _project():
        j = lax.rem(s - nw, nj)

        @pl.when(j == 0)
        def _():
            _norm_to_scratch(h_ref, xp_ref, xs_ref, g_ref[...], tmp, tms)

        def pair(t0):
            h = h_ref[...]
            return jnp.concatenate([_dot(h, wb_ref[t0]), _dot(h, wb_ref[t0 + 1])], axis=1)

        def put(z):
            zp_ref[...] = z[:tmp]
            zs_ref[...] = z[tmp:]

        nqkv, nsg, nglu = Z_SG // ZB, (Z_GLU - Z_SG) // ZB, (Z_LA - Z_GLU) // ZB
        bpz = ZB // rb

        @pl.when(j < nqkv)
        def _():
            put(pair(bpz * j))

        @pl.when((j >= nqkv) & (j < nqkv + nsg))
        def _():
            z = pair(bpz * j)
            put(z * jax.nn.sigmoid(z))

        @pl.when((j >= nqkv + nsg) & (j < nqkv + nsg + nglu))
        def _():
            ta = bpz * j
            ua = pair(ta)
            ug = pair(ta + bpz * nglu)
            put(ua * jax.nn.sigmoid(ug))

        @pl.when(j == nqkv + nsg + nglu)
        def _():
            glr = _dot(h_ref[...], wlr_ref[...])[:, 0:GATE_RANK]
            logit = _dot(glr, wup_ref[...]) + bg_ref[...]
            put(_log_sigmoid(logit) * (1.0 / GATE_NORM))


def _conv_prompt_kernel(glu_ref, cw_ref, cb_ref, lg_ref, lb_ref, c_ref, cnew_ref, slab_ref, y_ref,
                        *, tt, nsteps, hist):
    t_idx = pl.program_id(1)
    nslab = slab_ref.shape[0]
    pad = hist - (CONV_W - 1)

    @pl.when(t_idx == 0)
    def _():
        slab_ref[:, 0:hist, :] = jnp.zeros((nslab, hist, LANES), F32)

    @pl.when(t_idx > 0)
    def _():
        slab_ref[:, 0:hist, :] = slab_ref[:, tt:tt + hist, :]

    for sl in range(nslab):
        slab_ref[sl, hist:hist + tt, :] = glu_ref[:, sl * LANES:(sl + 1) * LANES]

    def slab_conv(sl, carry):
        acc = None
        for res in range(8):
            part = None
            for w in range(res, CONV_W, 8):
                term = slab_ref[sl, pad + w:pad + w + tt, :] * cw_ref[w, pl.ds(sl, 1), :]
                part = term if part is None else part + term
            acc = part if acc is None else acc + part
        y_ref[sl] = acc + cb_ref[pl.ds(sl, 1), :]
        return carry

    lax.fori_loop(0, nslab, slab_conv, 0)

    y = y_ref[...]
    inv_c = 1.0 / (nslab * LANES)
    mu = jnp.sum(jnp.sum(y, axis=0), axis=-1, keepdims=True) * inv_c
    yc = y - mu[None]
    var = jnp.sum(jnp.sum(yc * yc, axis=0), axis=-1, keepdims=True) * inv_c
    rstd = lax.rsqrt(var + LN_EPS)
    for sl in range(nslab):
        cs = slice(sl * LANES, (sl + 1) * LANES)
        z = yc[sl] * rstd * lg_ref[:, cs] + lb_ref[:, cs]
        c_ref[:, cs] = (z * jax.nn.sigmoid(z)).astype(c_ref.dtype)

    @pl.when(t_idx == nsteps - 1)
    def _():
        for sl in range(nslab):
            cnew_ref[0, 0, :, sl * LANES:(sl + 1) * LANES] = slab_ref[sl, hist + tt - (CONV_W - 1):hist + tt, :]


def _outproj_kernel(w_ref, xp_ref, xs_ref, oap_ref, oas_ref, cp_ref, cs_ref, op_ref, os_ref,
                    wb_ref, a_ref, c_ref, *, nw, nj, tmp, tms):
    s = pl.program_id(0)
    rb = MXU_N
    wpl = w_ref.shape[1] // rb
    bpc = xp_ref.shape[1] // rb

    @pl.when(s < nw)
    def _load_weight():
        for q in range(wpl):
            wb_ref[s * wpl + q] = w_ref[:, q * rb:(q + 1) * rb].astype(BF16)

    @pl.when(s >= nw)
    def _project():
        j = lax.rem(s - nw, nj)

        @pl.when(j == 0)
        def _():
            a_ref[0:tmp, :] = oap_ref[...]
            a_ref[tmp:tmp + tms, :] = oas_ref[...].astype(BF16)
            c_ref[0:tmp, :] = cp_ref[...]
            c_ref[tmp:tmp + tms, :] = cs_ref[...].astype(BF16)

        for half in range(bpc):
            t = bpc * j + half
            cols = slice(half * rb, (half + 1) * rb)
            z = _dot(a_ref[...], wb_ref[t, 0:D_A, :]) + _dot(c_ref[...], wb_ref[t, D_A:, :])
            op_ref[:, cols] = xp_ref[:, cols] + z[:tmp]
            os_ref[:, cols] = xs_ref[:, cols] + z[tmp:]


def _ffn_kernel(xp_ref, xs_ref, g_ref, gf_ref, wg_ref, wu_ref, wo_ref, op_ref, os_ref, h_ref,
                *, tmp, tms, nf):
    j = pl.program_id(1)

    @pl.when(j == 0)
    def _():
        _norm_to_scratch(h_ref, xp_ref, xs_ref, g_ref[...], tmp, tms)
        op_ref[...] = xp_ref[...]
        os_ref[...] = xs_ref[...]

    h = h_ref[...]
    fg = _dot(h, wg_ref[...].astype(BF16))
    fu = _dot(h, wu_ref[...].astype(BF16))
    act = (fg * jax.nn.sigmoid(fg) * fu).astype(BF16)
    wo = wo_ref[...].astype(BF16)
    half = tmp // 2
    op_ref[0:half, :] += _dot(act[0:half], wo)
    z = _dot(act[half:], wo)
    op_ref[half:tmp, :] += z[:tmp - half]
    os_ref[...] += z[tmp - half:]

    @pl.when(j == nf - 1)
    def _():
        gf = gf_ref[...]
        for r0 in range(0, tmp, 256):
            op_ref[r0:r0 + 256, :] = _rms_rows(op_ref[r0:r0 + 256, :], gf)
        os_ref[...] = _rms_rows(os_ref[...], gf)


@functools.lru_cache(maxsize=None)
def _gla_constants(L):
    nlev = int(np.log2(L)) + 1
    m = np.zeros((nlev + 2, L, L), np.float32)
    wm = np.zeros((nlev, L, L), np.float32)
    idx = np.arange(L)
    wm[0] = np.eye(L, dtype=np.float32)
    for lev in range(1, nlev):
        half = L >> lev
        pos = idx % (2 * half)
        upper = pos >= half
        r = idx - pos + half - 1
        t = idx[None, :]
        up_rows = (t > r[:, None]) & (t <= idx[:, None]) & upper[:, None]
        lo_rows = (t > idx[:, None]) & (t <= r[:, None]) & (~upper)[:, None]
        m[lev] = (up_rows | lo_rows).astype(np.float32)
        same = (idx[:, None] // (2 * half)) == (idx[None, :] // (2 * half))
        wm[lev] = (same & upper[:, None] & (~upper)[None, :]).astype(np.float32)
    m[nlev] = (idx[None, :] <= idx[:, None]).astype(np.float32)
    m[nlev + 1] = (idx[None, :] > idx[:, None]).astype(np.float32)
    m = m.reshape((nlev + 2) * L, L)
    return nlev, np.concatenate([m, m], axis=1), wm


def _gla_prompt_kernel(qkv_ref, la_ref, sg_ref, mall_ref, wm_ref, gn_ref,
                       o_ref, snew_ref, s_ref, *, nlev, nsteps, chunks_per_step):
    c = pl.program_id(1)

    @pl.when(c == 0)
    def _():
        s_ref[...] = jnp.zeros_like(s_ref)

    L = GLA_CHUNK
    scale = HEAD_K ** -0.5
    gn = gn_ref[...]
    chunks = [slice(sub * L, (sub + 1) * L) for sub in range(chunks_per_step)]
    pairs = [(rows, h) for rows in chunks for h in range(N_HEADS)]

    def q_of(rows, h):
        return qkv_ref[rows, h * HEAD_K:(h + 1) * HEAD_K] * scale

    def k_of(rows, h):
        return qkv_ref[rows, D_K + h * HEAD_K:D_K + (h + 1) * HEAD_K]

    mall2 = mall_ref[...]
    e_all = {}
    for rows in chunks:
        la = la_ref[rows, :]
        la_hi = la.astype(BF16)
        la_lo = (la - la_hi.astype(F32)).astype(BF16)
        e_all[rows.start] = _dot(mall2, jnp.concatenate([la_hi, la_lo], axis=0))

    def scaled(rows, h, lev):
        if lev == 0:
            return q_of(rows, h).astype(BF16), k_of(rows, h).astype(BF16)
        e = jnp.exp(e_all[rows.start][lev * L:(lev + 1) * L, h * HEAD_K:(h + 1) * HEAD_K])
        return (q_of(rows, h) * e).astype(BF16), (k_of(rows, h) * e).astype(BF16)

    assert len(pairs) % 2 == 0
    zero = jnp.zeros((L, HEAD_K), BF16)
    amat = {}
    for lev in range(nlev):
        for pa, pb in zip(pairs[0::2], pairs[1::2]):
            qa, ka = scaled(*pa, lev)
            qb, kb = scaled(*pb, lev)
            lhs = jnp.concatenate([qa, qb], axis=1)
            rhs_t = jnp.concatenate([jnp.concatenate([ka, zero], axis=1),
                                     jnp.concatenate([zero, kb], axis=1)], axis=0)
            p = _dot_nt(lhs, rhs_t)
            for key, blk in (((pa[0].start, pa[1]), p[:, 0:L]), ((pb[0].start, pb[1]), p[:, L:2 * L])):
                term = wm_ref[lev] * blk
                amat[key] = term if lev == 0 else amat[key] + term

    for rows, h in pairs:
        ks = slice(h * HEAD_K, (h + 1) * HEAD_K)
        vs = slice(h * HEAD_V, (h + 1) * HEAD_V)
        q, k, a = q_of(rows, h), k_of(rows, h), amat[rows.start, h]
        vb = qkv_ref[rows, 2 * D_K + h * HEAD_V:2 * D_K + (h + 1) * HEAD_V].astype(BF16)

        eb = e_all[rows.start][nlev * L:(nlev + 1) * L, ks]
        ek = e_all[rows.start][(nlev + 1) * L:(nlev + 2) * L, ks]
        qh = (q * jnp.exp(eb)).astype(BF16)
        kh = k * jnp.exp(ek)
        s = s_ref[h]
        o = _dot(jnp.concatenate([a.astype(BF16), qh], axis=1),
                 jnp.concatenate([vb, s.astype(BF16)], axis=0))

        blast = jnp.broadcast_to(eb[L - 1:L, :], (HEAD_K, HEAD_K))
        acol = jnp.exp(blast.T)
        acol = jnp.concatenate([acol, acol], axis=1)
        s_ref[h] = acol * s + _dot(kh.T.astype(BF16), vb)

        y = _rms_rows(o, gn) * sg_ref[rows, vs]
        o_ref[rows, vs] = y.astype(o_ref.dtype)

    @pl.when(c == nsteps - 1)
    def _():
        snew_ref[0, 0] = s_ref[...]


def _gla_sample_kernel(la_ref, qkv_ref, sg_ref, st_ref, gn_ref, o_ref, snew_ref, at_ref, *, bblk):
    s_idx = pl.program_id(0)

    @pl.when(s_idx == 0)
    def _():
        for h in range(N_HEADS):
            at_ref[h] = jnp.exp(la_ref[:, h * HEAD_K:(h + 1) * HEAD_K]).T

    gn = gn_ref[...]
    scale = HEAD_K ** -0.5
    lane = lax.broadcasted_iota(jnp.int32, (HEAD_K, LANES), 1)
    row = lax.broadcasted_iota(jnp.int32, (bblk, HEAD_V), 0)
    for h in range(N_HEADS):
        vs = slice(h * HEAD_V, (h + 1) * HEAD_V)
        q = (qkv_ref[:, h * HEAD_K:(h + 1) * HEAD_K] * scale).astype(BF16)
        kt = qkv_ref[:, D_K + h * HEAD_K:D_K + (h + 1) * HEAD_K].T.astype(BF16)
        v = qkv_ref[:, 2 * D_K + h * HEAD_V:2 * D_K + (h + 1) * HEAD_V].astype(BF16)
        vdiag = jnp.concatenate([jnp.where(row == j, v, jnp.zeros_like(v)) for j in range(bblk)], axis=1)
        kv = _dot(kt, vdiag)
        outs = []
        for j in range(bblk):
            b = s_idx * bblk + j
            acol = jnp.sum(jnp.where(lane == b, at_ref[h], 0.0), axis=1, keepdims=True)
            s_new = acol * st_ref[j, h] + kv[:, j * HEAD_V:(j + 1) * HEAD_V]
            snew_ref[0, j, h] = s_new
            outs.append(_dot(q, s_new.astype(BF16))[j:j + 1, :])
        for j in range(bblk):
            o_ref[j:j + 1, vs] = _rms_rows(outs[j], gn) * sg_ref[j:j + 1, vs]


def _ln_silu(y, g, b):
    mu = jnp.mean(y, axis=-1, keepdims=True)
    yc = y - mu
    var = jnp.mean(yc * yc, axis=-1, keepdims=True)
    z = yc * lax.rsqrt(var + LN_EPS) * g + b
    return z * jax.nn.sigmoid(z)


def _conv_sample_kernel(sc_ref, glu_ref, cw_ref, cb_ref, lg_ref, lb_ref, c_ref, cnew_ref):
    nh = CONV_W - 1
    g = glu_ref[...]
    y = g * cw_ref[nh:nh + 1, :] + cb_ref[...]
    for w in range(nh):
        y = y + sc_ref[w] * cw_ref[w:w + 1, :]
    c_ref[...] = _ln_silu(y, lg_ref[...], lb_ref[...])
    for w in range(nh - 1):
        cnew_ref[w] = sc_ref[w + 1]
    cnew_ref[nh - 1] = g


def _fixed(shape, idx=None):
    idx = (0,) * len(shape) if idx is None else idx
    return pl.BlockSpec(shape, lambda *_: idx)


def kernel(x_prompt, x_sample, state_gla, state_conv, norm_mix, w_in, w_gate_up, b_gate, gla_norm,
           conv_w, conv_b, conv_ln_g, conv_ln_b, w_out, norm_ffn, w_ffn_in, w_ffn_out, norm_final):
    B, T, D = x_prompt.shape
    BS = x_sample.shape[0]
    assert state_gla.shape[0] == 1 and x_sample.shape[1] == 1
    d_conv = state_conv.shape[-1]
    d_ff = w_ffn_out.shape[1]
    in_cols = w_in.shape[-1]
    assert d_conv == D_A and in_cols == 2 * D_K + D_A + GATE_RANK + D_A + 2 * d_conv
    MP = B * T
    n_tiles = 8
    tmp, tms = MP // n_tiles, BS // n_tiles
    assert tmp * n_tiles == MP and tms * n_tiles == BS and tms % 8 == 0 and tmp % 256 == 0
    tm = tmp + tms

    xp = x_prompt.reshape(MP, D)
    xs = x_sample.reshape(BS, D)

    def pair_shapes(cols):
        return (jax.ShapeDtypeStruct((MP, cols), F32), jax.ShapeDtypeStruct((BS, cols), F32))

    wt = jnp.swapaxes(w_in, 1, 2)[0]
    nw = (in_cols - GATE_RANK) // MXU_N
    assert nw * MXU_N == in_cols - GATE_RANK and MXU_N % GATE_RANK == 0
    nj = Z_COLS // ZB
    row_i = lambda s: jnp.maximum(s - nw, 0) // nj
    col_j = lambda s: lax.rem(jnp.maximum(s - nw, 0), nj)
    wblk = lambda s: jnp.minimum(s, nw - 1)
    z_p, z_s = pl.pallas_call(
        functools.partial(_in_proj_kernel, nw=nw, nj=nj, tmp=tmp, tms=tms),
        grid=(nw + n_tiles * nj,),
        in_specs=[pl.BlockSpec((MXU_N, D), lambda s: (wblk(s), 0)),
                  pl.BlockSpec((GATE_RANK, D), lambda s: ((wblk(s) + 1) * (MXU_N // GATE_RANK), 0)),
                  pl.BlockSpec((tmp, D), lambda s: (row_i(s), 0)),
                  pl.BlockSpec((tms, D), lambda s: (row_i(s), 0)),
                  _fixed((1, D)), _fixed((GATE_RANK, D_K)), _fixed((1, D_K))],
        out_specs=(pl.BlockSpec((tmp, ZB), lambda s: (row_i(s), col_j(s))),
                   pl.BlockSpec((tms, ZB), lambda s: (row_i(s), col_j(s)))),
        out_shape=pair_shapes(Z_COLS),
        scratch_shapes=[pltpu.VMEM((nw, D, MXU_N), BF16), pltpu.VMEM((D, LANES), BF16),
                        pltpu.VMEM((tm, D), BF16)],
        compiler_params=_cparams(("arbitrary",)),
        name="in_proj",
    )(wt, wt, xp, xs, norm_mix[0].reshape(1, D), w_gate_up[0], b_gate[0].reshape(1, D_K))

    chunks_per_step = 4
    L = GLA_CHUNK * chunks_per_step
    nchunks = T // L
    assert nchunks * L == T
    nlev, mall, wm = _gla_constants(GLA_CHUNK)
    gn = gla_norm[0].reshape(1, HEAD_V)
    zrow = lambda b, c: b * nchunks + c
    oa_p, gla_new_p = pl.pallas_call(
        functools.partial(_gla_prompt_kernel, nlev=nlev, nsteps=nchunks, chunks_per_step=chunks_per_step),
        grid=(B, nchunks),
        in_specs=[pl.BlockSpec((L, Z_SG), lambda b, c: (zrow(b, c), 0)),
                  pl.BlockSpec((L, D_K), lambda b, c: (zrow(b, c), Z_LA // D_K)),
                  pl.BlockSpec((L, D_A), lambda b, c: (zrow(b, c), Z_SG // D_A)),
                  _fixed(mall.shape), _fixed(wm.shape),
                  _fixed((1, HEAD_V))],
        out_specs=(pl.BlockSpec((L, D_A), lambda b, c: (zrow(b, c), 0)),
                   pl.BlockSpec((1, 1, N_HEADS, HEAD_K, HEAD_V), lambda b, c: (0, b, 0, 0, 0))),
        out_shape=(jax.ShapeDtypeStruct((MP, D_A), BF16),
                   jax.ShapeDtypeStruct((1, B, N_HEADS, HEAD_K, HEAD_V), F32)),
        scratch_shapes=[pltpu.VMEM((N_HEADS, HEAD_K, HEAD_V), F32)],
        compiler_params=_cparams(("arbitrary", "arbitrary")),
        name="gla_prompt",
    )(z_p, z_p, z_p, jnp.asarray(mall, BF16), jnp.asarray(wm), gn)

    bblk = 8
    assert BS == LANES and BS % bblk == 0
    oa_s, gla_new_s = pl.pallas_call(
        functools.partial(_gla_sample_kernel, bblk=bblk),
        grid=(BS // bblk,),
        in_specs=[_fixed((BS, D_K), (0, Z_LA // D_K)),
                  pl.BlockSpec((bblk, Z_SG), lambda s: (s, 0)),
                  pl.BlockSpec((bblk, D_A), lambda s: (s, Z_SG // D_A)),
                  pl.BlockSpec((bblk, N_HEADS, HEAD_K, HEAD_V), lambda s: (s, 0, 0, 0)),
                  _fixed((1, HEAD_V))],
        out_specs=(pl.BlockSpec((bblk, D_A), lambda s: (s, 0)),
                   pl.BlockSpec((1, bblk, N_HEADS, HEAD_K, HEAD_V), lambda s: (0, s, 0, 0, 0))),
        out_shape=(jax.ShapeDtypeStruct((BS, D_A), F32),
                   jax.ShapeDtypeStruct((1, BS, N_HEADS, HEAD_K, HEAD_V), F32)),
        scratch_shapes=[pltpu.VMEM((N_HEADS, HEAD_K, BS), F32)],
        compiler_params=_cparams(("arbitrary",)),
        name="gla_sample",
    )(z_s, z_s, z_s, state_gla[0], gn)

    nslab = d_conv // LANES
    vec = lambda a: a.reshape(1, d_conv)

    cb_blk = 32
    sc_t = jnp.transpose(state_conv[0], (1, 0, 2))
    c_s, conv_new_s_t = pl.pallas_call(
        _conv_sample_kernel,
        grid=(BS // cb_blk,),
        in_specs=[pl.BlockSpec((CONV_W - 1, cb_blk, d_conv), lambda s: (0, s, 0)),
                  pl.BlockSpec((cb_blk, d_conv), lambda s: (s, Z_GLU // d_conv)),
                  _fixed((CONV_W, d_conv)), _fixed((1, d_conv)), _fixed((1, d_conv)), _fixed((1, d_conv))],
        out_specs=(pl.BlockSpec((cb_blk, d_conv), lambda s: (s, 0)),
                   pl.BlockSpec((CONV_W - 1, cb_blk, d_conv), lambda s: (0, s, 0))),
        out_shape=(jax.ShapeDtypeStruct((BS, d_conv), F32),
                   jax.ShapeDtypeStruct((CONV_W - 1, BS, d_conv), F32)),
        compiler_params=_cparams(("arbitrary",)),
        name="conv_sample",
    )(sc_t, z_s, conv_w[0], vec(conv_b[0]), vec(conv_ln_g[0]), vec(conv_ln_b[0]))
    conv_new_s = jnp.transpose(conv_new_s_t, (1, 0, 2))[None]

    tt = 512
    hist = 32
    nsteps = T // tt
    c_p, conv_new_p = pl.pallas_call(
        functools.partial(_conv_prompt_kernel, tt=tt, nsteps=nsteps, hist=hist),
        grid=(B, nsteps),
        in_specs=[pl.BlockSpec((tt, d_conv), lambda b, t: (b * nsteps + t, Z_GLU // d_conv)),
                  _fixed((CONV_W, nslab, LANES)), _fixed((nslab, LANES)),
                  _fixed((1, d_conv)), _fixed((1, d_conv))],
        out_specs=(pl.BlockSpec((tt, d_conv), lambda b, t: (b * nsteps + t, 0)),
                   pl.BlockSpec((1, 1, CONV_W - 1, d_conv), lambda b, t: (0, b, 0, 0))),
        out_shape=(jax.ShapeDtypeStruct((MP, d_conv), BF16),
                   jax.ShapeDtypeStruct((1, B, CONV_W - 1, d_conv), F32)),
        scratch_shapes=[pltpu.VMEM((nslab, hist + tt, LANES), F32), pltpu.VMEM((nslab, tt, LANES), F32)],
        compiler_params=_cparams(("arbitrary", "arbitrary")),
        name="conv_prompt",
    )(z_p, conv_w[0].reshape(CONV_W, nslab, LANES), conv_b[0].reshape(nslab, LANES),
      vec(conv_ln_g[0]), vec(conv_ln_b[0]))

    zbo = min(1024, D)
    wcols = min(2 * MXU_N, D)
    nwo = D // wcols
    njo = D // zbo
    assert nwo * wcols == D and njo * zbo == D
    row_o = lambda s: jnp.maximum(s - nwo, 0) // njo
    col_o = lambda s: lax.rem(jnp.maximum(s - nwo, 0), njo)
    x1_p, x1_s = pl.pallas_call(
        functools.partial(_outproj_kernel, nw=nwo, nj=njo, tmp=tmp, tms=tms),
        grid=(nwo + n_tiles * njo,),
        in_specs=[pl.BlockSpec((D_A + d_conv, wcols), lambda s: (0, jnp.minimum(s, nwo - 1))),
                  pl.BlockSpec((tmp, zbo), lambda s: (row_o(s), col_o(s))),
                  pl.BlockSpec((tms, zbo), lambda s: (row_o(s), col_o(s))),
                  pl.BlockSpec((tmp, D_A), lambda s: (row_o(s), 0)),
                  pl.BlockSpec((tms, D_A), lambda s: (row_o(s), 0)),
                  pl.BlockSpec((tmp, d_conv), lambda s: (row_o(s), 0)),
                  pl.BlockSpec((tms, d_conv), lambda s: (row_o(s), 0))],
        out_specs=(pl.BlockSpec((tmp, zbo), lambda s: (row_o(s), col_o(s))),
                   pl.BlockSpec((tms, zbo), lambda s: (row_o(s), col_o(s)))),
        out_shape=pair_shapes(D),
        scratch_shapes=[pltpu.VMEM((D // MXU_N, D_A + d_conv, MXU_N), BF16),
                        pltpu.VMEM((tm, D_A), BF16), pltpu.VMEM((tm, d_conv), BF16)],
        compiler_params=_cparams(("arbitrary",)),
        name="out_proj",
    )(w_out[0], xp, xs, oa_p, oa_s, c_p, c_s)

    tf = 256
    nf = d_ff // tf
    assert nf * tf == d_ff
    y_p, y_s = pl.pallas_call(
        functools.partial(_ffn_kernel, tmp=tmp, tms=tms, nf=nf),
        grid=(n_tiles, nf),
        in_specs=[pl.BlockSpec((tmp, D), lambda i, j: (i, 0)), pl.BlockSpec((tms, D), lambda i, j: (i, 0)),
                  _fixed((1, D)), _fixed((1, D)),
                  pl.BlockSpec((D, tf), lambda i, j: (0, j)),
                  pl.BlockSpec((D, tf), lambda i, j: (0, j + nf)),
                  pl.BlockSpec((tf, D), lambda i, j: (j, 0))],
        out_specs=(pl.BlockSpec((tmp, D), lambda i, j: (i, 0)), pl.BlockSpec((tms, D), lambda i, j: (i, 0))),
        out_shape=pair_shapes(D),
        scratch_shapes=[pltpu.VMEM((tm, D), BF16)],
        compiler_params=_cparams(("arbitrary", "arbitrary")),
        name="ffn",
    )(x1_p, x1_s, norm_ffn[0].reshape(1, D), norm_final.reshape(1, D), w_ffn_in[0], w_ffn_in[0], w_ffn_out[0])

    return (y_p.reshape(B, T, D), y_s.reshape(BS, 1, D), gla_new_p, conv_new_p, gla_new_s, conv_new_s)
```

```python
import functools

import numpy as np
import jax
import jax.numpy as jnp
from jax import lax
from jax.experimental import pallas as pl
from jax.experimental.pallas import tpu as pltpu

F32 = jnp.float32
BF16 = jnp.bfloat16

RMS_EPS = 1e-6
LN_EPS = 1e-5
GATE_NORM = 16.0
N_HEADS = 4
HEAD_K = 128
HEAD_V = 256
D_K = N_HEADS * HEAD_K
D_A = N_HEADS * HEAD_V
GATE_RANK = 16
CONV_W = 31

LANES = 128
MXU_N = 256
GLA_CHUNK = 128
VMEM_LIMIT = 58 * 1024 * 1024

ZB = 512
Z_QKV, Z_SG, Z_GLU, Z_LA = 0, 2 * D_K + D_A, 2 * D_K + 2 * D_A, 2 * D_K + 3 * D_A
Z_COLS = Z_LA + D_K


def _cparams(sem):
    return pltpu.CompilerParams(dimension_semantics=sem, vmem_limit_bytes=VMEM_LIMIT)


def _dot(a, b):
    return jnp.dot(a, b, preferred_element_type=F32)


def _dot_nt(a, b):
    return lax.dot_general(a, b, (((1,), (1,)), ((), ())), preferred_element_type=F32)


def _rms_rows(x, gain):
    ms = jnp.mean(x * x, axis=-1, keepdims=True)
    return x * lax.rsqrt(ms + RMS_EPS) * gain


def _log_sigmoid(x):
    return jnp.minimum(x, 0.0) - jnp.log(1.0 + jnp.exp(-jnp.abs(x)))


def _norm_to_scratch(h_ref, xp_ref, xs_ref, gain, tmp, tms, rows_per_pass=256):
    for r0 in range(0, tmp, rows_per_pass):
        h_ref[r0:r0 + rows_per_pass, :] = _rms_rows(xp_ref[r0:r0 + rows_per_pass, :], gain).astype(BF16)
    h_ref[tmp:tmp + tms, :] = _rms_rows(xs_ref[...], gain).astype(BF16)


def _in_proj_kernel(wta_ref, wtb_ref, wnext_ref, xp_ref, xs_ref, g_ref, wup_ref, bg_ref, zp_ref, zs_ref,
                    wb_ref, wlr_ref, h_ref, *, nw, nj, tmp, tms):
    s = pl.program_id(0)
    rb = MXU_N
    n_direct = (2 * D_K + D_A) // rb

    @pl.when(s < n_direct)
    def _():
        rows = jnp.concatenate([wta_ref[...], wtb_ref[...]], axis=0)
        wb_ref[s] = rows.T.astype(BF16)

    @pl.when((s >= n_direct) & (s < nw))
    def _():
        rows = jnp.concatenate([wta_ref[GATE_RANK:, :], wtb_ref[...], wnext_ref[...]], axis=0)
        wb_ref[s] = rows.T.astype(BF16)

    @pl.when(s == n_direct)
    def _():
        wlr_ref[...] = wta_ref[...].T.astype(BF16)

    @pl.when(s >= nw)
    def _project():
        j = lax.rem(s - nw, nj)

        @pl.when(j == 0)
        def _():
            _norm_to_scratch(h_ref, xp_ref, xs_ref, g_ref[...], tmp, tms)

        def pair(t0):
            h = h_ref[...]
            return jnp.concatenate([_dot(h, wb_ref[t0]), _dot(h, wb_ref[t0 + 1])], axis=1)

        def put(z):
            zp_ref[...] = z[:tmp]
            zs_ref[...] = z[tmp:]

        nqkv, nsg, nglu = Z_SG // ZB, (Z_GLU - Z_SG) // ZB, (Z_LA - Z_GLU) // ZB
        bpz = ZB // rb

        @pl.when(j < nqkv)
        def _():
            put(pair(bpz * j))

        @pl.when((j >= nqkv) & (j < nqkv + nsg))
        def _():
            z = pair(bpz * j)
            put(z * jax.nn.sigmoid(z))

        @pl.when((j >= nqkv + nsg) & (j < nqkv + nsg + nglu))
        def _():
            ta = bpz * j
            ua = pair(ta)
            ug = pair(ta + bpz * nglu)
            put(ua * jax.nn.sigmoid(ug))

        @pl.when(j == nqkv + nsg + nglu)
        def _():
            glr = _dot(h_ref[...], wlr_ref[...])[:, 0:GATE_RANK]
            logit = _dot(glr, wup_ref[...]) + bg_ref[...]
            put(_log_sigmoid(logit) * (1.0 / GATE_NORM))


def _conv_prompt_kernel(glu_ref, cw_ref, cb_ref, lg_ref, lb_ref, c_ref, cnew_ref, slab_ref, y_ref,
                        *, tt, nsteps, hist):
    t_idx = pl.program_id(1)
    nslab = slab_ref.shape[0]
    pad = hist - (CONV_W - 1)

    @pl.when(t_idx == 0)
    def _():
        slab_ref[:, 0:hist, :] = jnp.zeros((nslab, hist, LANES), F32)

    @pl.when(t_idx > 0)
    def _():
        slab_ref[:, 0:hist, :] = slab_ref[:, tt:tt + hist, :]

    for sl in range(nslab):
        slab_ref[sl, hist:hist + tt, :] = glu_ref[:, sl * LANES:(sl + 1) * LANES]

    def slab_conv(sl, carry):
        acc = None
        for res in range(8):
            part = None
            for w in range(res, CONV_W, 8):
                term = slab_ref[sl, pad + w:pad + w + tt, :] * cw_ref[w, pl.ds(sl, 1), :]
                part = term if part is None else part + term
            acc = part if acc is None else acc + part
        y_ref[sl] = acc + cb_ref[pl.ds(sl, 1), :]
        return carry

    lax.fori_loop(0, nslab, slab_conv, 0)

    inv_c = 1.0 / (nslab * LANES)
    for r0 in range(0, tt, 256):
        y = y_ref[:, r0:r0 + 256, :]
        mu = jnp.sum(jnp.sum(y, axis=0), axis=-1, keepdims=True) * inv_c
        yc = y - mu[None]
        var = jnp.sum(jnp.sum(yc * yc, axis=0), axis=-1, keepdims=True) * inv_c
        rstd = lax.rsqrt(var + LN_EPS)
        for sl in range(nslab):
            cs = slice(sl * LANES, (sl + 1) * LANES)
            z = yc[sl] * rstd * lg_ref[:, cs] + lb_ref[:, cs]
            c_ref[r0:r0 + 256, cs] = (z * jax.nn.sigmoid(z)).astype(c_ref.dtype)

    @pl.when(t_idx == nsteps - 1)
    def _():
        for sl in range(nslab):
            cnew_ref[0, 0, :, sl * LANES:(sl + 1) * LANES] = slab_ref[sl, hist + tt - (CONV_W - 1):hist + tt, :]


def _outproj_kernel(w_ref, xp_ref, xs_ref, oap_ref, oas_ref, cp_ref, cs_ref, op_ref, os_ref,
                    wb_ref, a_ref, c_ref, *, nw, nj, tmp, tms):
    s = pl.program_id(0)
    rb = MXU_N
    wpl = w_ref.shape[1] // rb
    bpc = xp_ref.shape[1] // rb

    @pl.when(s < nw)
    def _load_weight():
        for q in range(wpl):
            wb_ref[s * wpl + q] = w_ref[:, q * rb:(q + 1) * rb].astype(BF16)

    @pl.when(s >= nw)
    def _project():
        j = lax.rem(s - nw, nj)

        @pl.when(j == 0)
        def _():
            a_ref[0:tmp, :] = oap_ref[...]
            a_ref[tmp:tmp + tms, :] = oas_ref[...].astype(BF16)
            c_ref[0:tmp, :] = cp_ref[...]
            c_ref[tmp:tmp + tms, :] = cs_ref[...].astype(BF16)

        for half in range(bpc):
            t = bpc * j + half
            cols = slice(half * rb, (half + 1) * rb)
            z = _dot(a_ref[...], wb_ref[t, 0:D_A, :]) + _dot(c_ref[...], wb_ref[t, D_A:, :])
            op_ref[:, cols] = xp_ref[:, cols] + z[:tmp]
            os_ref[:, cols] = xs_ref[:, cols] + z[tmp:]


def _ffn_kernel(xp_ref, xs_ref, g_ref, gf_ref, wg_ref, wu_ref, wo_ref, op_ref, os_ref, h_ref,
                *, tmp, tms, nf):
    j = pl.program_id(1)

    @pl.when(j == 0)
    def _():
        _norm_to_scratch(h_ref, xp_ref, xs_ref, g_ref[...], tmp, tms)
        op_ref[...] = xp_ref[...]
        os_ref[...] = xs_ref[...]

    h = h_ref[...]
    fg = _dot(h, wg_ref[...].astype(BF16))
    fu = _dot(h, wu_ref[...].astype(BF16))
    act = (fg * jax.nn.sigmoid(fg) * fu).astype(BF16)
    wo = wo_ref[...].astype(BF16)
    half = tmp // 2
    op_ref[0:half, :] += _dot(act[0:half], wo)
    z = _dot(act[half:], wo)
    op_ref[half:tmp, :] += z[:tmp - half]
    os_ref[...] += z[tmp - half:]

    @pl.when(j == nf - 1)
    def _():
        gf = gf_ref[...]
        for r0 in range(0, tmp, 256):
            op_ref[r0:r0 + 256, :] = _rms_rows(op_ref[r0:r0 + 256, :], gf)
        os_ref[...] = _rms_rows(os_ref[...], gf)


@functools.lru_cache(maxsize=None)
def _gla_constants(L):
    nlev = int(np.log2(L)) + 1
    m = np.zeros((nlev + 2, L, L), np.float32)
    wm = np.zeros((nlev, L, L), np.float32)
    idx = np.arange(L)
    wm[0] = np.eye(L, dtype=np.float32)
    for lev in range(1, nlev):
        half = L >> lev
        pos = idx % (2 * half)
        upper = pos >= half
        r = idx - pos + half - 1
        t = idx[None, :]
        up_rows = (t > r[:, None]) & (t <= idx[:, None]) & upper[:, None]
        lo_rows = (t > idx[:, None]) & (t <= r[:, None]) & (~upper)[:, None]
        m[lev] = (up_rows | lo_rows).astype(np.float32)
        same = (idx[:, None] // (2 * half)) == (idx[None, :] // (2 * half))
        wm[lev] = (same & upper[:, None] & (~upper)[None, :]).astype(np.float32)
    m[nlev] = (idx[None, :] <= idx[:, None]).astype(np.float32)
    m[nlev + 1] = (idx[None, :] > idx[:, None]).astype(np.float32)
    m = m.reshape((nlev + 2) * L, L)
    return nlev, np.concatenate([m, m], axis=1), wm


def _gla_prompt_kernel(qkv_ref, la_ref, sg_ref, mall_ref, wm_ref, gn_ref,
                       o_ref, snew_ref, s_ref, *, nlev, nsteps, chunks_per_step):
    c = pl.program_id(1)

    @pl.when(c == 0)
    def _():
        s_ref[...] = jnp.zeros_like(s_ref)

    L = GLA_CHUNK
    scale = HEAD_K ** -0.5
    gn = gn_ref[...]
    chunks = [slice(sub * L, (sub + 1) * L) for sub in range(chunks_per_step)]
    pairs = [(rows, h) for rows in chunks for h in range(N_HEADS)]

    def q_of(rows, h):
        return qkv_ref[rows, h * HEAD_K:(h + 1) * HEAD_K] * scale

    def k_of(rows, h):
        return qkv_ref[rows, D_K + h * HEAD_K:D_K + (h + 1) * HEAD_K]

    mall2 = mall_ref[...]
    e_all = {}
    for rows in chunks:
        la = la_ref[rows, :]
        la_hi = la.astype(BF16)
        la_lo = (la - la_hi.astype(F32)).astype(BF16)
        e_all[rows.start] = _dot(mall2, jnp.concatenate([la_hi, la_lo], axis=0))

    def scaled(rows, h, lev):
        if lev == 0:
            return q_of(rows, h).astype(BF16), k_of(rows, h).astype(BF16)
        e = jnp.exp(e_all[rows.start][lev * L:(lev + 1) * L, h * HEAD_K:(h + 1) * HEAD_K])
        return (q_of(rows, h) * e).astype(BF16), (k_of(rows, h) * e).astype(BF16)

    assert len(pairs) % 2 == 0
    zero = jnp.zeros((L, HEAD_K), BF16)
    amat = {}
    for lev in range(nlev):
        for pa, pb in zip(pairs[0::2], pairs[1::2]):
            qa, ka = scaled(*pa, lev)
            qb, kb = scaled(*pb, lev)
            lhs = jnp.concatenate([qa, qb], axis=1)
            rhs_t = jnp.concatenate([jnp.concatenate([ka, zero], axis=1),
                                     jnp.concatenate([zero, kb], axis=1)], axis=0)
            p = _dot_nt(lhs, rhs_t)
            for key, blk in (((pa[0].start, pa[1]), p[:, 0:L]), ((pb[0].start, pb[1]), p[:, L:2 * L])):
                term = wm_ref[lev] * blk
                amat[key] = term if lev == 0 else amat[key] + term

    for rows, h in pairs:
        ks = slice(h * HEAD_K, (h + 1) * HEAD_K)
        vs = slice(h * HEAD_V, (h + 1) * HEAD_V)
        q, k, a = q_of(rows, h), k_of(rows, h), amat[rows.start, h]
        vb = qkv_ref[rows, 2 * D_K + h * HEAD_V:2 * D_K + (h + 1) * HEAD_V].astype(BF16)

        eb = e_all[rows.start][nlev * L:(nlev + 1) * L, ks]
        ek = e_all[rows.start][(nlev + 1) * L:(nlev + 2) * L, ks]
        qh = (q * jnp.exp(eb)).astype(BF16)
        kh = k * jnp.exp(ek)
        s = s_ref[h]
        o = _dot(jnp.concatenate([a.astype(BF16), qh], axis=1),
                 jnp.concatenate([vb, s.astype(BF16)], axis=0))

        blast = jnp.broadcast_to(eb[L - 1:L, :], (HEAD_K, HEAD_K))
        acol = jnp.exp(blast.T)
        acol = jnp.concatenate([acol, acol], axis=1)
        s_ref[h] = acol * s + _dot(kh.T.astype(BF16), vb)

        y = _rms_rows(o, gn) * sg_ref[rows, vs]
        o_ref[rows, vs] = y.astype(o_ref.dtype)

    @pl.when(c == nsteps - 1)
    def _():
        snew_ref[0, 0] = s_ref[...]


def _gla_sample_kernel(la_ref, qkv_ref, sg_ref, st_ref, gn_ref, o_ref, snew_ref, at_ref, *, bblk):
    s_idx = pl.program_id(0)

    @pl.when(s_idx == 0)
    def _():
        for h in range(N_HEADS):
            at_ref[h] = jnp.exp(la_ref[:, h * HEAD_K:(h + 1) * HEAD_K]).T

    gn = gn_ref[...]
    scale = HEAD_K ** -0.5
    lane = lax.broadcasted_iota(jnp.int32, (HEAD_K, LANES), 1)
    row = lax.broadcasted_iota(jnp.int32, (bblk, HEAD_V), 0)
    for h in range(N_HEADS):
        vs = slice(h * HEAD_V, (h + 1) * HEAD_V)
        q = (qkv_ref[:, h * HEAD_K:(h + 1) * HEAD_K] * scale).astype(BF16)
        kt = qkv_ref[:, D_K + h * HEAD_K:D_K + (h + 1) * HEAD_K].T.astype(BF16)
        v = qkv_ref[:, 2 * D_K + h * HEAD_V:2 * D_K + (h + 1) * HEAD_V].astype(BF16)
        vdiag = jnp.concatenate([jnp.where(row == j, v, jnp.zeros_like(v)) for j in range(bblk)], axis=1)
        kv = _dot(kt, vdiag)
        outs = []
        for j in range(bblk):
            b = s_idx * bblk + j
            acol = jnp.sum(jnp.where(lane == b, at_ref[h], 0.0), axis=1, keepdims=True)
            s_new = acol * st_ref[j, h] + kv[:, j * HEAD_V:(j + 1) * HEAD_V]
            snew_ref[0, j, h] = s_new
            outs.append(_dot(q, s_new.astype(BF16))[j:j + 1, :])
        for j in range(bblk):
            o_ref[j:j + 1, vs] = _rms_rows(outs[j], gn) * sg_ref[j:j + 1, vs]


def _ln_silu(y, g, b):
    mu = jnp.mean(y, axis=-1, keepdims=True)
    yc = y - mu
    var = jnp.mean(yc * yc, axis=-1, keepdims=True)
    z = yc * lax.rsqrt(var + LN_EPS) * g + b
    return z * jax.nn.sigmoid(z)


def _conv_sample_kernel(sc_ref, glu_ref, cw_ref, cb_ref, lg_ref, lb_ref, c_ref, cnew_ref):
    nh = CONV_W - 1
    g = glu_ref[...]
    y = g * cw_ref[nh:nh + 1, :] + cb_ref[...]
    for w in range(nh):
        y = y + sc_ref[w] * cw_ref[w:w + 1, :]
    c_ref[...] = _ln_silu(y, lg_ref[...], lb_ref[...])
    for w in range(nh - 1):
        cnew_ref[w] = sc_ref[w + 1]
    cnew_ref[nh - 1] = g


def _fixed(shape, idx=None):
    idx = (0,) * len(shape) if idx is None else idx
    return pl.BlockSpec(shape, lambda *_: idx)


def kernel(x_prompt, x_sample, state_gla, state_conv, norm_mix, w_in, w_gate_up, b_gate, gla_norm,
           conv_w, conv_b, conv_ln_g, conv_ln_b, w_out, norm_ffn, w_ffn_in, w_ffn_out, norm_final):
    B, T, D = x_prompt.shape
    BS = x_sample.shape[0]
    assert state_gla.shape[0] == 1 and x_sample.shape[1] == 1
    d_conv = state_conv.shape[-1]
    d_ff = w_ffn_out.shape[1]
    in_cols = w_in.shape[-1]
    assert d_conv == D_A and in_cols == 2 * D_K + D_A + GATE_RANK + D_A + 2 * d_conv
    MP = B * T
    n_tiles = 8
    tmp, tms = MP // n_tiles, BS // n_tiles
    assert tmp * n_tiles == MP and tms * n_tiles == BS and tms % 8 == 0 and tmp % 256 == 0
    tm = tmp + tms

    xp = x_prompt.reshape(MP, D)
    xs = x_sample.reshape(BS, D)

    def pair_shapes(cols):
        return (jax.ShapeDtypeStruct((MP, cols), F32), jax.ShapeDtypeStruct((BS, cols), F32))

    wt = jnp.swapaxes(w_in, 1, 2)[0]
    nw = (in_cols - GATE_RANK) // MXU_N
    assert nw * MXU_N == in_cols - GATE_RANK and MXU_N % GATE_RANK == 0
    nj = Z_COLS // ZB
    row_i = lambda s: jnp.maximum(s - nw, 0) // nj
    col_j = lambda s: lax.rem(jnp.maximum(s - nw, 0), nj)
    wblk = lambda s: jnp.minimum(s, nw - 1)
    z_p, z_s = pl.pallas_call(
        functools.partial(_in_proj_kernel, nw=nw, nj=nj, tmp=tmp, tms=tms),
        grid=(nw + n_tiles * nj,),
        in_specs=[pl.BlockSpec((MXU_N // 2, D), lambda s: (2 * wblk(s), 0)),
                  pl.BlockSpec((MXU_N // 2, D), lambda s: (2 * wblk(s) + 1, 0)),
                  pl.BlockSpec((GATE_RANK, D), lambda s: ((wblk(s) + 1) * (MXU_N // GATE_RANK), 0)),
                  pl.BlockSpec((tmp, D), lambda s: (row_i(s), 0)),
                  pl.BlockSpec((tms, D), lambda s: (row_i(s), 0)),
                  _fixed((1, D)), _fixed((GATE_RANK, D_K)), _fixed((1, D_K))],
        out_specs=(pl.BlockSpec((tmp, ZB), lambda s: (row_i(s), col_j(s))),
                   pl.BlockSpec((tms, ZB), lambda s: (row_i(s), col_j(s)))),
        out_shape=pair_shapes(Z_COLS),
        scratch_shapes=[pltpu.VMEM((nw, D, MXU_N), BF16), pltpu.VMEM((D, LANES), BF16),
                        pltpu.VMEM((tm, D), BF16)],
        compiler_params=_cparams(("arbitrary",)),
        name="in_proj",
    )(wt, wt, wt, xp, xs, norm_mix[0].reshape(1, D), w_gate_up[0], b_gate[0].reshape(1, D_K))

    chunks_per_step = 4
    L = GLA_CHUNK * chunks_per_step
    nchunks = T // L
    assert nchunks * L == T
    nlev, mall, wm = _gla_constants(GLA_CHUNK)
    gn = gla_norm[0].reshape(1, HEAD_V)
    zrow = lambda b, c: b * nchunks + c
    oa_p, gla_new_p = pl.pallas_call(
        functools.partial(_gla_prompt_kernel, nlev=nlev, nsteps=nchunks, chunks_per_step=chunks_per_step),
        grid=(B, nchunks),
        in_specs=[pl.BlockSpec((L, Z_SG), lambda b, c: (zrow(b, c), 0)),
                  pl.BlockSpec((L, D_K), lambda b, c: (zrow(b, c), Z_LA // D_K)),
                  pl.BlockSpec((L, D_A), lambda b, c: (zrow(b, c), Z_SG // D_A)),
                  _fixed(mall.shape), _fixed(wm.shape),
                  _fixed((1, HEAD_V))],
        out_specs=(pl.BlockSpec((L, D_A), lambda b, c: (zrow(b, c), 0)),
                   pl.BlockSpec((1, 1, N_HEADS, HEAD_K, HEAD_V), lambda b, c: (0, b, 0, 0, 0))),
        out_shape=(jax.ShapeDtypeStruct((MP, D_A), BF16),
                   jax.ShapeDtypeStruct((1, B, N_HEADS, HEAD_K, HEAD_V), F32)),
        scratch_shapes=[pltpu.VMEM((N_HEADS, HEAD_K, HEAD_V), F32)],
        compiler_params=_cparams(("arbitrary", "arbitrary")),
        name="gla_prompt",
    )(z_p, z_p, z_p, jnp.asarray(mall, BF16), jnp.asarray(wm), gn)

    bblk = 8
    assert BS == LANES and BS % bblk == 0
    oa_s, gla_new_s = pl.pallas_call(
        functools.partial(_gla_sample_kernel, bblk=bblk),
        grid=(BS // bblk,),
        in_specs=[_fixed((BS, D_K), (0, Z_LA // D_K)),
                  pl.BlockSpec((bblk, Z_SG), lambda s: (s, 0)),
                  pl.BlockSpec((bblk, D_A), lambda s: (s, Z_SG // D_A)),
                  pl.BlockSpec((bblk, N_HEADS, HEAD_K, HEAD_V), lambda s: (s, 0, 0, 0)),
                  _fixed((1, HEAD_V))],
        out_specs=(pl.BlockSpec((bblk, D_A), lambda s: (s, 0)),
                   pl.BlockSpec((1, bblk, N_HEADS, HEAD_K, HEAD_V), lambda s: (0, s, 0, 0, 0))),
        out_shape=(jax.ShapeDtypeStruct((BS, D_A), F32),
                   jax.ShapeDtypeStruct((1, BS, N_HEADS, HEAD_K, HEAD_V), F32)),
        scratch_shapes=[pltpu.VMEM((N_HEADS, HEAD_K, BS), F32)],
        compiler_params=_cparams(("arbitrary",)),
        name="gla_sample",
    )(z_s, z_s, z_s, state_gla[0], gn)

    nslab = d_conv // LANES
    vec = lambda a: a.reshape(1, d_conv)

    cb_blk = 32
    sc_t = jnp.transpose(state_conv[0], (1, 0, 2))
    c_s, conv_new_s_t = pl.pallas_call(
        _conv_sample_kernel,
        grid=(BS // cb_blk,),
        in_specs=[pl.BlockSpec((CONV_W - 1, cb_blk, d_conv), lambda s: (0, s, 0)),
                  pl.BlockSpec((cb_blk, d_conv), lambda s: (s, Z_GLU // d_conv)),
                  _fixed((CONV_W, d_conv)), _fixed((1, d_conv)), _fixed((1, d_conv)), _fixed((1, d_conv))],
        out_specs=(pl.BlockSpec((cb_blk, d_conv), lambda s: (s, 0)),
                   pl.BlockSpec((CONV_W - 1, cb_blk, d_conv), lambda s: (0, s, 0))),
        out_shape=(jax.ShapeDtypeStruct((BS, d_conv), F32),
                   jax.ShapeDtypeStruct((CONV_W - 1, BS, d_conv), F32)),
        compiler_params=_cparams(("arbitrary",)),
        name="conv_sample",
    )(sc_t, z_s, conv_w[0], vec(conv_b[0]), vec(conv_ln_g[0]), vec(conv_ln_b[0]))
    conv_new_s = jnp.transpose(conv_new_s_t, (1, 0, 2))[None]

    tt = min(1024, T)
    hist = 32
    nsteps = T // tt
    c_p, conv_new_p = pl.pallas_call(
        functools.partial(_conv_prompt_kernel, tt=tt, nsteps=nsteps, hist=hist),
        grid=(B, nsteps),
        in_specs=[pl.BlockSpec((tt, d_conv), lambda b, t: (b * nsteps + t, Z_GLU // d_conv)),
                  _fixed((CONV_W, nslab, LANES)), _fixed((nslab, LANES)),
                  _fixed((1, d_conv)), _fixed((1, d_conv))],
        out_specs=(pl.BlockSpec((tt, d_conv), lambda b, t: (b * nsteps + t, 0)),
                   pl.BlockSpec((1, 1, CONV_W - 1, d_conv), lambda b, t: (0, b, 0, 0))),
        out_shape=(jax.ShapeDtypeStruct((MP, d_conv), BF16),
                   jax.ShapeDtypeStruct((1, B, CONV_W - 1, d_conv), F32)),
        scratch_shapes=[pltpu.VMEM((nslab, hist + tt, LANES), F32), pltpu.VMEM((nslab, tt, LANES), F32)],
        compiler_params=_cparams(("arbitrary", "arbitrary")),
        name="conv_prompt",
    )(z_p, conv_w[0].reshape(CONV_W, nslab, LANES), conv_b[0].reshape(nslab, LANES),
      vec(conv_ln_g[0]), vec(conv_ln_b[0]))

    zbo = min(1024, D)
    wcols = min(2 * MXU_N, D)
    nwo = D // wcols
    njo = D // zbo
    assert nwo * wcols == D and njo * zbo == D
    row_o = lambda s: jnp.maximum(s - nwo, 0) // njo
    col_o = lambda s: lax.rem(jnp.maximum(s - nwo, 0), njo)
    x1_p, x1_s = pl.pallas_call(
        functools.partial(_outproj_kernel, nw=nwo, nj=njo, tmp=tmp, tms=tms),
        grid=(nwo + n_tiles * njo,),
        in_specs=[pl.BlockSpec((D_A + d_conv, wcols), lambda s: (0, jnp.minimum(s, nwo - 1))),
                  pl.BlockSpec((tmp, zbo), lambda s: (row_o(s), col_o(s))),
                  pl.BlockSpec((tms, zbo), lambda s: (row_o(s), col_o(s))),
                  pl.BlockSpec((tmp, D_A), lambda s: (row_o(s), 0)),
                  pl.BlockSpec((tms, D_A), lambda s: (row_o(s), 0)),
                  pl.BlockSpec((tmp, d_conv), lambda s: (row_o(s), 0)),
                  pl.BlockSpec((tms, d_conv), lambda s: (row_o(s), 0))],
        out_specs=(pl.BlockSpec((tmp, zbo), lambda s: (row_o(s), col_o(s))),
                   pl.BlockSpec((tms, zbo), lambda s: (row_o(s), col_o(s)))),
        out_shape=pair_shapes(D),
        scratch_shapes=[pltpu.VMEM((D // MXU_N, D_A + d_conv, MXU_N), BF16),
                        pltpu.VMEM((tm, D_A), BF16), pltpu.VMEM((tm, d_conv), BF16)],
        compiler_params=_cparams(("arbitrary",)),
        name="out_proj",
    )(w_out[0], xp, xs, oa_p, oa_s, c_p, c_s)

    tf = 256
    nf = d_ff // tf
    assert nf * tf == d_ff
    y_p, y_s = pl.pallas_call(
        functools.partial(_ffn_kernel, tmp=tmp, tms=tms, nf=nf),
        grid=(n_tiles, nf),
        in_specs=[pl.BlockSpec((tmp, D), lambda i, j: (i, 0)), pl.BlockSpec((tms, D), lambda i, j: (i, 0)),
                  _fixed((1, D)), _fixed((1, D)),
                  pl.BlockSpec((D, tf), lambda i, j: (0, j)),
                  pl.BlockSpec((D, tf), lambda i, j: (0, j + nf)),
                  pl.BlockSpec((tf, D), lambda i, j: (j, 0))],
        out_specs=(pl.BlockSpec((tmp, D), lambda i, j: (i, 0)), pl.BlockSpec((tms, D), lambda i, j: (i, 0))),
        out_shape=pair_shapes(D),
        scratch_shapes=[pltpu.VMEM((tm, D), BF16)],
        compiler_params=_cparams(("arbitrary", "arbitrary")),
        name="ffn",
    )(x1_p, x1_s, norm_ffn[0].reshape(1, D), norm_final.reshape(1, D), w_ffn_in[0], w_ffn_in[0], w_ffn_out[0])

    return (y_p.reshape(B, T, D), y_s.reshape(BS, 1, D), gla_new_p, conv_new_p, gla_new_s, conv_new_s)
```

```python
import functools

import numpy as np
import jax
import jax.numpy as jnp
from jax import lax
from jax.experimental import pallas as pl
from jax.experimental.pallas import tpu as pltpu

F32 = jnp.float32
BF16 = jnp.bfloat16

RMS_EPS = 1e-6
LN_EPS = 1e-5
GATE_NORM = 16.0
N_HEADS = 4
HEAD_K = 128
HEAD_V = 256
D_K = N_HEADS * HEAD_K
D_A = N_HEADS * HEAD_V
GATE_RANK = 16
CONV_W = 31

LANES = 128
MXU_N = 256
GLA_CHUNK = 128
VMEM_LIMIT = 58 * 1024 * 1024

ZB = 512
Z_QKV, Z_SG, Z_GLU, Z_LA = 0, 2 * D_K + D_A, 2 * D_K + 2 * D_A, 2 * D_K + 3 * D_A
Z_COLS = Z_LA + D_K


def _cparams(sem):
    return pltpu.CompilerParams(dimension_semantics=sem, vmem_limit_bytes=VMEM_LIMIT)


def _dot(a, b):
    return jnp.dot(a, b, preferred_element_type=F32)


def _dot_nt(a, b):
    return lax.dot_general(a, b, (((1,), (1,)), ((), ())), preferred_element_type=F32)


def _rms_rows(x, gain):
    ms = jnp.mean(x * x, axis=-1, keepdims=True)
    return x * lax.rsqrt(ms + RMS_EPS) * gain


def _log_sigmoid(x):
    return jnp.minimum(x, 0.0) - jnp.log(1.0 + jnp.exp(-jnp.abs(x)))


def _norm_to_scratch(h_ref, xp_ref, xs_ref, gain, tmp, tms, rows_per_pass=256):
    for r0 in range(0, tmp, rows_per_pass):
        h_ref[r0:r0 + rows_per_pass, :] = _rms_rows(xp_ref[r0:r0 + rows_per_pass, :], gain).astype(BF16)
    h_ref[tmp:tmp + tms, :] = _rms_rows(xs_ref[...], gain).astype(BF16)


def _in_proj_kernel(wt_ref, wnext_ref, xp_ref, xs_ref, g_ref, wup_ref, bg_ref, zp_ref, zs_ref,
                    wb_ref, wlr_ref, h_ref, *, nw, nj, tmp, tms):
    s = pl.program_id(0)
    rb = MXU_N
    n_direct = (2 * D_K + D_A) // rb

    @pl.when(s < n_direct)
    def _():
        wb_ref[s] = wt_ref[...].T.astype(BF16)

    @pl.when((s >= n_direct) & (s < nw))
    def _():
        rows = jnp.concatenate([wt_ref[GATE_RANK:rb, :], wnext_ref[...]], axis=0)
        wb_ref[s] = rows.T.astype(BF16)

    @pl.when(s == n_direct)
    def _():
        wlr_ref[...] = wt_ref[0:LANES, :].T.astype(BF16)

    @pl.when(s >= nw)
    def _project():
        j = lax.rem(s - nw, nj)

        @pl.when(j == 0)
        def _():
            _norm_to_scratch(h_ref, xp_ref, xs_ref, g_ref[...], tmp, tms)

        def pair(t0):
            h = h_ref[...]
            return jnp.concatenate([_dot(h, wb_ref[t0]), _dot(h, wb_ref[t0 + 1])], axis=1)

        def put(z):
            zp_ref[...] = z[:tmp]
            zs_ref[...] = z[tmp:]

        nqkv, nsg, nglu = Z_SG // ZB, (Z_GLU - Z_SG) // ZB, (Z_LA - Z_GLU) // ZB
        bpz = ZB // rb

        @pl.when(j < nqkv)
        def _():
            put(pair(bpz * j))

        @pl.when((j >= nqkv) & (j < nqkv + nsg))
        def _():
            z = pair(bpz * j)
            put(z * jax.nn.sigmoid(z))

        @pl.when((j >= nqkv + nsg) & (j < nqkv + nsg + nglu))
        def _():
            ta = bpz * j
            ua = pair(ta)
            ug = pair(ta + bpz * nglu)
            put(ua * jax.nn.sigmoid(ug))

        @pl.when(j == nqkv + nsg + nglu)
        def _():
            glr = _dot(h_ref[...], wlr_ref[...])[:, 0:GATE_RANK]
            logit = _dot(glr, wup_ref[...]) + bg_ref[...]
            put(_log_sigmoid(logit) * (1.0 / GATE_NORM))


def _conv_prompt_kernel(glu_ref, cw_ref, cb_ref, lg_ref, lb_ref, c_ref, cnew_ref, slab_ref, y_ref,
                        *, tt, nsteps, hist):
    t_idx = pl.program_id(1)
    nslab = slab_ref.shape[0]
    pad = hist - (CONV_W - 1)

    @pl.when(t_idx == 0)
    def _():
        slab_ref[:, 0:hist, :] = jnp.zeros((nslab, hist, LANES), F32)

    @pl.when(t_idx > 0)
    def _():
        slab_ref[:, 0:hist, :] = slab_ref[:, tt:tt + hist, :]

    for sl in range(nslab):
        slab_ref[sl, hist:hist + tt, :] = glu_ref[:, sl * LANES:(sl + 1) * LANES]

    def slab_conv(sl, carry):
        acc = None
        for res in range(8):
            part = None
            for w in range(res, CONV_W, 8):
                term = slab_ref[sl, pad + w:pad + w + tt, :] * cw_ref[w, pl.ds(sl, 1), :]
                part = term if part is None else part + term
            acc = part if acc is None else acc + part
        y_ref[sl] = acc + cb_ref[pl.ds(sl, 1), :]
        return carry

    lax.fori_loop(0, nslab, slab_conv, 0)

    y = y_ref[...]
    inv_c = 1.0 / (nslab * LANES)
    mu = jnp.sum(jnp.sum(y, axis=0), axis=-1, keepdims=True) * inv_c
    yc = y - mu[None]
    var = jnp.sum(jnp.sum(yc * yc, axis=0), axis=-1, keepdims=True) * inv_c
    rstd = lax.rsqrt(var + LN_EPS)
    for sl in range(nslab):
        cs = slice(sl * LANES, (sl + 1) * LANES)
        z = yc[sl] * rstd * lg_ref[:, cs] + lb_ref[:, cs]
        c_ref[:, cs] = (z * jax.nn.sigmoid(z)).astype(c_ref.dtype)

    @pl.when(t_idx == nsteps - 1)
    def _():
        for sl in range(nslab):
            cnew_ref[0, 0, :, sl * LANES:(sl + 1) * LANES] = slab_ref[sl, hist + tt - (CONV_W - 1):hist + tt, :]


def _outproj_kernel(w_ref, xp_ref, xs_ref, oap_ref, oas_ref, cp_ref, cs_ref, op_ref, os_ref,
                    wb_ref, a_ref, c_ref, *, nw, nj, tmp, tms):
    s = pl.program_id(0)
    rb = MXU_N
    wpl = w_ref.shape[1] // rb
    bpc = xp_ref.shape[1] // rb

    @pl.when(s < nw)
    def _load_weight():
        for q in range(wpl):
            wb_ref[s * wpl + q] = w_ref[:, q * rb:(q + 1) * rb].astype(BF16)

    @pl.when(s >= nw)
    def _project():
        j = lax.rem(s - nw, nj)

        @pl.when(j == 0)
        def _():
            a_ref[0:tmp, :] = oap_ref[...]
            a_ref[tmp:tmp + tms, :] = oas_ref[...].astype(BF16)
            c_ref[0:tmp, :] = cp_ref[...]
            c_ref[tmp:tmp + tms, :] = cs_ref[...].astype(BF16)

        for half in range(bpc):
            t = bpc * j + half
            cols = slice(half * rb, (half + 1) * rb)
            z = _dot(a_ref[...], wb_ref[t, 0:D_A, :]) + _dot(c_ref[...], wb_ref[t, D_A:, :])
            op_ref[:, cols] = xp_ref[:, cols] + z[:tmp]
            os_ref[:, cols] = xs_ref[:, cols] + z[tmp:]


def _ffn_kernel(xp_ref, xs_ref, g_ref, gf_ref, wg_ref, wu_ref, wo_ref, op_ref, os_ref, h_ref,
                *, tmp, tms, nf):
    j = pl.program_id(1)

    def ffn_block(acc_p, acc_s):
        h = h_ref[...]
        fg = _dot(h, wg_ref[...].astype(BF16))
        fu = _dot(h, wu_ref[...].astype(BF16))
        act = (fg * jax.nn.sigmoid(fg) * fu).astype(BF16)
        wo = wo_ref[...].astype(BF16)
        half = tmp // 2
        op_ref[0:half, :] = acc_p[0:half, :] + _dot(act[0:half], wo)
        z = _dot(act[half:], wo)
        op_ref[half:tmp, :] = acc_p[half:tmp, :] + z[:tmp - half]
        os_ref[...] = acc_s[...] + z[tmp - half:]

    @pl.when(j == 0)
    def _():
        _norm_to_scratch(h_ref, xp_ref, xs_ref, g_ref[...], tmp, tms)
        ffn_block(xp_ref, xs_ref)

    @pl.when(j > 0)
    def _():
        ffn_block(op_ref, os_ref)

    @pl.when(j == nf - 1)
    def _():
        gf = gf_ref[...]
        for r0 in range(0, tmp, 256):
            op_ref[r0:r0 + 256, :] = _rms_rows(op_ref[r0:r0 + 256, :], gf)
        os_ref[...] = _rms_rows(os_ref[...], gf)


@functools.lru_cache(maxsize=None)
def _gla_constants(L):
    nlev = int(np.log2(L)) + 1
    m = np.zeros((nlev + 2, L, L), np.float32)
    wm = np.zeros((nlev, L, L), np.float32)
    idx = np.arange(L)
    wm[0] = np.eye(L, dtype=np.float32)
    for lev in range(1, nlev):
        half = L >> lev
        pos = idx % (2 * half)
        upper = pos >= half
        r = idx - pos + half - 1
        t = idx[None, :]
        up_rows = (t > r[:, None]) & (t <= idx[:, None]) & upper[:, None]
        lo_rows = (t > idx[:, None]) & (t <= r[:, None]) & (~upper)[:, None]
        m[lev] = (up_rows | lo_rows).astype(np.float32)
        same = (idx[:, None] // (2 * half)) == (idx[None, :] // (2 * half))
        wm[lev] = (same & upper[:, None] & (~upper)[None, :]).astype(np.float32)
    m[nlev] = (idx[None, :] <= idx[:, None]).astype(np.float32)
    m[nlev + 1] = (idx[None, :] > idx[:, None]).astype(np.float32)
    m = m.reshape((nlev + 2) * L, L)
    return nlev, np.concatenate([m, m], axis=1), wm


def _gla_prompt_kernel(qkv_ref, la_ref, sg_ref, mall_ref, wm_ref, gn_ref,
                       o_ref, snew_ref, s_ref, *, nlev, nsteps, chunks_per_step):
    c = pl.program_id(1)

    @pl.when(c == 0)
    def _():
        s_ref[...] = jnp.zeros_like(s_ref)

    L = GLA_CHUNK
    scale = HEAD_K ** -0.5
    gn = gn_ref[...]
    chunks = [slice(sub * L, (sub + 1) * L) for sub in range(chunks_per_step)]
    pairs = [(rows, h) for rows in chunks for h in range(N_HEADS)]

    def q_of(rows, h):
        return qkv_ref[rows, h * HEAD_K:(h + 1) * HEAD_K] * scale

    def k_of(rows, h):
        return qkv_ref[rows, D_K + h * HEAD_K:D_K + (h + 1) * HEAD_K]

    mall2 = mall_ref[...]
    e_all = {}
    for rows in chunks:
        la = la_ref[rows, :]
        la_hi = la.astype(BF16)
        la_lo = (la - la_hi.astype(F32)).astype(BF16)
        e_all[rows.start] = _dot(mall2, jnp.concatenate([la_hi, la_lo], axis=0))

    def scaled(rows, h, lev):
        if lev == 0:
            return q_of(rows, h).astype(BF16), k_of(rows, h).astype(BF16)
        e = jnp.exp(e_all[rows.start][lev * L:(lev + 1) * L, h * HEAD_K:(h + 1) * HEAD_K])
        return (q_of(rows, h) * e).astype(BF16), (k_of(rows, h) * e).astype(BF16)

    assert len(pairs) % 2 == 0
    zero = jnp.zeros((L, HEAD_K), BF16)
    amat = {}
    for lev in range(nlev):
        for pa, pb in zip(pairs[0::2], pairs[1::2]):
            qa, ka = scaled(*pa, lev)
            qb, kb = scaled(*pb, lev)
            lhs = jnp.concatenate([qa, qb], axis=1)
            rhs_t = jnp.concatenate([jnp.concatenate([ka, zero], axis=1),
                                     jnp.concatenate([zero, kb], axis=1)], axis=0)
            p = _dot_nt(lhs, rhs_t)
            for key, blk in (((pa[0].start, pa[1]), p[:, 0:L]), ((pb[0].start, pb[1]), p[:, L:2 * L])):
                term = wm_ref[lev] * blk
                amat[key] = term if lev == 0 else amat[key] + term

    for rows, h in pairs:
        ks = slice(h * HEAD_K, (h + 1) * HEAD_K)
        vs = slice(h * HEAD_V, (h + 1) * HEAD_V)
        q, k, a = q_of(rows, h), k_of(rows, h), amat[rows.start, h]
        vb = qkv_ref[rows, 2 * D_K + h * HEAD_V:2 * D_K + (h + 1) * HEAD_V].astype(BF16)

        eb = e_all[rows.start][nlev * L:(nlev + 1) * L, ks]
        ek = e_all[rows.start][(nlev + 1) * L:(nlev + 2) * L, ks]
        qh = (q * jnp.exp(eb)).astype(BF16)
        kh = k * jnp.exp(ek)
        s = s_ref[h]
        o = _dot(jnp.concatenate([a.astype(BF16), qh], axis=1),
                 jnp.concatenate([vb, s.astype(BF16)], axis=0))

        blast = jnp.broadcast_to(eb[L - 1:L, :], (HEAD_K, HEAD_K))
        acol = jnp.exp(blast.T)
        acol = jnp.concatenate([acol, acol], axis=1)
        s_ref[h] = acol * s + _dot(kh.T.astype(BF16), vb)

        y = _rms_rows(o, gn) * sg_ref[rows, vs]
        o_ref[rows, vs] = y.astype(o_ref.dtype)

    @pl.when(c == nsteps - 1)
    def _():
        snew_ref[0, 0] = s_ref[...]


def _gla_sample_kernel(la_ref, qkv_ref, sg_ref, st_ref, gn_ref, o_ref, snew_ref, at_ref, *, bblk):
    s_idx = pl.program_id(0)

    @pl.when(s_idx == 0)
    def _():
        for h in range(N_HEADS):
            at_ref[h] = jnp.exp(la_ref[:, h * HEAD_K:(h + 1) * HEAD_K]).T

    gn = gn_ref[...]
    scale = HEAD_K ** -0.5
    lane = lax.broadcasted_iota(jnp.int32, (HEAD_K, LANES), 1)
    row = lax.broadcasted_iota(jnp.int32, (bblk, HEAD_V), 0)
    for h in range(N_HEADS):
        vs = slice(h * HEAD_V, (h + 1) * HEAD_V)
        q = (qkv_ref[:, h * HEAD_K:(h + 1) * HEAD_K] * scale).astype(BF16)
        kt = qkv_ref[:, D_K + h * HEAD_K:D_K + (h + 1) * HEAD_K].T.astype(BF16)
        v = qkv_ref[:, 2 * D_K + h * HEAD_V:2 * D_K + (h + 1) * HEAD_V].astype(BF16)
        vdiag = jnp.concatenate([jnp.where(row == j, v, jnp.zeros_like(v)) for j in range(bblk)], axis=1)
        kv = _dot(kt, vdiag)
        outs = []
        for j in range(bblk):
            b = s_idx * bblk + j
            acol = jnp.sum(jnp.where(lane == b, at_ref[h], 0.0), axis=1, keepdims=True)
            s_new = acol * st_ref[j, h] + kv[:, j * HEAD_V:(j + 1) * HEAD_V]
            snew_ref[0, j, h] = s_new
            outs.append(_dot(q, s_new.astype(BF16))[j:j + 1, :])
        for j in range(bblk):
            o_ref[j:j + 1, vs] = _rms_rows(outs[j], gn) * sg_ref[j:j + 1, vs]


def _ln_silu(y, g, b):
    mu = jnp.mean(y, axis=-1, keepdims=True)
    yc = y - mu
    var = jnp.mean(yc * yc, axis=-1, keepdims=True)
    z = yc * lax.rsqrt(var + LN_EPS) * g + b
    return z * jax.nn.sigmoid(z)


def _conv_sample_kernel(sc_ref, glu_ref, cw_ref, cb_ref, lg_ref, lb_ref, c_ref, cnew_ref):
    nh = CONV_W - 1
    g = glu_ref[...]
    y = g * cw_ref[nh:nh + 1, :] + cb_ref[...]
    for w in range(nh):
        y = y + sc_ref[w] * cw_ref[w:w + 1, :]
    c_ref[...] = _ln_silu(y, lg_ref[...], lb_ref[...])
    for w in range(nh - 1):
        cnew_ref[w] = sc_ref[w + 1]
    cnew_ref[nh - 1] = g


def _fixed(shape, idx=None):
    idx = (0,) * len(shape) if idx is None else idx
    return pl.BlockSpec(shape, lambda *_: idx)


def kernel(x_prompt, x_sample, state_gla, state_conv, norm_mix, w_in, w_gate_up, b_gate, gla_norm,
           conv_w, conv_b, conv_ln_g, conv_ln_b, w_out, norm_ffn, w_ffn_in, w_ffn_out, norm_final):
    B, T, D = x_prompt.shape
    BS = x_sample.shape[0]
    assert state_gla.shape[0] == 1 and x_sample.shape[1] == 1
    d_conv = state_conv.shape[-1]
    d_ff = w_ffn_out.shape[1]
    in_cols = w_in.shape[-1]
    assert d_conv == D_A and in_cols == 2 * D_K + D_A + GATE_RANK + D_A + 2 * d_conv
    MP = B * T
    n_tiles = 8
    tmp, tms = MP // n_tiles, BS // n_tiles
    assert tmp * n_tiles == MP and tms * n_tiles == BS and tms % 8 == 0 and tmp % 256 == 0
    tm = tmp + tms

    xp = x_prompt.reshape(MP, D)
    xs = x_sample.reshape(BS, D)

    def pair_shapes(cols):
        return (jax.ShapeDtypeStruct((MP, cols), F32), jax.ShapeDtypeStruct((BS, cols), F32))

    wt = jnp.swapaxes(w_in, 1, 2)[0]
    nw = (in_cols - GATE_RANK) // MXU_N
    assert nw * MXU_N == in_cols - GATE_RANK and MXU_N % GATE_RANK == 0
    nj = Z_COLS // ZB
    row_i = lambda s: jnp.maximum(s - nw, 0) // nj
    col_j = lambda s: lax.rem(jnp.maximum(s - nw, 0), nj)
    wblk = lambda s: jnp.minimum(s, nw - 1)
    z_p, z_s = pl.pallas_call(
        functools.partial(_in_proj_kernel, nw=nw, nj=nj, tmp=tmp, tms=tms),
        grid=(nw + n_tiles * nj,),
        in_specs=[pl.BlockSpec((MXU_N, D), lambda s: (wblk(s), 0)),
                  pl.BlockSpec((GATE_RANK, D), lambda s: ((wblk(s) + 1) * (MXU_N // GATE_RANK), 0)),
                  pl.BlockSpec((tmp, D), lambda s: (row_i(s), 0)),
                  pl.BlockSpec((tms, D), lambda s: (row_i(s), 0)),
                  _fixed((1, D)), _fixed((GATE_RANK, D_K)), _fixed((1, D_K))],
        out_specs=(pl.BlockSpec((tmp, ZB), lambda s: (row_i(s), col_j(s))),
                   pl.BlockSpec((tms, ZB), lambda s: (row_i(s), col_j(s)))),
        out_shape=pair_shapes(Z_COLS),
        scratch_shapes=[pltpu.VMEM((nw, D, MXU_N), BF16), pltpu.VMEM((D, LANES), BF16),
                        pltpu.VMEM((tm, D), BF16)],
        compiler_params=_cparams(("arbitrary",)),
        name="in_proj",
    )(wt, wt, xp, xs, norm_mix[0].reshape(1, D), w_gate_up[0], b_gate[0].reshape(1, D_K))

    chunks_per_step = 4
    L = GLA_CHUNK * chunks_per_step
    nchunks = T // L
    assert nchunks * L == T
    nlev, mall, wm = _gla_constants(GLA_CHUNK)
    gn = gla_norm[0].reshape(1, HEAD_V)
    zrow = lambda b, c: b * nchunks + c
    oa_p, gla_new_p = pl.pallas_call(
        functools.partial(_gla_prompt_kernel, nlev=nlev, nsteps=nchunks, chunks_per_step=chunks_per_step),
        grid=(B, nchunks),
        in_specs=[pl.BlockSpec((L, Z_SG), lambda b, c: (zrow(b, c), 0)),
                  pl.BlockSpec((L, D_K), lambda b, c: (zrow(b, c), Z_LA // D_K)),
                  pl.BlockSpec((L, D_A), lambda b, c: (zrow(b, c), Z_SG // D_A)),
                  _fixed(mall.shape), _fixed(wm.shape),
                  _fixed((1, HEAD_V))],
        out_specs=(pl.BlockSpec((L, D_A), lambda b, c: (zrow(b, c), 0)),
                   pl.BlockSpec((1, 1, N_HEADS, HEAD_K, HEAD_V), lambda b, c: (0, b, 0, 0, 0))),
        out_shape=(jax.ShapeDtypeStruct((MP, D_A), BF16),
                   jax.ShapeDtypeStruct((1, B, N_HEADS, HEAD_K, HEAD_V), F32)),
        scratch_shapes=[pltpu.VMEM((N_HEADS, HEAD_K, HEAD_V), F32)],
        compiler_params=_cparams(("arbitrary", "arbitrary")),
        name="gla_prompt",
    )(z_p, z_p, z_p, jnp.asarray(mall, BF16), jnp.asarray(wm), gn)

    bblk = 8
    assert BS == LANES and BS % bblk == 0
    oa_s, gla_new_s = pl.pallas_call(
        functools.partial(_gla_sample_kernel, bblk=bblk),
        grid=(BS // bblk,),
        in_specs=[_fixed((BS, D_K), (0, Z_LA // D_K)),
                  pl.BlockSpec((bblk, Z_SG), lambda s: (s, 0)),
                  pl.BlockSpec((bblk, D_A), lambda s: (s, Z_SG // D_A)),
                  pl.BlockSpec((bblk, N_HEADS, HEAD_K, HEAD_V), lambda s: (s, 0, 0, 0)),
                  _fixed((1, HEAD_V))],
        out_specs=(pl.BlockSpec((bblk, D_A), lambda s: (s, 0)),
                   pl.BlockSpec((1, bblk, N_HEADS, HEAD_K, HEAD_V), lambda s: (0, s, 0, 0, 0))),
        out_shape=(jax.ShapeDtypeStruct((BS, D_A), F32),
                   jax.ShapeDtypeStruct((1, BS, N_HEADS, HEAD_K, HEAD_V), F32)),
        scratch_shapes=[pltpu.VMEM((N_HEADS, HEAD_K, BS), F32)],
        compiler_params=_cparams(("arbitrary",)),
        name="gla_sample",
    )(z_s, z_s, z_s, state_gla[0], gn)

    nslab = d_conv // LANES
    vec = lambda a: a.reshape(1, d_conv)

    cb_blk = 32
    sc_t = jnp.transpose(state_conv[0], (1, 0, 2))
    c_s, conv_new_s_t = pl.pallas_call(
        _conv_sample_kernel,
        grid=(BS // cb_blk,),
        in_specs=[pl.BlockSpec((CONV_W - 1, cb_blk, d_conv), lambda s: (0, s, 0)),
                  pl.BlockSpec((cb_blk, d_conv), lambda s: (s, Z_GLU // d_conv)),
                  _fixed((CONV_W, d_conv)), _fixed((1, d_conv)), _fixed((1, d_conv)), _fixed((1, d_conv))],
        out_specs=(pl.BlockSpec((cb_blk, d_conv), lambda s: (s, 0)),
                   pl.BlockSpec((CONV_W - 1, cb_blk, d_conv), lambda s: (0, s, 0))),
        out_shape=(jax.ShapeDtypeStruct((BS, d_conv), F32),
                   jax.ShapeDtypeStruct((CONV_W - 1, BS, d_conv), F32)),
        compiler_params=_cparams(("arbitrary",)),
        name="conv_sample",
    )(sc_t, z_s, conv_w[0], vec(conv_b[0]), vec(conv_ln_g[0]), vec(conv_ln_b[0]))
    conv_new_s = jnp.transpose(conv_new_s_t, (1, 0, 2))[None]

    tt = 512
    hist = 32
    nsteps = T // tt
    c_p, conv_new_p = pl.pallas_call(
        functools.partial(_conv_prompt_kernel, tt=tt, nsteps=nsteps, hist=hist),
        grid=(B, nsteps),
        in_specs=[pl.BlockSpec((tt, d_conv), lambda b, t: (b * nsteps + t, Z_GLU // d_conv)),
                  _fixed((CONV_W, nslab, LANES)), _fixed((nslab, LANES)),
                  _fixed((1, d_conv)), _fixed((1, d_conv))],
        out_specs=(pl.BlockSpec((tt, d_conv), lambda b, t: (b * nsteps + t, 0)),
                   pl.BlockSpec((1, 1, CONV_W - 1, d_conv), lambda b, t: (0, b, 0, 0))),
        out_shape=(jax.ShapeDtypeStruct((MP, d_conv), BF16),
                   jax.ShapeDtypeStruct((1, B, CONV_W - 1, d_conv), F32)),
        scratch_shapes=[pltpu.VMEM((nslab, hist + tt, LANES), F32), pltpu.VMEM((nslab, tt, LANES), F32)],
        compiler_params=_cparams(("arbitrary", "arbitrary")),
        name="conv_prompt",
    )(z_p, conv_w[0].reshape(CONV_W, nslab, LANES), conv_b[0].reshape(nslab, LANES),
      vec(conv_ln_g[0]), vec(conv_ln_b[0]))

    zbo = min(1024, D)
    wcols = min(2 * MXU_N, D)
    nwo = D // wcols
    njo = D // zbo
    assert nwo * wcols == D and njo * zbo == D
    row_o = lambda s: jnp.maximum(s - nwo, 0) // njo
    col_o = lambda s: lax.rem(jnp.maximum(s - nwo, 0), njo)
    x1_p, x1_s = pl.pallas_call(
        functools.partial(_outproj_kernel, nw=nwo, nj=njo, tmp=tmp, tms=tms),
        grid=(nwo + n_tiles * njo,),
        in_specs=[pl.BlockSpec((D_A + d_conv, wcols), lambda s: (0, jnp.minimum(s, nwo - 1))),
                  pl.BlockSpec((tmp, zbo), lambda s: (row_o(s), col_o(s))),
                  pl.BlockSpec((tms, zbo), lambda s: (row_o(s), col_o(s))),
                  pl.BlockSpec((tmp, D_A), lambda s: (row_o(s), 0)),
                  pl.BlockSpec((tms, D_A), lambda s: (row_o(s), 0)),
                  pl.BlockSpec((tmp, d_conv), lambda s: (row_o(s), 0)),
                  pl.BlockSpec((tms, d_conv), lambda s: (row_o(s), 0))],
        out_specs=(pl.BlockSpec((tmp, zbo), lambda s: (row_o(s), col_o(s))),
                   pl.BlockSpec((tms, zbo), lambda s: (row_o(s), col_o(s)))),
        out_shape=pair_shapes(D),
        scratch_shapes=[pltpu.VMEM((D // MXU_N, D_A + d_conv, MXU_N), BF16),
                        pltpu.VMEM((tm, D_A), BF16), pltpu.VMEM((tm, d_conv), BF16)],
        compiler_params=_cparams(("arbitrary",)),
        name="out_proj",
    )(w_out[0], xp, xs, oa_p, oa_s, c_p, c_s)

    tf = 256
    nf = d_ff // tf
    assert nf * tf == d_ff
    y_p, y_s = pl.pallas_call(
        functools.partial(_ffn_kernel, tmp=tmp, tms=tms, nf=nf),
        grid=(n_tiles, nf),
        in_specs=[pl.BlockSpec((tmp, D), lambda i, j: (i, 0)), pl.BlockSpec((tms, D), lambda i, j: (i, 0)),
                  _fixed((1, D)), _fixed((1, D)),
                  pl.BlockSpec((D, tf), lambda i, j: (0, j)),
                  pl.BlockSpec((D, tf), lambda i, j: (0, j + nf)),
                  pl.BlockSpec((tf, D), lambda i, j: (j, 0))],
        out_specs=(pl.BlockSpec((tmp, D), lambda i, j: (i, 0)), pl.BlockSpec((tms, D), lambda i, j: (i, 0))),
        out_shape=pair_shapes(D),
        scratch_shapes=[pltpu.VMEM((tm, D), BF16)],
        compiler_params=_cparams(("arbitrary", "arbitrary")),
        name="ffn",
    )(x1_p, x1_s, norm_ffn[0].reshape(1, D), norm_final.reshape(1, D), w_ffn_in[0], w_ffn_in[0], w_ffn_out[0])

    return (y_p.reshape(B, T, D), y_s.reshape(BS, 1, D), gla_new_p, conv_new_p, gla_new_s, conv_new_s)
```

```python
import functools

import numpy as np
import jax
import jax.numpy as jnp
from jax import lax
from jax.experimental import pallas as pl
from jax.experimental.pallas import tpu as pltpu

F32 = jnp.float32
BF16 = jnp.bfloat16

RMS_EPS = 1e-6
LN_EPS = 1e-5
GATE_NORM = 16.0
N_HEADS = 4
HEAD_K = 128
HEAD_V = 256
D_K = N_HEADS * HEAD_K
D_A = N_HEADS * HEAD_V
GATE_RANK = 16
CONV_W = 31

LANES = 128
MXU_N = 256
GLA_CHUNK = 128
VMEM_LIMIT = 58 * 1024 * 1024

ZB = 512
Z_QKV, Z_SG, Z_GLU, Z_LA = 0, 2 * D_K + D_A, 2 * D_K + 2 * D_A, 2 * D_K + 3 * D_A
Z_COLS = Z_LA + D_K


def _cparams(sem):
    return pltpu.CompilerParams(dimension_semantics=sem, vmem_limit_bytes=VMEM_LIMIT)


def _dot(a, b):
    return jnp.dot(a, b, preferred_element_type=F32)


def _dot_nt(a, b):
    return lax.dot_general(a, b, (((1,), (1,)), ((), ())), preferred_element_type=F32)


def _rms_rows(x, gain):
    ms = jnp.mean(x * x, axis=-1, keepdims=True)
    return x * lax.rsqrt(ms + RMS_EPS) * gain


def _log_sigmoid(x):
    return jnp.minimum(x, 0.0) - jnp.log(1.0 + jnp.exp(-jnp.abs(x)))


def _norm_to_scratch(h_ref, xp_ref, xs_ref, gain, tmp, tms, rows_per_pass=256):
    for r0 in range(0, tmp, rows_per_pass):
        h_ref[r0:r0 + rows_per_pass, :] = _rms_rows(xp_ref[r0:r0 + rows_per_pass, :], gain).astype(BF16)
    h_ref[tmp:tmp + tms, :] = _rms_rows(xs_ref[...], gain).astype(BF16)


def _in_proj_kernel(wt_ref, wnext_ref, xp_ref, xs_ref, g_ref, wup_ref, bg_ref, zp_ref, zs_ref,
                    wb_ref, wlr_ref, h_ref, *, nw, nj, tmp, tms):
    s = pl.program_id(0)
    rb = MXU_N
    n_direct = (2 * D_K + D_A) // rb

    @pl.when(s < n_direct)
    def _():
        wb_ref[s] = wt_ref[...].T.astype(BF16)

    @pl.when((s >= n_direct) & (s < nw))
    def _():
        rows = jnp.concatenate([wt_ref[GATE_RANK:rb, :], wnext_ref[...]], axis=0)
        wb_ref[s] = rows.T.astype(BF16)

    @pl.when(s == n_direct)
    def _():
        wlr_ref[...] = wt_ref[0:LANES, :].T.astype(BF16)

    @pl.when(s >= nw)
    def _project():
        j = lax.rem(s - nw, nj)

        def pair(t0):
            h = h_ref[...]
            return jnp.concatenate([_dot(h, wb_ref[t0]), _dot(h, wb_ref[t0 + 1])], axis=1)

        def put(z):
            zp_ref[...] = z[:tmp]
            zs_ref[...] = z[tmp:]

        nqkv, nsg, nglu = Z_SG // ZB, (Z_GLU - Z_SG) // ZB, (Z_LA - Z_GLU) // ZB
        bpz = ZB // rb

        @pl.when(j == 0)
        def _():
            _norm_to_scratch(h_ref, xp_ref, xs_ref, g_ref[...], tmp, tms)
            put(pair(0))

        @pl.when((j > 0) & (j < nqkv))
        def _():
            put(pair(bpz * j))

        @pl.when((j >= nqkv) & (j < nqkv + nsg))
        def _():
            z = pair(bpz * j)
            put(z * jax.nn.sigmoid(z))

        @pl.when((j >= nqkv + nsg) & (j < nqkv + nsg + nglu))
        def _():
            ta = bpz * j
            ua = pair(ta)
            ug = pair(ta + bpz * nglu)
            put(ua * jax.nn.sigmoid(ug))

        @pl.when(j == nqkv + nsg + nglu)
        def _():
            glr = _dot(h_ref[...], wlr_ref[...])[:, 0:GATE_RANK]
            logit = _dot(glr, wup_ref[...]) + bg_ref[...]
            put(_log_sigmoid(logit) * (1.0 / GATE_NORM))


def _conv_prompt_kernel(glu_ref, cw_ref, cb_ref, lg_ref, lb_ref, c_ref, cnew_ref, slab_ref, y_ref,
                        *, tt, nsteps, hist):
    t_idx = pl.program_id(1)
    nslab = slab_ref.shape[0]
    pad = hist - (CONV_W - 1)

    @pl.when(t_idx == 0)
    def _():
        slab_ref[:, 0:hist, :] = jnp.zeros((nslab, hist, LANES), F32)

    @pl.when(t_idx > 0)
    def _():
        slab_ref[:, 0:hist, :] = slab_ref[:, tt:tt + hist, :]

    for sl in range(nslab):
        slab_ref[sl, hist:hist + tt, :] = glu_ref[:, sl * LANES:(sl + 1) * LANES]

    def slab_conv(sl, carry):
        acc = None
        for res in range(8):
            part = None
            for w in range(res, CONV_W, 8):
                term = slab_ref[sl, pad + w:pad + w + tt, :] * cw_ref[w, pl.ds(sl, 1), :]
                part = term if part is None else part + term
            acc = part if acc is None else acc + part
        y_ref[sl] = acc + cb_ref[pl.ds(sl, 1), :]
        return carry

    lax.fori_loop(0, nslab, slab_conv, 0)

    y = y_ref[...]
    inv_c = 1.0 / (nslab * LANES)
    mu = jnp.sum(jnp.sum(y, axis=0), axis=-1, keepdims=True) * inv_c
    yc = y - mu[None]
    var = jnp.sum(jnp.sum(yc * yc, axis=0), axis=-1, keepdims=True) * inv_c
    rstd = lax.rsqrt(var + LN_EPS)
    for sl in range(nslab):
        cs = slice(sl * LANES, (sl + 1) * LANES)
        z = yc[sl] * rstd * lg_ref[:, cs] + lb_ref[:, cs]
        c_ref[:, cs] = (z * jax.nn.sigmoid(z)).astype(c_ref.dtype)

    @pl.when(t_idx == nsteps - 1)
    def _():
        for sl in range(nslab):
            cnew_ref[0, 0, :, sl * LANES:(sl + 1) * LANES] = slab_ref[sl, hist + tt - (CONV_W - 1):hist + tt, :]


def _outproj_kernel(w_ref, xp_ref, xs_ref, oap_ref, oas_ref, cp_ref, cs_ref, op_ref, os_ref,
                    wb_ref, a_ref, c_ref, *, nw, nj, tmp, tms):
    s = pl.program_id(0)
    rb = MXU_N
    wpl = w_ref.shape[1] // rb
    bpc = xp_ref.shape[1] // rb

    @pl.when(s < nw)
    def _load_weight():
        for q in range(wpl):
            wb_ref[s * wpl + q] = w_ref[:, q * rb:(q + 1) * rb].astype(BF16)

    @pl.when(s >= nw)
    def _project():
        j = lax.rem(s - nw, nj)

        @pl.when(j == 0)
        def _():
            a_ref[0:tmp, :] = oap_ref[...]
            a_ref[tmp:tmp + tms, :] = oas_ref[...].astype(BF16)
            c_ref[0:tmp, :] = cp_ref[...]
            c_ref[tmp:tmp + tms, :] = cs_ref[...].astype(BF16)

        for half in range(bpc):
            t = bpc * j + half
            cols = slice(half * rb, (half + 1) * rb)
            z = _dot(a_ref[...], wb_ref[t, 0:D_A, :]) + _dot(c_ref[...], wb_ref[t, D_A:, :])
            op_ref[:, cols] = xp_ref[:, cols] + z[:tmp]
            os_ref[:, cols] = xs_ref[:, cols] + z[tmp:]


def _ffn_kernel(xp_ref, xs_ref, g_ref, gf_ref, wg_ref, wu_ref, wo_ref, op_ref, os_ref, h_ref,
                *, tmp, tms, nf):
    j = pl.program_id(1)

    def ffn_block(acc_p, acc_s):
        h = h_ref[...]
        fg = _dot(h, wg_ref[...].astype(BF16))
        fu = _dot(h, wu_ref[...].astype(BF16))
        act = (fg * jax.nn.sigmoid(fg) * fu).astype(BF16)
        wo = wo_ref[...].astype(BF16)
        half = tmp // 2
        op_ref[0:half, :] = acc_p[0:half, :] + _dot(act[0:half], wo)
        z = _dot(act[half:], wo)
        op_ref[half:tmp, :] = acc_p[half:tmp, :] + z[:tmp - half]
        os_ref[...] = acc_s[...] + z[tmp - half:]

    @pl.when(j == 0)
    def _():
        _norm_to_scratch(h_ref, xp_ref, xs_ref, g_ref[...], tmp, tms)
        ffn_block(xp_ref, xs_ref)

    @pl.when((j > 0) & (j < nf - 1))
    def _():
        ffn_block(op_ref, os_ref)

    @pl.when(j == nf - 1)
    def _():
        ffn_block(op_ref, os_ref)
        gf = gf_ref[...]
        for r0 in range(0, tmp, 256):
            op_ref[r0:r0 + 256, :] = _rms_rows(op_ref[r0:r0 + 256, :], gf)
        os_ref[...] = _rms_rows(os_ref[...], gf)


@functools.lru_cache(maxsize=None)
def _gla_constants(L):
    nlev = int(np.log2(L)) + 1
    m = np.zeros((nlev + 2, L, L), np.float32)
    wm = np.zeros((nlev, L, L), np.float32)
    idx = np.arange(L)
    wm[0] = np.eye(L, dtype=np.float32)
    for lev in range(1, nlev):
        half = L >> lev
        pos = idx % (2 * half)
        upper = pos >= half
        r = idx - pos + half - 1
        t = idx[None, :]
        up_rows = (t > r[:, None]) & (t <= idx[:, None]) & upper[:, None]
        lo_rows = (t > idx[:, None]) & (t <= r[:, None]) & (~upper)[:, None]
        m[lev] = (up_rows | lo_rows).astype(np.float32)
        same = (idx[:, None] // (2 * half)) == (idx[None, :] // (2 * half))
        wm[lev] = (same & upper[:, None] & (~upper)[None, :]).astype(np.float32)
    m[nlev] = (idx[None, :] <= idx[:, None]).astype(np.float32)
    m[nlev + 1] = (idx[None, :] > idx[:, None]).astype(np.float32)
    m = m.reshape((nlev + 2) * L, L)
    return nlev, np.concatenate([m, m], axis=1), wm


def _gla_prompt_kernel(qkv_ref, la_ref, sg_ref, mall_ref, wm_ref, gn_ref,
                       o_ref, snew_ref, s_ref, *, nlev, nsteps, chunks_per_step):
    c = pl.program_id(1)

    @pl.when(c == 0)
    def _():
        s_ref[...] = jnp.zeros_like(s_ref)

    L = GLA_CHUNK
    scale = HEAD_K ** -0.5
    gn = gn_ref[...]
    chunks = [slice(sub * L, (sub + 1) * L) for sub in range(chunks_per_step)]
    pairs = [(rows, h) for rows in chunks for h in range(N_HEADS)]

    def q_of(rows, h):
        return qkv_ref[rows, h * HEAD_K:(h + 1) * HEAD_K] * scale

    def k_of(rows, h):
        return qkv_ref[rows, D_K + h * HEAD_K:D_K + (h + 1) * HEAD_K]

    mall2 = mall_ref[...]
    e_all = {}
    for rows in chunks:
        la = la_ref[rows, :]
        la_hi = la.astype(BF16)
        la_lo = (la - la_hi.astype(F32)).astype(BF16)
        e_all[rows.start] = _dot(mall2, jnp.concatenate([la_hi, la_lo], axis=0))

    def scaled(rows, h, lev):
        if lev == 0:
            return q_of(rows, h).astype(BF16), k_of(rows, h).astype(BF16)
        e = jnp.exp(e_all[rows.start][lev * L:(lev + 1) * L, h * HEAD_K:(h + 1) * HEAD_K])
        return (q_of(rows, h) * e).astype(BF16), (k_of(rows, h) * e).astype(BF16)

    assert len(pairs) % 2 == 0
    zero = jnp.zeros((L, HEAD_K), BF16)
    amat = {}
    for lev in range(nlev):
        for pa, pb in zip(pairs[0::2], pairs[1::2]):
            qa, ka = scaled(*pa, lev)
            qb, kb = scaled(*pb, lev)
            lhs = jnp.concatenate([qa, qb], axis=1)
            rhs_t = jnp.concatenate([jnp.concatenate([ka, zero], axis=1),
                                     jnp.concatenate([zero, kb], axis=1)], axis=0)
            p = _dot_nt(lhs, rhs_t)
            for key, blk in (((pa[0].start, pa[1]), p[:, 0:L]), ((pb[0].start, pb[1]), p[:, L:2 * L])):
                term = wm_ref[lev] * blk
                amat[key] = term if lev == 0 else amat[key] + term

    for rows, h in pairs:
        ks = slice(h * HEAD_K, (h + 1) * HEAD_K)
        vs = slice(h * HEAD_V, (h + 1) * HEAD_V)
        q, k, a = q_of(rows, h), k_of(rows, h), amat[rows.start, h]
        vb = qkv_ref[rows, 2 * D_K + h * HEAD_V:2 * D_K + (h + 1) * HEAD_V].astype(BF16)

        eb = e_all[rows.start][nlev * L:(nlev + 1) * L, ks]
        ek = e_all[rows.start][(nlev + 1) * L:(nlev + 2) * L, ks]
        qh = (q * jnp.exp(eb)).astype(BF16)
        kh = k * jnp.exp(ek)
        s = s_ref[h]
        o = _dot(jnp.concatenate([a.astype(BF16), qh], axis=1),
                 jnp.concatenate([vb, s.astype(BF16)], axis=0))

        blast = jnp.broadcast_to(eb[L - 1:L, :], (HEAD_K, HEAD_K))
        acol = jnp.exp(blast.T)
        acol = jnp.concatenate([acol, acol], axis=1)
        s_ref[h] = acol * s + _dot(kh.T.astype(BF16), vb)

        y = _rms_rows(o, gn) * sg_ref[rows, vs]
        o_ref[rows, vs] = y.astype(o_ref.dtype)

    @pl.when(c == nsteps - 1)
    def _():
        snew_ref[0, 0] = s_ref[...]


def _gla_sample_kernel(la_ref, qkv_ref, sg_ref, st_ref, gn_ref, o_ref, snew_ref, at_ref, *, bblk):
    s_idx = pl.program_id(0)

    @pl.when(s_idx == 0)
    def _():
        for h in range(N_HEADS):
            at_ref[h] = jnp.exp(la_ref[:, h * HEAD_K:(h + 1) * HEAD_K]).T

    gn = gn_ref[...]
    scale = HEAD_K ** -0.5
    lane = lax.broadcasted_iota(jnp.int32, (HEAD_K, LANES), 1)
    row = lax.broadcasted_iota(jnp.int32, (bblk, HEAD_V), 0)
    for h in range(N_HEADS):
        vs = slice(h * HEAD_V, (h + 1) * HEAD_V)
        q = (qkv_ref[:, h * HEAD_K:(h + 1) * HEAD_K] * scale).astype(BF16)
        kt = qkv_ref[:, D_K + h * HEAD_K:D_K + (h + 1) * HEAD_K].T.astype(BF16)
        v = qkv_ref[:, 2 * D_K + h * HEAD_V:2 * D_K + (h + 1) * HEAD_V].astype(BF16)
        vdiag = jnp.concatenate([jnp.where(row == j, v, jnp.zeros_like(v)) for j in range(bblk)], axis=1)
        kv = _dot(kt, vdiag)
        outs = []
        for j in range(bblk):
            b = s_idx * bblk + j
            acol = jnp.sum(jnp.where(lane == b, at_ref[h], 0.0), axis=1, keepdims=True)
            s_new = acol * st_ref[j, h] + kv[:, j * HEAD_V:(j + 1) * HEAD_V]
            snew_ref[0, j, h] = s_new
            outs.append(_dot(q, s_new.astype(BF16))[j:j + 1, :])
        for j in range(bblk):
            o_ref[j:j + 1, vs] = _rms_rows(outs[j], gn) * sg_ref[j:j + 1, vs]


def _ln_silu(y, g, b):
    mu = jnp.mean(y, axis=-1, keepdims=True)
    yc = y - mu
    var = jnp.mean(yc * yc, axis=-1, keepdims=True)
    z = yc * lax.rsqrt(var + LN_EPS) * g + b
    return z * jax.nn.sigmoid(z)


def _conv_sample_kernel(sc_ref, glu_ref, cw_ref, cb_ref, lg_ref, lb_ref, c_ref, cnew_ref):
    nh = CONV_W - 1
    g = glu_ref[...]
    y = g * cw_ref[nh:nh + 1, :] + cb_ref[...]
    for w in range(nh):
        y = y + sc_ref[w] * cw_ref[w:w + 1, :]
    c_ref[...] = _ln_silu(y, lg_ref[...], lb_ref[...])
    for w in range(nh - 1):
        cnew_ref[w] = sc_ref[w + 1]
    cnew_ref[nh - 1] = g


def _fixed(shape, idx=None):
    idx = (0,) * len(shape) if idx is None else idx
    return pl.BlockSpec(shape, lambda *_: idx)


def kernel(x_prompt, x_sample, state_gla, state_conv, norm_mix, w_in, w_gate_up, b_gate, gla_norm,
           conv_w, conv_b, conv_ln_g, conv_ln_b, w_out, norm_ffn, w_ffn_in, w_ffn_out, norm_final):
    B, T, D = x_prompt.shape
    BS = x_sample.shape[0]
    assert state_gla.shape[0] == 1 and x_sample.shape[1] == 1
    d_conv = state_conv.shape[-1]
    d_ff = w_ffn_out.shape[1]
    in_cols = w_in.shape[-1]
    assert d_conv == D_A and in_cols == 2 * D_K + D_A + GATE_RANK + D_A + 2 * d_conv
    MP = B * T
    n_tiles = 8
    tmp, tms = MP // n_tiles, BS // n_tiles
    assert tmp * n_tiles == MP and tms * n_tiles == BS and tms % 8 == 0 and tmp % 256 == 0
    tm = tmp + tms

    xp = x_prompt.reshape(MP, D)
    xs = x_sample.reshape(BS, D)

    def pair_shapes(cols):
        return (jax.ShapeDtypeStruct((MP, cols), F32), jax.ShapeDtypeStruct((BS, cols), F32))

    wt = jnp.swapaxes(w_in, 1, 2)[0]
    nw = (in_cols - GATE_RANK) // MXU_N
    assert nw * MXU_N == in_cols - GATE_RANK and MXU_N % GATE_RANK == 0
    nj = Z_COLS // ZB
    row_i = lambda s: jnp.maximum(s - nw, 0) // nj
    col_j = lambda s: lax.rem(jnp.maximum(s - nw, 0), nj)
    wblk = lambda s: jnp.minimum(s, nw - 1)
    z_p, z_s = pl.pallas_call(
        functools.partial(_in_proj_kernel, nw=nw, nj=nj, tmp=tmp, tms=tms),
        grid=(nw + n_tiles * nj,),
        in_specs=[pl.BlockSpec((MXU_N, D), lambda s: (wblk(s), 0)),
                  pl.BlockSpec((GATE_RANK, D), lambda s: ((wblk(s) + 1) * (MXU_N // GATE_RANK), 0)),
                  pl.BlockSpec((tmp, D), lambda s: (row_i(s), 0)),
                  pl.BlockSpec((tms, D), lambda s: (row_i(s), 0)),
                  _fixed((1, D)), _fixed((GATE_RANK, D_K)), _fixed((1, D_K))],
        out_specs=(pl.BlockSpec((tmp, ZB), lambda s: (row_i(s), col_j(s))),
                   pl.BlockSpec((tms, ZB), lambda s: (row_i(s), col_j(s)))),
        out_shape=pair_shapes(Z_COLS),
        scratch_shapes=[pltpu.VMEM((nw, D, MXU_N), BF16), pltpu.VMEM((D, LANES), BF16),
                        pltpu.VMEM((tm, D), BF16)],
        compiler_params=_cparams(("arbitrary",)),
        name="in_proj",
    )(wt, wt, xp, xs, norm_mix[0].reshape(1, D), w_gate_up[0], b_gate[0].reshape(1, D_K))

    chunks_per_step = 4
    L = GLA_CHUNK * chunks_per_step
    nchunks = T // L
    assert nchunks * L == T
    nlev, mall, wm = _gla_constants(GLA_CHUNK)
    gn = gla_norm[0].reshape(1, HEAD_V)
    zrow = lambda b, c: b * nchunks + c
    oa_p, gla_new_p = pl.pallas_call(
        functools.partial(_gla_prompt_kernel, nlev=nlev, nsteps=nchunks, chunks_per_step=chunks_per_step),
        grid=(B, nchunks),
        in_specs=[pl.BlockSpec((L, Z_SG), lambda b, c: (zrow(b, c), 0)),
                  pl.BlockSpec((L, D_K), lambda b, c: (zrow(b, c), Z_LA // D_K)),
                  pl.BlockSpec((L, D_A), lambda b, c: (zrow(b, c), Z_SG // D_A)),
                  _fixed(mall.shape), _fixed(wm.shape),
                  _fixed((1, HEAD_V))],
        out_specs=(pl.BlockSpec((L, D_A), lambda b, c: (zrow(b, c), 0)),
                   pl.BlockSpec((1, 1, N_HEADS, HEAD_K, HEAD_V), lambda b, c: (0, b, 0, 0, 0))),
        out_shape=(jax.ShapeDtypeStruct((MP, D_A), BF16),
                   jax.ShapeDtypeStruct((1, B, N_HEADS, HEAD_K, HEAD_V), F32)),
        scratch_shapes=[pltpu.VMEM((N_HEADS, HEAD_K, HEAD_V), F32)],
        compiler_params=_cparams(("arbitrary", "arbitrary")),
        name="gla_prompt",
    )(z_p, z_p, z_p, jnp.asarray(mall, BF16), jnp.asarray(wm), gn)

    bblk = 8
    assert BS == LANES and BS % bblk == 0
    oa_s, gla_new_s = pl.pallas_call(
        functools.partial(_gla_sample_kernel, bblk=bblk),
        grid=(BS // bblk,),
        in_specs=[_fixed((BS, D_K), (0, Z_LA // D_K)),
                  pl.BlockSpec((bblk, Z_SG), lambda s: (s, 0)),
                  pl.BlockSpec((bblk, D_A), lambda s: (s, Z_SG // D_A)),
                  pl.BlockSpec((bblk, N_HEADS, HEAD_K, HEAD_V), lambda s: (s, 0, 0, 0)),
                  _fixed((1, HEAD_V))],
        out_specs=(pl.BlockSpec((bblk, D_A), lambda s: (s, 0)),
                   pl.BlockSpec((1, bblk, N_HEADS, HEAD_K, HEAD_V), lambda s: (0, s, 0, 0, 0))),
        out_shape=(jax.ShapeDtypeStruct((BS, D_A), F32),
                   jax.ShapeDtypeStruct((1, BS, N_HEADS, HEAD_K, HEAD_V), F32)),
        scratch_shapes=[pltpu.VMEM((N_HEADS, HEAD_K, BS), F32)],
        compiler_params=_cparams(("arbitrary",)),
        name="gla_sample",
    )(z_s, z_s, z_s, state_gla[0], gn)

    nslab = d_conv // LANES
    vec = lambda a: a.reshape(1, d_conv)

    cb_blk = 32
    sc_t = jnp.transpose(state_conv[0], (1, 0, 2))
    c_s, conv_new_s_t = pl.pallas_call(
        _conv_sample_kernel,
        grid=(BS // cb_blk,),
        in_specs=[pl.BlockSpec((CONV_W - 1, cb_blk, d_conv), lambda s: (0, s, 0)),
                  pl.BlockSpec((cb_blk, d_conv), lambda s: (s, Z_GLU // d_conv)),
                  _fixed((CONV_W, d_conv)), _fixed((1, d_conv)), _fixed((1, d_conv)), _fixed((1, d_conv))],
        out_specs=(pl.BlockSpec((cb_blk, d_conv), lambda s: (s, 0)),
                   pl.BlockSpec((CONV_W - 1, cb_blk, d_conv), lambda s: (0, s, 0))),
        out_shape=(jax.ShapeDtypeStruct((BS, d_conv), F32),
                   jax.ShapeDtypeStruct((CONV_W - 1, BS, d_conv), F32)),
        compiler_params=_cparams(("arbitrary",)),
        name="conv_sample",
    )(sc_t, z_s, conv_w[0], vec(conv_b[0]), vec(conv_ln_g[0]), vec(conv_ln_b[0]))
    conv_new_s = jnp.transpose(conv_new_s_t, (1, 0, 2))[None]

    tt = 512
    hist = 32
    nsteps = T // tt
    c_p, conv_new_p = pl.pallas_call(
        functools.partial(_conv_prompt_kernel, tt=tt, nsteps=nsteps, hist=hist),
        grid=(B, nsteps),
        in_specs=[pl.BlockSpec((tt, d_conv), lambda b, t: (b * nsteps + t, Z_GLU // d_conv)),
                  _fixed((CONV_W, nslab, LANES)), _fixed((nslab, LANES)),
                  _fixed((1, d_conv)), _fixed((1, d_conv))],
        out_specs=(pl.BlockSpec((tt, d_conv), lambda b, t: (b * nsteps + t, 0)),
                   pl.BlockSpec((1, 1, CONV_W - 1, d_conv), lambda b, t: (0, b, 0, 0))),
        out_shape=(jax.ShapeDtypeStruct((MP, d_conv), BF16),
                   jax.ShapeDtypeStruct((1, B, CONV_W - 1, d_conv), F32)),
        scratch_shapes=[pltpu.VMEM((nslab, hist + tt, LANES), F32), pltpu.VMEM((nslab, tt, LANES), F32)],
        compiler_params=_cparams(("arbitrary", "arbitrary")),
        name="conv_prompt",
    )(z_p, conv_w[0].reshape(CONV_W, nslab, LANES), conv_b[0].reshape(nslab, LANES),
      vec(conv_ln_g[0]), vec(conv_ln_b[0]))

    zbo = min(1024, D)
    wcols = min(2 * MXU_N, D)
    nwo = D // wcols
    njo = D // zbo
    assert nwo * wcols == D and njo * zbo == D
    row_o = lambda s: jnp.maximum(s - nwo, 0) // njo
    col_o = lambda s: lax.rem(jnp.maximum(s - nwo, 0), njo)
    x1_p, x1_s = pl.pallas_call(
        functools.partial(_outproj_kernel, nw=nwo, nj=njo, tmp=tmp, tms=tms),
        grid=(nwo + n_tiles * njo,),
        in_specs=[pl.BlockSpec((D_A + d_conv, wcols), lambda s: (0, jnp.minimum(s, nwo - 1))),
                  pl.BlockSpec((tmp, zbo), lambda s: (row_o(s), col_o(s))),
                  pl.BlockSpec((tms, zbo), lambda s: (row_o(s), col_o(s))),
                  pl.BlockSpec((tmp, D_A), lambda s: (row_o(s), 0)),
                  pl.BlockSpec((tms, D_A), lambda s: (row_o(s), 0)),
                  pl.BlockSpec((tmp, d_conv), lambda s: (row_o(s), 0)),
                  pl.BlockSpec((tms, d_conv), lambda s: (row_o(s), 0))],
        out_specs=(pl.BlockSpec((tmp, zbo), lambda s: (row_o(s), col_o(s))),
                   pl.BlockSpec((tms, zbo), lambda s: (row_o(s), col_o(s)))),
        out_shape=pair_shapes(D),
        scratch_shapes=[pltpu.VMEM((D // MXU_N, D_A + d_conv, MXU_N), BF16),
                        pltpu.VMEM((tm, D_A), BF16), pltpu.VMEM((tm, d_conv), BF16)],
        compiler_params=_cparams(("arbitrary",)),
        name="out_proj",
    )(w_out[0], xp, xs, oa_p, oa_s, c_p, c_s)

    tf = 256
    nf = d_ff // tf
    assert nf * tf == d_ff
    y_p, y_s = pl.pallas_call(
        functools.partial(_ffn_kernel, tmp=tmp, tms=tms, nf=nf),
        grid=(n_tiles, nf),
        in_specs=[pl.BlockSpec((tmp, D), lambda i, j: (i, 0)), pl.BlockSpec((tms, D), lambda i, j: (i, 0)),
                  _fixed((1, D)), _fixed((1, D)),
                  pl.BlockSpec((D, tf), lambda i, j: (0, j)),
                  pl.BlockSpec((D, tf), lambda i, j: (0, j + nf)),
                  pl.BlockSpec((tf, D), lambda i, j: (j, 0))],
        out_specs=(pl.BlockSpec((tmp, D), lambda i, j: (i, 0)), pl.BlockSpec((tms, D), lambda i, j: (i, 0))),
        out_shape=pair_shapes(D),
        scratch_shapes=[pltpu.VMEM((tm, D), BF16)],
        compiler_params=_cparams(("arbitrary", "arbitrary")),
        name="ffn",
    )(x1_p, x1_s, norm_ffn[0].reshape(1, D), norm_final.reshape(1, D), w_ffn_in[0], w_ffn_in[0], w_ffn_out[0])

    return (y_p.reshape(B, T, D), y_s.reshape(BS, 1, D), gla_new_p, conv_new_p, gla_new_s, conv_new_s)
```

```python
import functools

import numpy as np
import jax
import jax.numpy as jnp
from jax import lax
from jax.experimental import pallas as pl
from jax.experimental.pallas import tpu as pltpu

F32 = jnp.float32
BF16 = jnp.bfloat16

RMS_EPS = 1e-6
LN_EPS = 1e-5
GATE_NORM = 16.0
N_HEADS = 4
HEAD_K = 128
HEAD_V = 256
D_K = N_HEADS * HEAD_K
D_A = N_HEADS * HEAD_V
GATE_RANK = 16
CONV_W = 31

LANES = 128
MXU_N = 256
GLA_CHUNK = 128
VMEM_LIMIT = 58 * 1024 * 1024

ZB = 512
Z_QKV, Z_SG, Z_GLU, Z_LA = 0, 2 * D_K + D_A, 2 * D_K + 2 * D_A, 2 * D_K + 3 * D_A
Z_COLS = Z_LA + D_K


def _cparams(sem):
    return pltpu.CompilerParams(dimension_semantics=sem, vmem_limit_bytes=VMEM_LIMIT)


def _dot(a, b):
    return jnp.dot(a, b, preferred_element_type=F32)


def _dot_nt(a, b):
    return lax.dot_general(a, b, (((1,), (1,)), ((), ())), preferred_element_type=F32)


def _rms_rows(x, gain):
    ms = jnp.mean(x * x, axis=-1, keepdims=True)
    return x * lax.rsqrt(ms + RMS_EPS) * gain


def _log_sigmoid(x):
    return jnp.minimum(x, 0.0) - jnp.log(1.0 + jnp.exp(-jnp.abs(x)))


def _norm_to_scratch(h_ref, xp_ref, xs_ref, gain, tmp, tms, rows_per_pass=256):
    for r0 in range(0, tmp, rows_per_pass):
        h_ref[r0:r0 + rows_per_pass, :] = _rms_rows(xp_ref[r0:r0 + rows_per_pass, :], gain).astype(BF16)
    h_ref[tmp:tmp + tms, :] = _rms_rows(xs_ref[...], gain).astype(BF16)


def _in_proj_kernel(wt_ref, wnext_ref, xp_ref, xs_ref, g_ref, wup_ref, bg_ref, zp_ref, zs_ref,
                    wb_ref, wlr_ref, h_ref, *, nw, nj, tmp, tms):
    s = pl.program_id(0)
    rb = MXU_N
    n_direct = (2 * D_K + D_A) // rb

    @pl.when(s < n_direct)
    def _():
        wb_ref[s] = wt_ref[...].T.astype(BF16)

    @pl.when((s >= n_direct) & (s < nw))
    def _():
        rows = jnp.concatenate([wt_ref[GATE_RANK:rb, :], wnext_ref[...]], axis=0)
        wb_ref[s] = rows.T.astype(BF16)

    @pl.when(s == n_direct)
    def _():
        wlr_ref[...] = wt_ref[0:LANES, :].T.astype(BF16)

    @pl.when(s >= nw)
    def _project():
        j = lax.rem(s - nw, nj)

        def pair(t0):
            h = h_ref[...]
            return jnp.concatenate([_dot(h, wb_ref[t0]), _dot(h, wb_ref[t0 + 1])], axis=1)

        def put(z):
            zp_ref[...] = z[:tmp]
            zs_ref[...] = z[tmp:]

        nqkv, nsg, nglu = Z_SG // ZB, (Z_GLU - Z_SG) // ZB, (Z_LA - Z_GLU) // ZB
        bpz = ZB // rb

        @pl.when(j == 0)
        def _():
            _norm_to_scratch(h_ref, xp_ref, xs_ref, g_ref[...], tmp, tms)
            put(pair(0))

        @pl.when((j > 0) & (j < nqkv))
        def _():
            put(pair(bpz * j))

        @pl.when((j >= nqkv) & (j < nqkv + nsg))
        def _():
            z = pair(bpz * j)
            put(z * jax.nn.sigmoid(z))

        @pl.when((j >= nqkv + nsg) & (j < nqkv + nsg + nglu))
        def _():
            ta = bpz * j
            ua = pair(ta)
            ug = pair(ta + bpz * nglu)
            put(ua * jax.nn.sigmoid(ug))

        @pl.when(j == nqkv + nsg + nglu)
        def _():
            glr = _dot(h_ref[...], wlr_ref[...])[:, 0:GATE_RANK]
            logit = _dot(glr, wup_ref[...]) + bg_ref[...]
            put(_log_sigmoid(logit) * (1.0 / GATE_NORM))


def _conv_prompt_kernel(glu_ref, cw_ref, cb_ref, lg_ref, lb_ref, c_ref, cnew_ref, slab_ref, y_ref,
                        *, tt, nsteps, hist):
    t_idx = pl.program_id(1)
    nslab = slab_ref.shape[0]
    pad = hist - (CONV_W - 1)

    @pl.when(t_idx == 0)
    def _():
        slab_ref[:, 0:hist, :] = jnp.zeros((nslab, hist, LANES), F32)

    @pl.when(t_idx > 0)
    def _():
        slab_ref[:, 0:hist, :] = slab_ref[:, tt:tt + hist, :]

    for sl in range(nslab):
        slab_ref[sl, hist:hist + tt, :] = glu_ref[:, sl * LANES:(sl + 1) * LANES]

    def slab_conv(sl, carry):
        acc = None
        for res in range(8):
            part = None
            for w in range(res, CONV_W, 8):
                term = slab_ref[sl, pad + w:pad + w + tt, :] * cw_ref[w, pl.ds(sl, 1), :]
                part = term if part is None else part + term
            acc = part if acc is None else acc + part
        y_ref[sl] = acc + cb_ref[pl.ds(sl, 1), :]
        return carry

    lax.fori_loop(0, nslab, slab_conv, 0)

    y = y_ref[...]
    inv_c = 1.0 / (nslab * LANES)
    mu = jnp.sum(jnp.sum(y, axis=0), axis=-1, keepdims=True) * inv_c
    yc = y - mu[None]
    var = jnp.sum(jnp.sum(yc * yc, axis=0), axis=-1, keepdims=True) * inv_c
    rstd = lax.rsqrt(var + LN_EPS)
    for sl in range(nslab):
        cs = slice(sl * LANES, (sl + 1) * LANES)
        z = yc[sl] * rstd * lg_ref[:, cs] + lb_ref[:, cs]
        c_ref[:, cs] = (z * jax.nn.sigmoid(z)).astype(c_ref.dtype)

    @pl.when(t_idx == nsteps - 1)
    def _():
        for sl in range(nslab):
            cnew_ref[0, 0, :, sl * LANES:(sl + 1) * LANES] = slab_ref[sl, hist + tt - (CONV_W - 1):hist + tt, :]


def _outproj_kernel(w_ref, xp_ref, xs_ref, oap_ref, oas_ref, cp_ref, cs_ref, op_ref, os_ref,
                    wb_ref, a_ref, c_ref, *, nw, nj, tmp, tms):
    s = pl.program_id(0)
    rb = MXU_N
    wpl = w_ref.shape[1] // rb
    bpc = xp_ref.shape[1] // rb

    @pl.when(s < nw)
    def _load_weight():
        for q in range(wpl):
            wb_ref[s * wpl + q] = w_ref[:, q * rb:(q + 1) * rb].astype(BF16)

    @pl.when(s >= nw)
    def _project():
        j = lax.rem(s - nw, nj)

        @pl.when(j == 0)
        def _():
            a_ref[0:tmp, :] = oap_ref[...]
            a_ref[tmp:tmp + tms, :] = oas_ref[...].astype(BF16)
            c_ref[0:tmp, :] = cp_ref[...]
            c_ref[tmp:tmp + tms, :] = cs_ref[...].astype(BF16)

        for half in range(bpc):
            t = bpc * j + half
            cols = slice(half * rb, (half + 1) * rb)
            z = _dot(a_ref[...], wb_ref[t, 0:D_A, :]) + _dot(c_ref[...], wb_ref[t, D_A:, :])
            op_ref[:, cols] = xp_ref[:, cols] + z[:tmp]
            os_ref[:, cols] = xs_ref[:, cols] + z[tmp:]


def _ffn_kernel(xp_ref, xs_ref, g_ref, gf_ref, wg_ref, wu_ref, wo_ref, op_ref, os_ref, h_ref,
                *, tmp, tms, nf):
    j = pl.program_id(1)

    def ffn_block(acc_p, acc_s):
        h = h_ref[...]
        fg = _dot(h, wg_ref[...].astype(BF16))
        fu = _dot(h, wu_ref[...].astype(BF16))
        act = (fg * jax.nn.sigmoid(fg) * fu).astype(BF16)
        wo = wo_ref[...].astype(BF16)
        half = tmp // 2
        op_ref[0:half, :] = acc_p[0:half, :] + _dot(act[0:half], wo)
        z = _dot(act[half:], wo)
        op_ref[half:tmp, :] = acc_p[half:tmp, :] + z[:tmp - half]
        os_ref[...] = acc_s[...] + z[tmp - half:]

    @pl.when(j == 0)
    def _():
        _norm_to_scratch(h_ref, xp_ref, xs_ref, g_ref[...], tmp, tms)
        ffn_block(xp_ref, xs_ref)

    @pl.when((j > 0) & (j < nf - 1))
    def _():
        ffn_block(op_ref, os_ref)

    @pl.when(j == nf - 1)
    def _():
        ffn_block(op_ref, os_ref)
        gf = gf_ref[...]
        for r0 in range(0, tmp, 256):
            op_ref[r0:r0 + 256, :] = _rms_rows(op_ref[r0:r0 + 256, :], gf)
        os_ref[...] = _rms_rows(os_ref[...], gf)


@functools.lru_cache(maxsize=None)
def _gla_constants(L):
    nlev = int(np.log2(L)) + 1
    m = np.zeros((nlev + 2, L, L), np.float32)
    wm = np.zeros((nlev, L, L), np.float32)
    idx = np.arange(L)
    wm[0] = np.eye(L, dtype=np.float32)
    for lev in range(1, nlev):
        half = L >> lev
        pos = idx % (2 * half)
        upper = pos >= half
        r = idx - pos + half - 1
        t = idx[None, :]
        up_rows = (t > r[:, None]) & (t <= idx[:, None]) & upper[:, None]
        lo_rows = (t > idx[:, None]) & (t <= r[:, None]) & (~upper)[:, None]
        m[lev] = (up_rows | lo_rows).astype(np.float32)
        same = (idx[:, None] // (2 * half)) == (idx[None, :] // (2 * half))
        wm[lev] = (same & upper[:, None] & (~upper)[None, :]).astype(np.float32)
    m[nlev] = (idx[None, :] <= idx[:, None]).astype(np.float32)
    m[nlev + 1] = (idx[None, :] > idx[:, None]).astype(np.float32)
    m = m.reshape((nlev + 2) * L, L)
    return nlev, np.concatenate([m, m], axis=1), wm


def _gla_prompt_kernel(qkv_ref, la_ref, sg_ref, mall_ref, wm_ref, gn_ref,
                       o_ref, snew_ref, s_ref, *, nlev, nsteps, chunks_per_step):
    c = pl.program_id(1)

    @pl.when(c == 0)
    def _():
        s_ref[...] = jnp.zeros_like(s_ref)

    L = GLA_CHUNK
    scale = HEAD_K ** -0.5
    gn = gn_ref[...]
    chunks = [slice(sub * L, (sub + 1) * L) for sub in range(chunks_per_step)]
    pairs = [(rows, h) for rows in chunks for h in range(N_HEADS)]

    def q_of(rows, h):
        return qkv_ref[rows, h * HEAD_K:(h + 1) * HEAD_K] * scale

    def k_of(rows, h):
        return qkv_ref[rows, D_K + h * HEAD_K:D_K + (h + 1) * HEAD_K]

    mall2 = mall_ref[...]
    e_all = {}
    for rows in chunks:
        la = la_ref[rows, :]
        la_hi = la.astype(BF16)
        la_lo = (la - la_hi.astype(F32)).astype(BF16)
        e_all[rows.start] = _dot(mall2, jnp.concatenate([la_hi, la_lo], axis=0))

    def scaled(rows, h, lev):
        if lev == 0:
            return q_of(rows, h).astype(BF16), k_of(rows, h).astype(BF16)
        e = jnp.exp(e_all[rows.start][lev * L:(lev + 1) * L, h * HEAD_K:(h + 1) * HEAD_K])
        return (q_of(rows, h) * e).astype(BF16), (k_of(rows, h) * e).astype(BF16)

    assert len(pairs) % 2 == 0
    zero = jnp.zeros((L, HEAD_K), BF16)
    amat = {}
    for lev in range(nlev):
        for pa, pb in zip(pairs[0::2], pairs[1::2]):
            qa, ka = scaled(*pa, lev)
            qb, kb = scaled(*pb, lev)
            lhs = jnp.concatenate([qa, qb], axis=1)
            rhs_t = jnp.concatenate([jnp.concatenate([ka, zero], axis=1),
                                     jnp.concatenate([zero, kb], axis=1)], axis=0)
            p = _dot_nt(lhs, rhs_t)
            for key, blk in (((pa[0].start, pa[1]), p[:, 0:L]), ((pb[0].start, pb[1]), p[:, L:2 * L])):
                term = wm_ref[lev] * blk
                amat[key] = term if lev == 0 else amat[key] + term

    for rows, h in pairs:
        ks = slice(h * HEAD_K, (h + 1) * HEAD_K)
        vs = slice(h * HEAD_V, (h + 1) * HEAD_V)
        q, k, a = q_of(rows, h), k_of(rows, h), amat[rows.start, h]
        vb = qkv_ref[rows, 2 * D_K + h * HEAD_V:2 * D_K + (h + 1) * HEAD_V].astype(BF16)

        eb = e_all[rows.start][nlev * L:(nlev + 1) * L, ks]
        ek = e_all[rows.start][(nlev + 1) * L:(nlev + 2) * L, ks]
        qh = (q * jnp.exp(eb)).astype(BF16)
        kh = k * jnp.exp(ek)
        s = s_ref[h]
        o = _dot(jnp.concatenate([a.astype(BF16), qh], axis=1),
                 jnp.concatenate([vb, s.astype(BF16)], axis=0))

        blast = jnp.broadcast_to(eb[L - 1:L, :], (HEAD_K, HEAD_K))
        acol = jnp.exp(blast.T)
        acol = jnp.concatenate([acol, acol], axis=1)
        s_ref[h] = acol * s + _dot(kh.T.astype(BF16), vb)

        y = _rms_rows(o, gn) * sg_ref[rows, vs]
        o_ref[rows, vs] = y.astype(o_ref.dtype)

    @pl.when(c == nsteps - 1)
    def _():
        snew_ref[0, 0] = s_ref[...]


def _gla_sample_kernel(la_ref, qkv_ref, sg_ref, st_ref, gn_ref, o_ref, snew_ref, at_ref, *, bblk):
    s_idx = pl.program_id(0)

    @pl.when(s_idx == 0)
    def _():
        for h in range(N_HEADS):
            at_ref[h] = jnp.exp(la_ref[:, h * HEAD_K:(h + 1) * HEAD_K]).T

    gn = gn_ref[...]
    scale = HEAD_K ** -0.5
    lane = lax.broadcasted_iota(jnp.int32, (HEAD_K, LANES), 1)
    row = lax.broadcasted_iota(jnp.int32, (bblk, HEAD_V), 0)
    for h in range(N_HEADS):
        vs = slice(h * HEAD_V, (h + 1) * HEAD_V)
        q = (qkv_ref[:, h * HEAD_K:(h + 1) * HEAD_K] * scale).astype(BF16)
        kt = qkv_ref[:, D_K + h * HEAD_K:D_K + (h + 1) * HEAD_K].T.astype(BF16)
        v = qkv_ref[:, 2 * D_K + h * HEAD_V:2 * D_K + (h + 1) * HEAD_V].astype(BF16)
        vdiag = jnp.concatenate([jnp.where(row == j, v, jnp.zeros_like(v)) for j in range(bblk)], axis=1)
        kv = _dot(kt, vdiag)
        outs = []
        for j in range(bblk):
            b = s_idx * bblk + j
            acol = jnp.sum(jnp.where(lane == b, at_ref[h], 0.0), axis=1, keepdims=True)
            s_new = acol * st_ref[j, h] + kv[:, j * HEAD_V:(j + 1) * HEAD_V]
            snew_ref[0, j, h] = s_new
            outs.append(_dot(q, s_new.astype(BF16))[j:j + 1, :])
        for j in range(bblk):
            o_ref[j:j + 1, vs] = _rms_rows(outs[j], gn) * sg_ref[j:j + 1, vs]


def _ln_silu(y, g, b):
    mu = jnp.mean(y, axis=-1, keepdims=True)
    yc = y - mu
    var = jnp.mean(yc * yc, axis=-1, keepdims=True)
    z = yc * lax.rsqrt(var + LN_EPS) * g + b
    return z * jax.nn.sigmoid(z)


def _conv_sample_kernel(sc_ref, glu_ref, cw_ref, cb_ref, lg_ref, lb_ref, c_ref, cnew_ref):
    nh = CONV_W - 1
    g = glu_ref[...]
    y = g * cw_ref[nh:nh + 1, :] + cb_ref[...]
    for w in range(nh):
        y = y + sc_ref[w] * cw_ref[w:w + 1, :]
    c_ref[...] = _ln_silu(y, lg_ref[...], lb_ref[...])
    for w in range(nh - 1):
        cnew_ref[w] = sc_ref[w + 1]
    cnew_ref[nh - 1] = g


def _fixed(shape, idx=None):
    idx = (0,) * len(shape) if idx is None else idx
    return pl.BlockSpec(shape, lambda *_: idx)


def kernel(x_prompt, x_sample, state_gla, state_conv, norm_mix, w_in, w_gate_up, b_gate, gla_norm,
           conv_w, conv_b, conv_ln_g, conv_ln_b, w_out, norm_ffn, w_ffn_in, w_ffn_out, norm_final):
    B, T, D = x_prompt.shape
    BS = x_sample.shape[0]
    assert state_gla.shape[0] == 1 and x_sample.shape[1] == 1
    d_conv = state_conv.shape[-1]
    d_ff = w_ffn_out.shape[1]
    in_cols = w_in.shape[-1]
    assert d_conv == D_A and in_cols == 2 * D_K + D_A + GATE_RANK + D_A + 2 * d_conv
    MP = B * T
    n_tiles = 8
    tmp, tms = MP // n_tiles, BS // n_tiles
    assert tmp * n_tiles == MP and tms * n_tiles == BS and tms % 8 == 0 and tmp % 256 == 0
    tm = tmp + tms

    xp = x_prompt.reshape(MP, D)
    xs = x_sample.reshape(BS, D)

    def pair_shapes(cols):
        return (jax.ShapeDtypeStruct((MP, cols), F32), jax.ShapeDtypeStruct((BS, cols), F32))

    wt = jnp.swapaxes(w_in, 1, 2)[0]
    nw = (in_cols - GATE_RANK) // MXU_N
    assert nw * MXU_N == in_cols - GATE_RANK and MXU_N % GATE_RANK == 0
    nj = Z_COLS // ZB
    row_i = lambda s: jnp.maximum(s - nw, 0) // nj
    col_j = lambda s: lax.rem(jnp.maximum(s - nw, 0), nj)
    wblk = lambda s: jnp.minimum(s, nw - 1)
    z_p, z_s = pl.pallas_call(
        functools.partial(_in_proj_kernel, nw=nw, nj=nj, tmp=tmp, tms=tms),
        grid=(nw + n_tiles * nj,),
        in_specs=[pl.BlockSpec((MXU_N, D), lambda s: (wblk(s), 0)),
                  pl.BlockSpec((GATE_RANK, D), lambda s: ((wblk(s) + 1) * (MXU_N // GATE_RANK), 0)),
                  pl.BlockSpec((tmp, D), lambda s: (row_i(s), 0)),
                  pl.BlockSpec((tms, D), lambda s: (row_i(s), 0)),
                  _fixed((1, D)), _fixed((GATE_RANK, D_K)), _fixed((1, D_K))],
        out_specs=(pl.BlockSpec((tmp, ZB), lambda s: (row_i(s), col_j(s))),
                   pl.BlockSpec((tms, ZB), lambda s: (row_i(s), col_j(s)))),
        out_shape=pair_shapes(Z_COLS),
        scratch_shapes=[pltpu.VMEM((nw, D, MXU_N), BF16), pltpu.VMEM((D, LANES), BF16),
                        pltpu.VMEM((tm, D), BF16)],
        compiler_params=_cparams(("arbitrary",)),
        name="in_proj",
    )(wt, wt, xp, xs, norm_mix[0].reshape(1, D), w_gate_up[0], b_gate[0].reshape(1, D_K))

    chunks_per_step = 4
    L = GLA_CHUNK * chunks_per_step
    nchunks = T // L
    assert nchunks * L == T
    nlev, mall, wm = _gla_constants(GLA_CHUNK)
    gn = gla_norm[0].reshape(1, HEAD_V)
    zrow = lambda b, c: b * nchunks + c
    oa_p, gla_new_p = pl.pallas_call(
        functools.partial(_gla_prompt_kernel, nlev=nlev, nsteps=nchunks, chunks_per_step=chunks_per_step),
        grid=(B, nchunks),
        in_specs=[pl.BlockSpec((L, Z_SG), lambda b, c: (zrow(b, c), 0)),
                  pl.BlockSpec((L, D_K), lambda b, c: (zrow(b, c), Z_LA // D_K)),
                  pl.BlockSpec((L, D_A), lambda b, c: (zrow(b, c), Z_SG // D_A)),
                  _fixed(mall.shape), _fixed(wm.shape),
                  _fixed((1, HEAD_V))],
        out_specs=(pl.BlockSpec((L, D_A), lambda b, c: (zrow(b, c), 0)),
                   pl.BlockSpec((1, 1, N_HEADS, HEAD_K, HEAD_V), lambda b, c: (0, b, 0, 0, 0))),
        out_shape=(jax.ShapeDtypeStruct((MP, D_A), BF16),
                   jax.ShapeDtypeStruct((1, B, N_HEADS, HEAD_K, HEAD_V), F32)),
        scratch_shapes=[pltpu.VMEM((N_HEADS, HEAD_K, HEAD_V), F32)],
        compiler_params=_cparams(("arbitrary", "arbitrary")),
        name="gla_prompt",
    )(z_p, z_p, z_p, jnp.asarray(mall, BF16), jnp.asarray(wm), gn)

    bblk = 16
    assert BS == LANES and BS % bblk == 0
    oa_s, gla_new_s = pl.pallas_call(
        functools.partial(_gla_sample_kernel, bblk=bblk),
        grid=(BS // bblk,),
        in_specs=[_fixed((BS, D_K), (0, Z_LA // D_K)),
                  pl.BlockSpec((bblk, Z_SG), lambda s: (s, 0)),
                  pl.BlockSpec((bblk, D_A), lambda s: (s, Z_SG // D_A)),
                  pl.BlockSpec((bblk, N_HEADS, HEAD_K, HEAD_V), lambda s: (s, 0, 0, 0)),
                  _fixed((1, HEAD_V))],
        out_specs=(pl.BlockSpec((bblk, D_A), lambda s: (s, 0)),
                   pl.BlockSpec((1, bblk, N_HEADS, HEAD_K, HEAD_V), lambda s: (0, s, 0, 0, 0))),
        out_shape=(jax.ShapeDtypeStruct((BS, D_A), F32),
                   jax.ShapeDtypeStruct((1, BS, N_HEADS, HEAD_K, HEAD_V), F32)),
        scratch_shapes=[pltpu.VMEM((N_HEADS, HEAD_K, BS), F32)],
        compiler_params=_cparams(("arbitrary",)),
        name="gla_sample",
    )(z_s, z_s, z_s, state_gla[0], gn)

    nslab = d_conv // LANES
    vec = lambda a: a.reshape(1, d_conv)

    cb_blk = 32
    sc_t = jnp.transpose(state_conv[0], (1, 0, 2))
    c_s, conv_new_s_t = pl.pallas_call(
        _conv_sample_kernel,
        grid=(BS // cb_blk,),
        in_specs=[pl.BlockSpec((CONV_W - 1, cb_blk, d_conv), lambda s: (0, s, 0)),
                  pl.BlockSpec((cb_blk, d_conv), lambda s: (s, Z_GLU // d_conv)),
                  _fixed((CONV_W, d_conv)), _fixed((1, d_conv)), _fixed((1, d_conv)), _fixed((1, d_conv))],
        out_specs=(pl.BlockSpec((cb_blk, d_conv), lambda s: (s, 0)),
                   pl.BlockSpec((CONV_W - 1, cb_blk, d_conv), lambda s: (0, s, 0))),
        out_shape=(jax.ShapeDtypeStruct((BS, d_conv), F32),
                   jax.ShapeDtypeStruct((CONV_W - 1, BS, d_conv), F32)),
        compiler_params=_cparams(("arbitrary",)),
        name="conv_sample",
    )(sc_t, z_s, conv_w[0], vec(conv_b[0]), vec(conv_ln_g[0]), vec(conv_ln_b[0]))
    conv_new_s = jnp.transpose(conv_new_s_t, (1, 0, 2))[None]

    tt = 512
    hist = 32
    nsteps = T // tt
    c_p, conv_new_p = pl.pallas_call(
        functools.partial(_conv_prompt_kernel, tt=tt, nsteps=nsteps, hist=hist),
        grid=(B, nsteps),
        in_specs=[pl.BlockSpec((tt, d_conv), lambda b, t: (b * nsteps + t, Z_GLU // d_conv)),
                  _fixed((CONV_W, nslab, LANES)), _fixed((nslab, LANES)),
                  _fixed((1, d_conv)), _fixed((1, d_conv))],
        out_specs=(pl.BlockSpec((tt, d_conv), lambda b, t: (b * nsteps + t, 0)),
                   pl.BlockSpec((1, 1, CONV_W - 1, d_conv), lambda b, t: (0, b, 0, 0))),
        out_shape=(jax.ShapeDtypeStruct((MP, d_conv), BF16),
                   jax.ShapeDtypeStruct((1, B, CONV_W - 1, d_conv), F32)),
        scratch_shapes=[pltpu.VMEM((nslab, hist + tt, LANES), F32), pltpu.VMEM((nslab, tt, LANES), F32)],
        compiler_params=_cparams(("arbitrary", "arbitrary")),
        name="conv_prompt",
    )(z_p, conv_w[0].reshape(CONV_W, nslab, LANES), conv_b[0].reshape(nslab, LANES),
      vec(conv_ln_g[0]), vec(conv_ln_b[0]))

    zbo = min(1024, D)
    wcols = min(2 * MXU_N, D)
    nwo = D // wcols
    njo = D // zbo
    assert nwo * wcols == D and njo * zbo == D
    row_o = lambda s: jnp.maximum(s - nwo, 0) // njo
    col_o = lambda s: lax.rem(jnp.maximum(s - nwo, 0), njo)
    x1_p, x1_s = pl.pallas_call(
        functools.partial(_outproj_kernel, nw=nwo, nj=njo, tmp=tmp, tms=tms),
        grid=(nwo + n_tiles * njo,),
        in_specs=[pl.BlockSpec((D_A + d_conv, wcols), lambda s: (0, jnp.minimum(s, nwo - 1))),
                  pl.BlockSpec((tmp, zbo), lambda s: (row_o(s), col_o(s))),
                  pl.BlockSpec((tms, zbo), lambda s: (row_o(s), col_o(s))),
                  pl.BlockSpec((tmp, D_A), lambda s: (row_o(s), 0)),
                  pl.BlockSpec((tms, D_A), lambda s: (row_o(s), 0)),
                  pl.BlockSpec((tmp, d_conv), lambda s: (row_o(s), 0)),
                  pl.BlockSpec((tms, d_conv), lambda s: (row_o(s), 0))],
        out_specs=(pl.BlockSpec((tmp, zbo), lambda s: (row_o(s), col_o(s))),
                   pl.BlockSpec((tms, zbo), lambda s: (row_o(s), col_o(s)))),
        out_shape=pair_shapes(D),
        scratch_shapes=[pltpu.VMEM((D // MXU_N, D_A + d_conv, MXU_N), BF16),
                        pltpu.VMEM((tm, D_A), BF16), pltpu.VMEM((tm, d_conv), BF16)],
        compiler_params=_cparams(("arbitrary",)),
        name="out_proj",
    )(w_out[0], xp, xs, oa_p, oa_s, c_p, c_s)

    tf = 256
    nf = d_ff // tf
    assert nf * tf == d_ff
    y_p, y_s = pl.pallas_call(
        functools.partial(_ffn_kernel, tmp=tmp, tms=tms, nf=nf),
        grid=(n_tiles, nf),
        in_specs=[pl.BlockSpec((tmp, D), lambda i, j: (i, 0)), pl.BlockSpec((tms, D), lambda i, j: (i, 0)),
                  _fixed((1, D)), _fixed((1, D)),
                  pl.BlockSpec((D, tf), lambda i, j: (0, j)),
                  pl.BlockSpec((D, tf), lambda i, j: (0, j + nf)),
                  pl.BlockSpec((tf, D), lambda i, j: (j, 0))],
        out_specs=(pl.BlockSpec((tmp, D), lambda i, j: (i, 0)), pl.BlockSpec((tms, D), lambda i, j: (i, 0))),
        out_shape=pair_shapes(D),
        scratch_shapes=[pltpu.VMEM((tm, D), BF16)],
        compiler_params=_cparams(("arbitrary", "arbitrary")),
        name="ffn",
    )(x1_p, x1_s, norm_ffn[0].reshape(1, D), norm_final.reshape(1, D), w_ffn_in[0], w_ffn_in[0], w_ffn_out[0])

    return (y_p.reshape(B, T, D), y_s.reshape(BS, 1, D), gla_new_p, conv_new_p, gla_new_s, conv_new_s)
```

```python
import functools

import numpy as np
import jax
import jax.numpy as jnp
from jax import lax
from jax.experimental import pallas as pl
from jax.experimental.pallas import tpu as pltpu

F32 = jnp.float32
BF16 = jnp.bfloat16

RMS_EPS = 1e-6
LN_EPS = 1e-5
GATE_NORM = 16.0
N_HEADS = 4
HEAD_K = 128
HEAD_V = 256
D_K = N_HEADS * HEAD_K
D_A = N_HEADS * HEAD_V
GATE_RANK = 16
CONV_W = 31

LANES = 128
MXU_N = 256
GLA_CHUNK = 128
VMEM_LIMIT = 58 * 1024 * 1024

ZB = 512
Z_QKV, Z_SG, Z_GLU, Z_LA = 0, 2 * D_K + D_A, 2 * D_K + 2 * D_A, 2 * D_K + 3 * D_A
Z_COLS = Z_LA + D_K


def _cparams(sem):
    return pltpu.CompilerParams(dimension_semantics=sem, vmem_limit_bytes=VMEM_LIMIT)


def _dot(a, b):
    return jnp.dot(a, b, preferred_element_type=F32)


def _dot_nt(a, b):
    return lax.dot_general(a, b, (((1,), (1,)), ((), ())), preferred_element_type=F32)


def _rms_rows(x, gain):
    ms = jnp.mean(x * x, axis=-1, keepdims=True)
    return x * lax.rsqrt(ms + RMS_EPS) * gain


def _log_sigmoid(x):
    return jnp.minimum(x, 0.0) - jnp.log(1.0 + jnp.exp(-jnp.abs(x)))


def _norm_to_scratch(h_ref, xp_ref, xs_ref, gain, tmp, tms, rows_per_pass=256):
    for r0 in range(0, tmp, rows_per_pass):
        h_ref[r0:r0 + rows_per_pass, :] = _rms_rows(xp_ref[r0:r0 + rows_per_pass, :], gain).astype(BF16)
    h_ref[tmp:tmp + tms, :] = _rms_rows(xs_ref[...], gain).astype(BF16)


def _in_proj_kernel(wt_ref, wnext_ref, xp_ref, xs_ref, g_ref, wup_ref, bg_ref, zp_ref, zs_ref,
                    wb_ref, wlr_ref, h_ref, *, nw, nj, tmp, tms):
    s = pl.program_id(0)
    rb = MXU_N
    n_direct = (2 * D_K + D_A) // rb

    @pl.when(s < n_direct)
    def _():
        wb_ref[s] = wt_ref[...].T.astype(BF16)

    @pl.when((s >= n_direct) & (s < nw))
    def _():
        rows = jnp.concatenate([wt_ref[GATE_RANK:rb, :], wnext_ref[...]], axis=0)
        wb_ref[s] = rows.T.astype(BF16)

    @pl.when(s == n_direct)
    def _():
        wlr_ref[...] = wt_ref[0:LANES, :].T.astype(BF16)

    @pl.when(s >= nw)
    def _project():
        j = lax.rem(s - nw, nj)

        def pair(t0):
            h = h_ref[...]
            return jnp.concatenate([_dot(h, wb_ref[t0]), _dot(h, wb_ref[t0 + 1])], axis=1)

        def put(z):
            zp_ref[...] = z[:tmp]
            zs_ref[...] = z[tmp:]

        nqkv, nsg, nglu = Z_SG // ZB, (Z_GLU - Z_SG) // ZB, (Z_LA - Z_GLU) // ZB
        bpz = ZB // rb

        @pl.when(j == 0)
        def _():
            _norm_to_scratch(h_ref, xp_ref, xs_ref, g_ref[...], tmp, tms)
            put(pair(0))

        @pl.when((j > 0) & (j < nqkv))
        def _():
            put(pair(bpz * j))

        @pl.when((j >= nqkv) & (j < nqkv + nsg))
        def _():
            z = pair(bpz * j)
            put(z * jax.nn.sigmoid(z))

        @pl.when((j >= nqkv + nsg) & (j < nqkv + nsg + nglu))
        def _():
            ta = bpz * j
            ua = pair(ta)
            ug = pair(ta + bpz * nglu)
            put(ua * jax.nn.sigmoid(ug))

        @pl.when(j == nqkv + nsg + nglu)
        def _():
            glr = _dot(h_ref[...], wlr_ref[...])[:, 0:GATE_RANK]
            logit = _dot(glr, wup_ref[...]) + bg_ref[...]
            put(_log_sigmoid(logit) * (1.0 / GATE_NORM))


def _conv_prompt_kernel(glu_ref, cw_ref, cb_ref, lg_ref, lb_ref, c_ref, cnew_ref, slab_ref, y_ref,
                        *, tt, nsteps, hist):
    t_idx = pl.program_id(1)
    nslab = slab_ref.shape[0]
    pad = hist - (CONV_W - 1)

    @pl.when(t_idx == 0)
    def _():
        slab_ref[:, 0:hist, :] = jnp.zeros((nslab, hist, LANES), F32)

    @pl.when(t_idx > 0)
    def _():
        slab_ref[:, 0:hist, :] = slab_ref[:, tt:tt + hist, :]

    for sl in range(nslab):
        slab_ref[sl, hist:hist + tt, :] = glu_ref[:, sl * LANES:(sl + 1) * LANES]

    def slab_conv(sl, carry):
        acc = None
        for res in range(8):
            part = None
            for w in range(res, CONV_W, 8):
                term = slab_ref[sl, pad + w:pad + w + tt, :] * cw_ref[w, pl.ds(sl, 1), :]
                part = term if part is None else part + term
            acc = part if acc is None else acc + part
        y_ref[sl] = acc + cb_ref[pl.ds(sl, 1), :]
        return carry

    lax.fori_loop(0, nslab, slab_conv, 0)

    y = y_ref[...]
    inv_c = 1.0 / (nslab * LANES)
    mu = jnp.sum(jnp.sum(y, axis=0), axis=-1, keepdims=True) * inv_c
    yc = y - mu[None]
    var = jnp.sum(jnp.sum(yc * yc, axis=0), axis=-1, keepdims=True) * inv_c
    rstd = lax.rsqrt(var + LN_EPS)
    for sl in range(nslab):
        cs = slice(sl * LANES, (sl + 1) * LANES)
        z = yc[sl] * rstd * lg_ref[:, cs] + lb_ref[:, cs]
        c_ref[:, cs] = (z * jax.nn.sigmoid(z)).astype(c_ref.dtype)

    @pl.when(t_idx == nsteps - 1)
    def _():
        for sl in range(nslab):
            cnew_ref[0, 0, :, sl * LANES:(sl + 1) * LANES] = slab_ref[sl, hist + tt - (CONV_W - 1):hist + tt, :]


def _outproj_kernel(w_ref, xp_ref, xs_ref, oap_ref, oas_ref, cp_ref, cs_ref, op_ref, os_ref,
                    wb_ref, a_ref, c_ref, *, nw, nj, tmp, tms):
    s = pl.program_id(0)
    rb = MXU_N
    wpl = w_ref.shape[1] // rb
    bpc = xp_ref.shape[1] // rb

    @pl.when(s < nw)
    def _load_weight():
        for q in range(wpl):
            wb_ref[s * wpl + q] = w_ref[:, q * rb:(q + 1) * rb].astype(BF16)

    @pl.when(s >= nw)
    def _project():
        j = lax.rem(s - nw, nj)

        @pl.when(j == 0)
        def _():
            a_ref[0:tmp, :] = oap_ref[...]
            a_ref[tmp:tmp + tms, :] = oas_ref[...].astype(BF16)
            c_ref[0:tmp, :] = cp_ref[...]
            c_ref[tmp:tmp + tms, :] = cs_ref[...].astype(BF16)

        for half in range(bpc):
            t = bpc * j + half
            cols = slice(half * rb, (half + 1) * rb)
            z = _dot(a_ref[...], wb_ref[t, 0:D_A, :]) + _dot(c_ref[...], wb_ref[t, D_A:, :])
            op_ref[:, cols] = xp_ref[:, cols] + z[:tmp]
            os_ref[:, cols] = xs_ref[:, cols] + z[tmp:]


def _ffn_kernel(xp_ref, xs_ref, g_ref, gf_ref, wg_ref, wu_ref, wo_ref, op_ref, os_ref, h_ref,
                *, tmp, tms, nf):
    j = pl.program_id(1)

    def ffn_block(acc_p, acc_s):
        h = h_ref[...]
        fg = _dot(h, wg_ref[...].astype(BF16))
        fu = _dot(h, wu_ref[...].astype(BF16))
        act = (fg * jax.nn.sigmoid(fg) * fu).astype(BF16)
        wo = wo_ref[...].astype(BF16)
        half = tmp // 2
        op_ref[0:half, :] = acc_p[0:half, :] + _dot(act[0:half], wo)
        z = _dot(act[half:], wo)
        op_ref[half:tmp, :] = acc_p[half:tmp, :] + z[:tmp - half]
        os_ref[...] = acc_s[...] + z[tmp - half:]

    @pl.when(j == 0)
    def _():
        _norm_to_scratch(h_ref, xp_ref, xs_ref, g_ref[...], tmp, tms)
        ffn_block(xp_ref, xs_ref)

    @pl.when((j > 0) & (j < nf - 1))
    def _():
        ffn_block(op_ref, os_ref)

    @pl.when(j == nf - 1)
    def _():
        ffn_block(op_ref, os_ref)
        gf = gf_ref[...]
        for r0 in range(0, tmp, 256):
            op_ref[r0:r0 + 256, :] = _rms_rows(op_ref[r0:r0 + 256, :], gf)
        os_ref[...] = _rms_rows(os_ref[...], gf)


@functools.lru_cache(maxsize=None)
def _gla_constants(L):
    nlev = int(np.log2(L)) + 1
    m = np.zeros((nlev + 2, L, L), np.float32)
    wm = np.zeros((nlev, L, L), np.float32)
    idx = np.arange(L)
    wm[0] = np.eye(L, dtype=np.float32)
    for lev in range(1, nlev):
        half = L >> lev
        pos = idx % (2 * half)
        upper = pos >= half
        r = idx - pos + half - 1
        t = idx[None, :]
        up_rows = (t > r[:, None]) & (t <= idx[:, None]) & upper[:, None]
        lo_rows = (t > idx[:, None]) & (t <= r[:, None]) & (~upper)[:, None]
        m[lev] = (up_rows | lo_rows).astype(np.float32)
        same = (idx[:, None] // (2 * half)) == (idx[None, :] // (2 * half))
        wm[lev] = (same & upper[:, None] & (~upper)[None, :]).astype(np.float32)
    m[nlev] = (idx[None, :] <= idx[:, None]).astype(np.float32)
    m[nlev + 1] = (idx[None, :] > idx[:, None]).astype(np.float32)
    m = m.reshape((nlev + 2) * L, L)
    return nlev, np.concatenate([m, m], axis=1), wm


def _gla_prompt_kernel(qkv_ref, la_ref, sg_ref, mall_ref, wm_ref, gn_ref,
                       o_ref, snew_ref, s_ref, *, nlev, nsteps, chunks_per_step):
    c = pl.program_id(1)

    @pl.when(c == 0)
    def _():
        s_ref[...] = jnp.zeros_like(s_ref)

    L = GLA_CHUNK
    scale = HEAD_K ** -0.5
    gn = gn_ref[...]
    chunks = [slice(sub * L, (sub + 1) * L) for sub in range(chunks_per_step)]
    pairs = [(rows, h) for rows in chunks for h in range(N_HEADS)]

    def q_of(rows, h):
        return qkv_ref[rows, h * HEAD_K:(h + 1) * HEAD_K] * scale

    def k_of(rows, h):
        return qkv_ref[rows, D_K + h * HEAD_K:D_K + (h + 1) * HEAD_K]

    mall2 = mall_ref[...]
    e_all = {}
    for rows in chunks:
        la = la_ref[rows, :]
        la_hi = la.astype(BF16)
        la_lo = (la - la_hi.astype(F32)).astype(BF16)
        e_all[rows.start] = _dot(mall2, jnp.concatenate([la_hi, la_lo], axis=0))

    def scaled(rows, h, lev):
        if lev == 0:
            return q_of(rows, h).astype(BF16), k_of(rows, h).astype(BF16)
        e = jnp.exp(e_all[rows.start][lev * L:(lev + 1) * L, h * HEAD_K:(h + 1) * HEAD_K])
        return (q_of(rows, h) * e).astype(BF16), (k_of(rows, h) * e).astype(BF16)

    assert len(pairs) % 2 == 0
    zero = jnp.zeros((L, HEAD_K), BF16)
    amat = {}
    for lev in range(nlev):
        for pa, pb in zip(pairs[0::2], pairs[1::2]):
            qa, ka = scaled(*pa, lev)
            qb, kb = scaled(*pb, lev)
            lhs = jnp.concatenate([qa, qb], axis=1)
            rhs_t = jnp.concatenate([jnp.concatenate([ka, zero], axis=1),
                                     jnp.concatenate([zero, kb], axis=1)], axis=0)
            p = _dot_nt(lhs, rhs_t)
            for key, blk in (((pa[0].start, pa[1]), p[:, 0:L]), ((pb[0].start, pb[1]), p[:, L:2 * L])):
                term = wm_ref[lev] * blk
                amat[key] = term if lev == 0 else amat[key] + term

    for rows, h in pairs:
        ks = slice(h * HEAD_K, (h + 1) * HEAD_K)
        vs = slice(h * HEAD_V, (h + 1) * HEAD_V)
        q, k, a = q_of(rows, h), k_of(rows, h), amat[rows.start, h]
        vb = qkv_ref[rows, 2 * D_K + h * HEAD_V:2 * D_K + (h + 1) * HEAD_V].astype(BF16)

        eb = e_all[rows.start][nlev * L:(nlev + 1) * L, ks]
        ek = e_all[rows.start][(nlev + 1) * L:(nlev + 2) * L, ks]
        qh = (q * jnp.exp(eb)).astype(BF16)
        kh = k * jnp.exp(ek)
        s = s_ref[h]
        o = _dot(jnp.concatenate([a.astype(BF16), qh], axis=1),
                 jnp.concatenate([vb, s.astype(BF16)], axis=0))

        blast = jnp.broadcast_to(eb[L - 1:L, :], (HEAD_K, HEAD_K))
        acol = jnp.exp(blast.T)
        acol = jnp.concatenate([acol, acol], axis=1)
        s_ref[h] = acol * s + _dot(kh.T.astype(BF16), vb)

        y = _rms_rows(o, gn) * sg_ref[rows, vs]
        o_ref[rows, vs] = y.astype(o_ref.dtype)

    @pl.when(c == nsteps - 1)
    def _():
        snew_ref[0, 0] = s_ref[...]


def _sample_mixers_kernel(la_ref, qkv_ref, sg_ref, st_ref, gn_ref, sc_ref, glu_ref, cw_ref, cb_ref, lg_ref, lb_ref,
                          o_ref, snew_ref, c_ref, cnew_ref, at_ref, *, bblk):
    s_idx = pl.program_id(0)
    _conv_sample_kernel(sc_ref, glu_ref, cw_ref, cb_ref, lg_ref, lb_ref, c_ref, cnew_ref)

    @pl.when(s_idx == 0)
    def _():
        for h in range(N_HEADS):
            at_ref[h] = jnp.exp(la_ref[:, h * HEAD_K:(h + 1) * HEAD_K]).T

    gn = gn_ref[...]
    scale = HEAD_K ** -0.5
    lane = lax.broadcasted_iota(jnp.int32, (HEAD_K, LANES), 1)
    row = lax.broadcasted_iota(jnp.int32, (bblk, HEAD_V), 0)
    for h in range(N_HEADS):
        vs = slice(h * HEAD_V, (h + 1) * HEAD_V)
        q = (qkv_ref[:, h * HEAD_K:(h + 1) * HEAD_K] * scale).astype(BF16)
        kt = qkv_ref[:, D_K + h * HEAD_K:D_K + (h + 1) * HEAD_K].T.astype(BF16)
        v = qkv_ref[:, 2 * D_K + h * HEAD_V:2 * D_K + (h + 1) * HEAD_V].astype(BF16)
        vdiag = jnp.concatenate([jnp.where(row == j, v, jnp.zeros_like(v)) for j in range(bblk)], axis=1)
        kv = _dot(kt, vdiag)
        outs = []
        for j in range(bblk):
            b = s_idx * bblk + j
            acol = jnp.sum(jnp.where(lane == b, at_ref[h], 0.0), axis=1, keepdims=True)
            s_new = acol * st_ref[j, h] + kv[:, j * HEAD_V:(j + 1) * HEAD_V]
            snew_ref[0, j, h] = s_new
            outs.append(_dot(q, s_new.astype(BF16))[j:j + 1, :])
        for j in range(bblk):
            o_ref[j:j + 1, vs] = _rms_rows(outs[j], gn) * sg_ref[j:j + 1, vs]


def _ln_silu(y, g, b):
    mu = jnp.mean(y, axis=-1, keepdims=True)
    yc = y - mu
    var = jnp.mean(yc * yc, axis=-1, keepdims=True)
    z = yc * lax.rsqrt(var + LN_EPS) * g + b
    return z * jax.nn.sigmoid(z)


def _conv_sample_kernel(sc_ref, glu_ref, cw_ref, cb_ref, lg_ref, lb_ref, c_ref, cnew_ref):
    nh = CONV_W - 1
    g = glu_ref[...]
    y = g * cw_ref[nh:nh + 1, :] + cb_ref[...]
    for w in range(nh):
        y = y + sc_ref[w] * cw_ref[w:w + 1, :]
    c_ref[...] = _ln_silu(y, lg_ref[...], lb_ref[...])
    for w in range(nh - 1):
        cnew_ref[w] = sc_ref[w + 1]
    cnew_ref[nh - 1] = g


def _fixed(shape, idx=None):
    idx = (0,) * len(shape) if idx is None else idx
    return pl.BlockSpec(shape, lambda *_: idx)


def kernel(x_prompt, x_sample, state_gla, state_conv, norm_mix, w_in, w_gate_up, b_gate, gla_norm,
           conv_w, conv_b, conv_ln_g, conv_ln_b, w_out, norm_ffn, w_ffn_in, w_ffn_out, norm_final):
    B, T, D = x_prompt.shape
    BS = x_sample.shape[0]
    assert state_gla.shape[0] == 1 and x_sample.shape[1] == 1
    d_conv = state_conv.shape[-1]
    d_ff = w_ffn_out.shape[1]
    in_cols = w_in.shape[-1]
    assert d_conv == D_A and in_cols == 2 * D_K + D_A + GATE_RANK + D_A + 2 * d_conv
    MP = B * T
    n_tiles = 8
    tmp, tms = MP // n_tiles, BS // n_tiles
    assert tmp * n_tiles == MP and tms * n_tiles == BS and tms % 8 == 0 and tmp % 256 == 0
    tm = tmp + tms

    xp = x_prompt.reshape(MP, D)
    xs = x_sample.reshape(BS, D)

    def pair_shapes(cols):
        return (jax.ShapeDtypeStruct((MP, cols), F32), jax.ShapeDtypeStruct((BS, cols), F32))

    wt = jnp.swapaxes(w_in, 1, 2)[0]
    nw = (in_cols - GATE_RANK) // MXU_N
    assert nw * MXU_N == in_cols - GATE_RANK and MXU_N % GATE_RANK == 0
    nj = Z_COLS // ZB
    row_i = lambda s: jnp.maximum(s - nw, 0) // nj
    col_j = lambda s: lax.rem(jnp.maximum(s - nw, 0), nj)
    wblk = lambda s: jnp.minimum(s, nw - 1)
    z_p, z_s = pl.pallas_call(
        functools.partial(_in_proj_kernel, nw=nw, nj=nj, tmp=tmp, tms=tms),
        grid=(nw + n_tiles * nj,),
        in_specs=[pl.BlockSpec((MXU_N, D), lambda s: (wblk(s), 0)),
                  pl.BlockSpec((GATE_RANK, D), lambda s: ((wblk(s) + 1) * (MXU_N // GATE_RANK), 0)),
                  pl.BlockSpec((tmp, D), lambda s: (row_i(s), 0)),
                  pl.BlockSpec((tms, D), lambda s: (row_i(s), 0)),
                  _fixed((1, D)), _fixed((GATE_RANK, D_K)), _fixed((1, D_K))],
        out_specs=(pl.BlockSpec((tmp, ZB), lambda s: (row_i(s), col_j(s))),
                   pl.BlockSpec((tms, ZB), lambda s: (row_i(s), col_j(s)))),
        out_shape=pair_shapes(Z_COLS),
        scratch_shapes=[pltpu.VMEM((nw, D, MXU_N), BF16), pltpu.VMEM((D, LANES), BF16),
                        pltpu.VMEM((tm, D), BF16)],
        compiler_params=_cparams(("arbitrary",)),
        name="in_proj",
    )(wt, wt, xp, xs, norm_mix[0].reshape(1, D), w_gate_up[0], b_gate[0].reshape(1, D_K))

    chunks_per_step = 4
    L = GLA_CHUNK * chunks_per_step
    nchunks = T // L
    assert nchunks * L == T
    nlev, mall, wm = _gla_constants(GLA_CHUNK)
    gn = gla_norm[0].reshape(1, HEAD_V)
    zrow = lambda b, c: b * nchunks + c
    oa_p, gla_new_p = pl.pallas_call(
        functools.partial(_gla_prompt_kernel, nlev=nlev, nsteps=nchunks, chunks_per_step=chunks_per_step),
        grid=(B, nchunks),
        in_specs=[pl.BlockSpec((L, Z_SG), lambda b, c: (zrow(b, c), 0)),
                  pl.BlockSpec((L, D_K), lambda b, c: (zrow(b, c), Z_LA // D_K)),
                  pl.BlockSpec((L, D_A), lambda b, c: (zrow(b, c), Z_SG // D_A)),
                  _fixed(mall.shape), _fixed(wm.shape),
                  _fixed((1, HEAD_V))],
        out_specs=(pl.BlockSpec((L, D_A), lambda b, c: (zrow(b, c), 0)),
                   pl.BlockSpec((1, 1, N_HEADS, HEAD_K, HEAD_V), lambda b, c: (0, b, 0, 0, 0))),
        out_shape=(jax.ShapeDtypeStruct((MP, D_A), BF16),
                   jax.ShapeDtypeStruct((1, B, N_HEADS, HEAD_K, HEAD_V), F32)),
        scratch_shapes=[pltpu.VMEM((N_HEADS, HEAD_K, HEAD_V), F32)],
        compiler_params=_cparams(("arbitrary", "arbitrary")),
        name="gla_prompt",
    )(z_p, z_p, z_p, jnp.asarray(mall, BF16), jnp.asarray(wm), gn)

    nslab = d_conv // LANES
    vec = lambda a: a.reshape(1, d_conv)

    bblk = 16
    assert BS == LANES and BS % bblk == 0
    sc_t = jnp.transpose(state_conv[0], (1, 0, 2))
    oa_s, gla_new_s, c_s, conv_new_s_t = pl.pallas_call(
        functools.partial(_sample_mixers_kernel, bblk=bblk),
        grid=(BS // bblk,),
        in_specs=[_fixed((BS, D_K), (0, Z_LA // D_K)),
                  pl.BlockSpec((bblk, Z_SG), lambda s: (s, 0)),
                  pl.BlockSpec((bblk, D_A), lambda s: (s, Z_SG // D_A)),
                  pl.BlockSpec((bblk, N_HEADS, HEAD_K, HEAD_V), lambda s: (s, 0, 0, 0)),
                  _fixed((1, HEAD_V)),
                  pl.BlockSpec((CONV_W - 1, bblk, d_conv), lambda s: (0, s, 0)),
                  pl.BlockSpec((bblk, d_conv), lambda s: (s, Z_GLU // d_conv)),
                  _fixed((CONV_W, d_conv)), _fixed((1, d_conv)), _fixed((1, d_conv)), _fixed((1, d_conv))],
        out_specs=(pl.BlockSpec((bblk, D_A), lambda s: (s, 0)),
                   pl.BlockSpec((1, bblk, N_HEADS, HEAD_K, HEAD_V), lambda s: (0, s, 0, 0, 0)),
                   pl.BlockSpec((bblk, d_conv), lambda s: (s, 0)),
                   pl.BlockSpec((CONV_W - 1, bblk, d_conv), lambda s: (0, s, 0))),
        out_shape=(jax.ShapeDtypeStruct((BS, D_A), F32),
                   jax.ShapeDtypeStruct((1, BS, N_HEADS, HEAD_K, HEAD_V), F32),
                   jax.ShapeDtypeStruct((BS, d_conv), F32),
                   jax.ShapeDtypeStruct((CONV_W - 1, BS, d_conv), F32)),
        scratch_shapes=[pltpu.VMEM((N_HEADS, HEAD_K, BS), F32)],
        compiler_params=_cparams(("arbitrary",)),
        name="sample_mixers",
    )(z_s, z_s, z_s, state_gla[0], gn,
      sc_t, z_s, conv_w[0], vec(conv_b[0]), vec(conv_ln_g[0]), vec(conv_ln_b[0]))
    conv_new_s = jnp.transpose(conv_new_s_t, (1, 0, 2))[None]

    tt = 512
    hist = 32
    nsteps = T // tt
    c_p, conv_new_p = pl.pallas_call(
        functools.partial(_conv_prompt_kernel, tt=tt, nsteps=nsteps, hist=hist),
        grid=(B, nsteps),
        in_specs=[pl.BlockSpec((tt, d_conv), lambda b, t: (b * nsteps + t, Z_GLU // d_conv)),
                  _fixed((CONV_W, nslab, LANES)), _fixed((nslab, LANES)),
                  _fixed((1, d_conv)), _fixed((1, d_conv))],
        out_specs=(pl.BlockSpec((tt, d_conv), lambda b, t: (b * nsteps + t, 0)),
                   pl.BlockSpec((1, 1, CONV_W - 1, d_conv), lambda b, t: (0, b, 0, 0))),
        out_shape=(jax.ShapeDtypeStruct((MP, d_conv), BF16),
                   jax.ShapeDtypeStruct((1, B, CONV_W - 1, d_conv), F32)),
        scratch_shapes=[pltpu.VMEM((nslab, hist + tt, LANES), F32), pltpu.VMEM((nslab, tt, LANES), F32)],
        compiler_params=_cparams(("arbitrary", "arbitrary")),
        name="conv_prompt",
    )(z_p, conv_w[0].reshape(CONV_W, nslab, LANES), conv_b[0].reshape(nslab, LANES),
      vec(conv_ln_g[0]), vec(conv_ln_b[0]))

    zbo = min(1024, D)
    wcols = min(2 * MXU_N, D)
    nwo = D // wcols
    njo = D // zbo
    assert nwo * wcols == D and njo * zbo == D
    row_o = lambda s: jnp.maximum(s - nwo, 0) // njo
    col_o = lambda s: lax.rem(jnp.maximum(s - nwo, 0), njo)
    x1_p, x1_s = pl.pallas_call(
        functools.partial(_outproj_kernel, nw=nwo, nj=njo, tmp=tmp, tms=tms),
        grid=(nwo + n_tiles * njo,),
        in_specs=[pl.BlockSpec((D_A + d_conv, wcols), lambda s: (0, jnp.minimum(s, nwo - 1))),
                  pl.BlockSpec((tmp, zbo), lambda s: (row_o(s), col_o(s))),
                  pl.BlockSpec((tms, zbo), lambda s: (row_o(s), col_o(s))),
                  pl.BlockSpec((tmp, D_A), lambda s: (row_o(s), 0)),
                  pl.BlockSpec((tms, D_A), lambda s: (row_o(s), 0)),
                  pl.BlockSpec((tmp, d_conv), lambda s: (row_o(s), 0)),
                  pl.BlockSpec((tms, d_conv), lambda s: (row_o(s), 0))],
        out_specs=(pl.BlockSpec((tmp, zbo), lambda s: (row_o(s), col_o(s))),
                   pl.BlockSpec((tms, zbo), lambda s: (row_o(s), col_o(s)))),
        out_shape=pair_shapes(D),
        scratch_shapes=[pltpu.VMEM((D // MXU_N, D_A + d_conv, MXU_N), BF16),
                        pltpu.VMEM((tm, D_A), BF16), pltpu.VMEM((tm, d_conv), BF16)],
        compiler_params=_cparams(("arbitrary",)),
        name="out_proj",
    )(w_out[0], xp, xs, oa_p, oa_s, c_p, c_s)

    tf = 256
    nf = d_ff // tf
    assert nf * tf == d_ff
    y_p, y_s = pl.pallas_call(
        functools.partial(_ffn_kernel, tmp=tmp, tms=tms, nf=nf),
        grid=(n_tiles, nf),
        in_specs=[pl.BlockSpec((tmp, D), lambda i, j: (i, 0)), pl.BlockSpec((tms, D), lambda i, j: (i, 0)),
                  _fixed((1, D)), _fixed((1, D)),
                  pl.BlockSpec((D, tf), lambda i, j: (0, j)),
                  pl.BlockSpec((D, tf), lambda i, j: (0, j + nf)),
                  pl.BlockSpec((tf, D), lambda i, j: (j, 0))],
        out_specs=(pl.BlockSpec((tmp, D), lambda i, j: (i, 0)), pl.BlockSpec((tms, D), lambda i, j: (i, 0))),
        out_shape=pair_shapes(D),
        scratch_shapes=[pltpu.VMEM((tm, D), BF16)],
        compiler_params=_cparams(("arbitrary", "arbitrary")),
        name="ffn",
    )(x1_p, x1_s, norm_ffn[0].reshape(1, D), norm_final.reshape(1, D), w_ffn_in[0], w_ffn_in[0], w_ffn_out[0])

    return (y_p.reshape(B, T, D), y_s.reshape(BS, 1, D), gla_new_p, conv_new_p, gla_new_s, conv_new_s)
```

```python
import functools

import numpy as np
import jax
import jax.numpy as jnp
from jax import lax
from jax.experimental import pallas as pl
from jax.experimental.pallas import tpu as pltpu

F32 = jnp.float32
BF16 = jnp.bfloat16

RMS_EPS = 1e-6
LN_EPS = 1e-5
GATE_NORM = 16.0
N_HEADS = 4
HEAD_K = 128
HEAD_V = 256
D_K = N_HEADS * HEAD_K
D_A = N_HEADS * HEAD_V
GATE_RANK = 16
CONV_W = 31

LANES = 128
MXU_N = 256
GLA_CHUNK = 128
VMEM_LIMIT = 58 * 1024 * 1024

ZB = 512
Z_QKV, Z_SG, Z_GLU, Z_LA = 0, 2 * D_K + D_A, 2 * D_K + 2 * D_A, 2 * D_K + 3 * D_A
Z_COLS = Z_LA + D_K


def _cparams(sem):
    return pltpu.CompilerParams(dimension_semantics=sem, vmem_limit_bytes=VMEM_LIMIT)


def _dot(a, b):
    return jnp.dot(a, b, preferred_element_type=F32)


def _dot_nt(a, b):
    return lax.dot_general(a, b, (((1,), (1,)), ((), ())), preferred_element_type=F32)


def _rms_rows(x, gain):
    ms = jnp.mean(x * x, axis=-1, keepdims=True)
    return x * lax.rsqrt(ms + RMS_EPS) * gain


def _log_sigmoid(x):
    return jnp.minimum(x, 0.0) - jnp.log(1.0 + jnp.exp(-jnp.abs(x)))


def _norm_to_scratch(h_ref, xp_ref, xs, gain, tmp, tms, rows_per_pass=256):
    for r0 in range(0, tmp, rows_per_pass):
        h_ref[r0:r0 + rows_per_pass, :] = _rms_rows(xp_ref[r0:r0 + rows_per_pass, :], gain).astype(BF16)
    h_ref[tmp:tmp + tms, :] = _rms_rows(xs, gain).astype(BF16)


def _in_proj_kernel(wt_ref, wnext_ref, xp_ref, xs_ref, g_ref, wup_ref, bg_ref, zp_ref, zs_ref,
                    wb_ref, wlr_ref, h_ref, *, nw, nj, tmp, tms):
    s = pl.program_id(0)
    rb = MXU_N
    n_direct = (2 * D_K + D_A) // rb

    @pl.when(s < n_direct)
    def _():
        wb_ref[s] = wt_ref[...].T.astype(BF16)

    @pl.when((s >= n_direct) & (s < nw))
    def _():
        rows = jnp.concatenate([wt_ref[GATE_RANK:rb, :], wnext_ref[...]], axis=0)
        wb_ref[s] = rows.T.astype(BF16)

    @pl.when(s == n_direct)
    def _():
        wlr_ref[...] = wt_ref[0:LANES, :].T.astype(BF16)

    @pl.when(s >= nw)
    def _project():
        j = lax.rem(s - nw, nj)

        def pair(t0):
            h = h_ref[...]
            return jnp.concatenate([_dot(h, wb_ref[t0]), _dot(h, wb_ref[t0 + 1])], axis=1)

        def put(z):
            zp_ref[...] = z[:tmp]
            zs_ref[...] = z[tmp:]

        nqkv, nsg, nglu = Z_SG // ZB, (Z_GLU - Z_SG) // ZB, (Z_LA - Z_GLU) // ZB
        bpz = ZB // rb

        @pl.when(j == 0)
        def _():
            _norm_to_scratch(h_ref, xp_ref, xs_ref[:, 0, :], g_ref[...], tmp, tms)
            put(pair(0))

        @pl.when((j > 0) & (j < nqkv))
        def _():
            put(pair(bpz * j))

        @pl.when((j >= nqkv) & (j < nqkv + nsg))
        def _():
            z = pair(bpz * j)
            put(z * jax.nn.sigmoid(z))

        @pl.when((j >= nqkv + nsg) & (j < nqkv + nsg + nglu))
        def _():
            ta = bpz * j
            ua = pair(ta)
            ug = pair(ta + bpz * nglu)
            put(ua * jax.nn.sigmoid(ug))

        @pl.when(j == nqkv + nsg + nglu)
        def _():
            glr = _dot(h_ref[...], wlr_ref[...])[:, 0:GATE_RANK]
            logit = _dot(glr, wup_ref[...]) + bg_ref[...]
            put(_log_sigmoid(logit) * (1.0 / GATE_NORM))


def _conv_prompt_kernel(glu_ref, cw_ref, cb_ref, lg_ref, lb_ref, c_ref, cnew_ref, slab_ref, y_ref,
                        *, tt, nsteps, hist):
    t_idx = pl.program_id(1)
    nslab = slab_ref.shape[0]
    pad = hist - (CONV_W - 1)

    @pl.when(t_idx == 0)
    def _():
        slab_ref[:, 0:hist, :] = jnp.zeros((nslab, hist, LANES), F32)

    @pl.when(t_idx > 0)
    def _():
        slab_ref[:, 0:hist, :] = slab_ref[:, tt:tt + hist, :]

    for sl in range(nslab):
        slab_ref[sl, hist:hist + tt, :] = glu_ref[:, sl * LANES:(sl + 1) * LANES]

    def slab_conv(sl, carry):
        acc = None
        for res in range(8):
            part = None
            for w in range(res, CONV_W, 8):
                term = slab_ref[sl, pad + w:pad + w + tt, :] * cw_ref[w, pl.ds(sl, 1), :]
                part = term if part is None else part + term
            acc = part if acc is None else acc + part
        y_ref[sl] = acc + cb_ref[pl.ds(sl, 1), :]
        return carry

    lax.fori_loop(0, nslab, slab_conv, 0)

    y = y_ref[...]
    inv_c = 1.0 / (nslab * LANES)
    mu = jnp.sum(jnp.sum(y, axis=0), axis=-1, keepdims=True) * inv_c
    yc = y - mu[None]
    var = jnp.sum(jnp.sum(yc * yc, axis=0), axis=-1, keepdims=True) * inv_c
    rstd = lax.rsqrt(var + LN_EPS)
    for sl in range(nslab):
        cs = slice(sl * LANES, (sl + 1) * LANES)
        z = yc[sl] * rstd * lg_ref[:, cs] + lb_ref[:, cs]
        c_ref[:, cs] = (z * jax.nn.sigmoid(z)).astype(c_ref.dtype)

    @pl.when(t_idx == nsteps - 1)
    def _():
        for sl in range(nslab):
            cnew_ref[0, 0, :, sl * LANES:(sl + 1) * LANES] = slab_ref[sl, hist + tt - (CONV_W - 1):hist + tt, :]


def _outproj_kernel(w_ref, xp_ref, xs_ref, oap_ref, oas_ref, cp_ref, cs_ref, op_ref, os_ref,
                    wb_ref, a_ref, c_ref, *, nw, nj, tmp, tms):
    s = pl.program_id(0)
    rb = MXU_N
    wpl = w_ref.shape[1] // rb
    bpc = xp_ref.shape[1] // rb

    @pl.when(s < nw)
    def _load_weight():
        for q in range(wpl):
            wb_ref[s * wpl + q] = w_ref[:, q * rb:(q + 1) * rb].astype(BF16)

    @pl.when(s >= nw)
    def _project():
        j = lax.rem(s - nw, nj)

        @pl.when(j == 0)
        def _():
            a_ref[0:tmp, :] = oap_ref[...]
            a_ref[tmp:tmp + tms, :] = oas_ref[...].astype(BF16)
            c_ref[0:tmp, :] = cp_ref[...]
            c_ref[tmp:tmp + tms, :] = cs_ref[...].astype(BF16)

        for half in range(bpc):
            t = bpc * j + half
            cols = slice(half * rb, (half + 1) * rb)
            z = _dot(a_ref[...], wb_ref[t, 0:D_A, :]) + _dot(c_ref[...], wb_ref[t, D_A:, :])
            op_ref[:, cols] = xp_ref[:, cols] + z[:tmp]
            os_ref[:, cols] = xs_ref[:, 0, cols] + z[tmp:]


def _ffn_kernel(xp_ref, xs_ref, g_ref, gf_ref, wg_ref, wu_ref, wo_ref, op_ref, os_ref, h_ref, accs_ref,
                *, tmp, tms, nf):
    j = pl.program_id(1)

    def ffn_block(acc_p, acc_s):
        h = h_ref[...]
        fg = _dot(h, wg_ref[...].astype(BF16))
        fu = _dot(h, wu_ref[...].astype(BF16))
        act = (fg * jax.nn.sigmoid(fg) * fu).astype(BF16)
        wo = wo_ref[...].astype(BF16)
        half = tmp // 2
        op_ref[0:half, :] = acc_p[0:half, :] + _dot(act[0:half], wo)
        z = _dot(act[half:], wo)
        op_ref[half:tmp, :] = acc_p[half:tmp, :] + z[:tmp - half]
        accs_ref[...] = acc_s[...] + z[tmp - half:]

    @pl.when(j == 0)
    def _():
        _norm_to_scratch(h_ref, xp_ref, xs_ref[...], g_ref[...], tmp, tms)
        ffn_block(xp_ref, xs_ref)

    @pl.when((j > 0) & (j < nf - 1))
    def _():
        ffn_block(op_ref, accs_ref)

    @pl.when(j == nf - 1)
    def _():
        ffn_block(op_ref, accs_ref)
        gf = gf_ref[...]
        for r0 in range(0, tmp, 256):
            op_ref[r0:r0 + 256, :] = _rms_rows(op_ref[r0:r0 + 256, :], gf)
        os_ref[:, 0, :] = _rms_rows(accs_ref[...], gf)


@functools.lru_cache(maxsize=None)
def _gla_constants(L):
    nlev = int(np.log2(L)) + 1
    m = np.zeros((nlev + 2, L, L), np.float32)
    wm = np.zeros((nlev, L, L), np.float32)
    idx = np.arange(L)
    wm[0] = np.eye(L, dtype=np.float32)
    for lev in range(1, nlev):
        half = L >> lev
        pos = idx % (2 * half)
        upper = pos >= half
        r = idx - pos + half - 1
        t = idx[None, :]
        up_rows = (t > r[:, None]) & (t <= idx[:, None]) & upper[:, None]
        lo_rows = (t > idx[:, None]) & (t <= r[:, None]) & (~upper)[:, None]
        m[lev] = (up_rows | lo_rows).astype(np.float32)
        same = (idx[:, None] // (2 * half)) == (idx[None, :] // (2 * half))
        wm[lev] = (same & upper[:, None] & (~upper)[None, :]).astype(np.float32)
    m[nlev] = (idx[None, :] <= idx[:, None]).astype(np.float32)
    m[nlev + 1] = (idx[None, :] > idx[:, None]).astype(np.float32)
    m = m.reshape((nlev + 2) * L, L)
    return nlev, np.concatenate([m, m], axis=1), wm


def _gla_prompt_kernel(qkv_ref, la_ref, sg_ref, mall_ref, wm_ref, gn_ref,
                       o_ref, snew_ref, s_ref, *, nlev, nsteps, chunks_per_step):
    c = pl.program_id(1)

    @pl.when(c == 0)
    def _():
        s_ref[...] = jnp.zeros_like(s_ref)

    L = GLA_CHUNK
    scale = HEAD_K ** -0.5
    gn = gn_ref[...]
    chunks = [slice(sub * L, (sub + 1) * L) for sub in range(chunks_per_step)]
    pairs = [(rows, h) for rows in chunks for h in range(N_HEADS)]

    def q_of(rows, h):
        return qkv_ref[rows, h * HEAD_K:(h + 1) * HEAD_K] * scale

    def k_of(rows, h):
        return qkv_ref[rows, D_K + h * HEAD_K:D_K + (h + 1) * HEAD_K]

    mall2 = mall_ref[...]
    e_all = {}
    for rows in chunks:
        la = la_ref[rows, :]
        la_hi = la.astype(BF16)
        la_lo = (la - la_hi.astype(F32)).astype(BF16)
        e_all[rows.start] = _dot(mall2, jnp.concatenate([la_hi, la_lo], axis=0))

    def scaled(rows, h, lev):
        if lev == 0:
            return q_of(rows, h).astype(BF16), k_of(rows, h).astype(BF16)
        e = jnp.exp(e_all[rows.start][lev * L:(lev + 1) * L, h * HEAD_K:(h + 1) * HEAD_K])
        return (q_of(rows, h) * e).astype(BF16), (k_of(rows, h) * e).astype(BF16)

    assert len(pairs) % 2 == 0
    zero = jnp.zeros((L, HEAD_K), BF16)
    amat = {}
    for lev in range(nlev):
        for pa, pb in zip(pairs[0::2], pairs[1::2]):
            qa, ka = scaled(*pa, lev)
            qb, kb = scaled(*pb, lev)
            lhs = jnp.concatenate([qa, qb], axis=1)
            rhs_t = jnp.concatenate([jnp.concatenate([ka, zero], axis=1),
                                     jnp.concatenate([zero, kb], axis=1)], axis=0)
            p = _dot_nt(lhs, rhs_t)
            for key, blk in (((pa[0].start, pa[1]), p[:, 0:L]), ((pb[0].start, pb[1]), p[:, L:2 * L])):
                term = wm_ref[lev] * blk
                amat[key] = term if lev == 0 else amat[key] + term

    for rows, h in pairs:
        ks = slice(h * HEAD_K, (h + 1) * HEAD_K)
        vs = slice(h * HEAD_V, (h + 1) * HEAD_V)
        q, k, a = q_of(rows, h), k_of(rows, h), amat[rows.start, h]
        vb = qkv_ref[rows, 2 * D_K + h * HEAD_V:2 * D_K + (h + 1) * HEAD_V].astype(BF16)

        eb = e_all[rows.start][nlev * L:(nlev + 1) * L, ks]
        ek = e_all[rows.start][(nlev + 1) * L:(nlev + 2) * L, ks]
        qh = (q * jnp.exp(eb)).astype(BF16)
        kh = k * jnp.exp(ek)
        s = s_ref[h]
        o = _dot(jnp.concatenate([a.astype(BF16), qh], axis=1),
                 jnp.concatenate([vb, s.astype(BF16)], axis=0))

        blast = jnp.broadcast_to(eb[L - 1:L, :], (HEAD_K, HEAD_K))
        acol = jnp.exp(blast.T)
        acol = jnp.concatenate([acol, acol], axis=1)
        s_ref[h] = acol * s + _dot(kh.T.astype(BF16), vb)

        y = _rms_rows(o, gn) * sg_ref[rows, vs]
        o_ref[rows, vs] = y.astype(o_ref.dtype)

    @pl.when(c == nsteps - 1)
    def _():
        snew_ref[0, 0] = s_ref[...]


def _sample_mixers_kernel(la_ref, qkv_ref, sg_ref, st_ref, gn_ref, sc_ref, glu_ref, cw_ref, cb_ref, lg_ref, lb_ref,
                          o_ref, snew_ref, c_ref, cnew_ref, at_ref, *, bblk):
    s_idx = pl.program_id(0)
    _conv_sample_kernel(sc_ref, glu_ref, cw_ref, cb_ref, lg_ref, lb_ref, c_ref, cnew_ref)

    @pl.when(s_idx == 0)
    def _():
        for h in range(N_HEADS):
            at_ref[h] = jnp.exp(la_ref[:, h * HEAD_K:(h + 1) * HEAD_K]).T

    gn = gn_ref[...]
    scale = HEAD_K ** -0.5
    lane = lax.broadcasted_iota(jnp.int32, (HEAD_K, LANES), 1)
    row = lax.broadcasted_iota(jnp.int32, (bblk, HEAD_V), 0)
    for h in range(N_HEADS):
        vs = slice(h * HEAD_V, (h + 1) * HEAD_V)
        q = (qkv_ref[:, h * HEAD_K:(h + 1) * HEAD_K] * scale).astype(BF16)
        kt = qkv_ref[:, D_K + h * HEAD_K:D_K + (h + 1) * HEAD_K].T.astype(BF16)
        v = qkv_ref[:, 2 * D_K + h * HEAD_V:2 * D_K + (h + 1) * HEAD_V].astype(BF16)
        vdiag = jnp.concatenate([jnp.where(row == j, v, jnp.zeros_like(v)) for j in range(bblk)], axis=1)
        kv = _dot(kt, vdiag)
        outs = []
        for j in range(bblk):
            b = s_idx * bblk + j
            acol = jnp.sum(jnp.where(lane == b, at_ref[h], 0.0), axis=1, keepdims=True)
            s_new = acol * st_ref[j, h] + kv[:, j * HEAD_V:(j + 1) * HEAD_V]
            snew_ref[0, j, h] = s_new
            outs.append(_dot(q, s_new.astype(BF16))[j:j + 1, :])
        for j in range(bblk):
            o_ref[j:j + 1, vs] = _rms_rows(outs[j], gn) * sg_ref[j:j + 1, vs]


def _ln_silu(y, g, b):
    mu = jnp.mean(y, axis=-1, keepdims=True)
    yc = y - mu
    var = jnp.mean(yc * yc, axis=-1, keepdims=True)
    z = yc * lax.rsqrt(var + LN_EPS) * g + b
    return z * jax.nn.sigmoid(z)


def _conv_sample_kernel(sc_ref, glu_ref, cw_ref, cb_ref, lg_ref, lb_ref, c_ref, cnew_ref):
    nh = CONV_W - 1
    g = glu_ref[...]
    y = g * cw_ref[nh:nh + 1, :] + cb_ref[...]
    for w in range(nh):
        y = y + sc_ref[w] * cw_ref[w:w + 1, :]
    c_ref[...] = _ln_silu(y, lg_ref[...], lb_ref[...])
    for w in range(nh - 1):
        cnew_ref[w] = sc_ref[w + 1]
    cnew_ref[nh - 1] = g


def _fixed(shape, idx=None):
    idx = (0,) * len(shape) if idx is None else idx
    return pl.BlockSpec(shape, lambda *_: idx)


def kernel(x_prompt, x_sample, state_gla, state_conv, norm_mix, w_in, w_gate_up, b_gate, gla_norm,
           conv_w, conv_b, conv_ln_g, conv_ln_b, w_out, norm_ffn, w_ffn_in, w_ffn_out, norm_final):
    B, T, D = x_prompt.shape
    BS = x_sample.shape[0]
    assert state_gla.shape[0] == 1 and x_sample.shape[1] == 1
    d_conv = state_conv.shape[-1]
    d_ff = w_ffn_out.shape[1]
    in_cols = w_in.shape[-1]
    assert d_conv == D_A and in_cols == 2 * D_K + D_A + GATE_RANK + D_A + 2 * d_conv
    MP = B * T
    n_tiles = 8
    tmp, tms = MP // n_tiles, BS // n_tiles
    assert tmp * n_tiles == MP and tms * n_tiles == BS and tms % 8 == 0 and tmp % 256 == 0
    tm = tmp + tms

    xp = x_prompt.reshape(MP, D)

    def pair_shapes(cols):
        return (jax.ShapeDtypeStruct((MP, cols), F32), jax.ShapeDtypeStruct((BS, cols), F32))

    wt = jnp.swapaxes(w_in, 1, 2)[0]
    nw = (in_cols - GATE_RANK) // MXU_N
    assert nw * MXU_N == in_cols - GATE_RANK and MXU_N % GATE_RANK == 0
    nj = Z_COLS // ZB
    row_i = lambda s: jnp.maximum(s - nw, 0) // nj
    col_j = lambda s: lax.rem(jnp.maximum(s - nw, 0), nj)
    wblk = lambda s: jnp.minimum(s, nw - 1)
    z_p, z_s = pl.pallas_call(
        functools.partial(_in_proj_kernel, nw=nw, nj=nj, tmp=tmp, tms=tms),
        grid=(nw + n_tiles * nj,),
        in_specs=[pl.BlockSpec((MXU_N, D), lambda s: (wblk(s), 0)),
                  pl.BlockSpec((GATE_RANK, D), lambda s: ((wblk(s) + 1) * (MXU_N // GATE_RANK), 0)),
                  pl.BlockSpec((tmp, D), lambda s: (row_i(s), 0)),
                  pl.BlockSpec((tms, 1, D), lambda s: (row_i(s), 0, 0)),
                  _fixed((1, D)), _fixed((GATE_RANK, D_K)), _fixed((1, D_K))],
        out_specs=(pl.BlockSpec((tmp, ZB), lambda s: (row_i(s), col_j(s))),
                   pl.BlockSpec((tms, ZB), lambda s: (row_i(s), col_j(s)))),
        out_shape=pair_shapes(Z_COLS),
        scratch_shapes=[pltpu.VMEM((nw, D, MXU_N), BF16), pltpu.VMEM((D, LANES), BF16),
                        pltpu.VMEM((tm, D), BF16)],
        compiler_params=_cparams(("arbitrary",)),
        name="in_proj",
    )(wt, wt, xp, x_sample, norm_mix[0].reshape(1, D), w_gate_up[0], b_gate[0].reshape(1, D_K))

    chunks_per_step = 4
    L = GLA_CHUNK * chunks_per_step
    nchunks = T // L
    assert nchunks * L == T
    nlev, mall, wm = _gla_constants(GLA_CHUNK)
    gn = gla_norm[0].reshape(1, HEAD_V)
    zrow = lambda b, c: b * nchunks + c
    oa_p, gla_new_p = pl.pallas_call(
        functools.partial(_gla_prompt_kernel, nlev=nlev, nsteps=nchunks, chunks_per_step=chunks_per_step),
        grid=(B, nchunks),
        in_specs=[pl.BlockSpec((L, Z_SG), lambda b, c: (zrow(b, c), 0)),
                  pl.BlockSpec((L, D_K), lambda b, c: (zrow(b, c), Z_LA // D_K)),
                  pl.BlockSpec((L, D_A), lambda b, c: (zrow(b, c), Z_SG // D_A)),
                  _fixed(mall.shape), _fixed(wm.shape),
                  _fixed((1, HEAD_V))],
        out_specs=(pl.BlockSpec((L, D_A), lambda b, c: (zrow(b, c), 0)),
                   pl.BlockSpec((1, 1, N_HEADS, HEAD_K, HEAD_V), lambda b, c: (0, b, 0, 0, 0))),
        out_shape=(jax.ShapeDtypeStruct((MP, D_A), BF16),
                   jax.ShapeDtypeStruct((1, B, N_HEADS, HEAD_K, HEAD_V), F32)),
        scratch_shapes=[pltpu.VMEM((N_HEADS, HEAD_K, HEAD_V), F32)],
        compiler_params=_cparams(("arbitrary", "arbitrary")),
        name="gla_prompt",
    )(z_p, z_p, z_p, jnp.asarray(mall, BF16), jnp.asarray(wm), gn)

    nslab = d_conv // LANES
    vec = lambda a: a.reshape(1, d_conv)

    bblk = 16
    assert BS == LANES and BS % bblk == 0
    sc_t = jnp.transpose(state_conv[0], (1, 0, 2))
    oa_s, gla_new_s, c_s, conv_new_s_t = pl.pallas_call(
        functools.partial(_sample_mixers_kernel, bblk=bblk),
        grid=(BS // bblk,),
        in_specs=[_fixed((BS, D_K), (0, Z_LA // D_K)),
                  pl.BlockSpec((bblk, Z_SG), lambda s: (s, 0)),
                  pl.BlockSpec((bblk, D_A), lambda s: (s, Z_SG // D_A)),
                  pl.BlockSpec((bblk, N_HEADS, HEAD_K, HEAD_V), lambda s: (s, 0, 0, 0)),
                  _fixed((1, HEAD_V)),
                  pl.BlockSpec((CONV_W - 1, bblk, d_conv), lambda s: (0, s, 0)),
                  pl.BlockSpec((bblk, d_conv), lambda s: (s, Z_GLU // d_conv)),
                  _fixed((CONV_W, d_conv)), _fixed((1, d_conv)), _fixed((1, d_conv)), _fixed((1, d_conv))],
        out_specs=(pl.BlockSpec((bblk, D_A), lambda s: (s, 0)),
                   pl.BlockSpec((1, bblk, N_HEADS, HEAD_K, HEAD_V), lambda s: (0, s, 0, 0, 0)),
                   pl.BlockSpec((bblk, d_conv), lambda s: (s, 0)),
                   pl.BlockSpec((CONV_W - 1, bblk, d_conv), lambda s: (0, s, 0))),
        out_shape=(jax.ShapeDtypeStruct((BS, D_A), F32),
                   jax.ShapeDtypeStruct((1, BS, N_HEADS, HEAD_K, HEAD_V), F32),
                   jax.ShapeDtypeStruct((BS, d_conv), F32),
                   jax.ShapeDtypeStruct((CONV_W - 1, BS, d_conv), F32)),
        scratch_shapes=[pltpu.VMEM((N_HEADS, HEAD_K, BS), F32)],
        compiler_params=_cparams(("arbitrary",)),
        name="sample_mixers",
    )(z_s, z_s, z_s, state_gla[0], gn,
      sc_t, z_s, conv_w[0], vec(conv_b[0]), vec(conv_ln_g[0]), vec(conv_ln_b[0]))
    conv_new_s = jnp.transpose(conv_new_s_t, (1, 0, 2))[None]

    tt = 512
    hist = 32
    nsteps = T // tt
    c_p, conv_new_p = pl.pallas_call(
        functools.partial(_conv_prompt_kernel, tt=tt, nsteps=nsteps, hist=hist),
        grid=(B, nsteps),
        in_specs=[pl.BlockSpec((tt, d_conv), lambda b, t: (b * nsteps + t, Z_GLU // d_conv)),
                  _fixed((CONV_W, nslab, LANES)), _fixed((nslab, LANES)),
                  _fixed((1, d_conv)), _fixed((1, d_conv))],
        out_specs=(pl.BlockSpec((tt, d_conv), lambda b, t: (b * nsteps + t, 0)),
                   pl.BlockSpec((1, 1, CONV_W - 1, d_conv), lambda b, t: (0, b, 0, 0))),
        out_shape=(jax.ShapeDtypeStruct((MP, d_conv), BF16),
                   jax.ShapeDtypeStruct((1, B, CONV_W - 1, d_conv), F32)),
        scratch_shapes=[pltpu.VMEM((nslab, hist + tt, LANES), F32), pltpu.VMEM((nslab, tt, LANES), F32)],
        compiler_params=_cparams(("arbitrary", "arbitrary")),
        name="conv_prompt",
    )(z_p, conv_w[0].reshape(CONV_W, nslab, LANES), conv_b[0].reshape(nslab, LANES),
      vec(conv_ln_g[0]), vec(conv_ln_b[0]))

    zbo = min(1024, D)
    wcols = min(2 * MXU_N, D)
    nwo = D // wcols
    njo = D // zbo
    assert nwo * wcols == D and njo * zbo == D
    row_o = lambda s: jnp.maximum(s - nwo, 0) // njo
    col_o = lambda s: lax.rem(jnp.maximum(s - nwo, 0), njo)
    x1_p, x1_s = pl.pallas_call(
        functools.partial(_outproj_kernel, nw=nwo, nj=njo, tmp=tmp, tms=tms),
        grid=(nwo + n_tiles * njo,),
        in_specs=[pl.BlockSpec((D_A + d_conv, wcols), lambda s: (0, jnp.minimum(s, nwo - 1))),
                  pl.BlockSpec((tmp, zbo), lambda s: (row_o(s), col_o(s))),
                  pl.BlockSpec((tms, 1, zbo), lambda s: (row_o(s), 0, col_o(s))),
                  pl.BlockSpec((tmp, D_A), lambda s: (row_o(s), 0)),
                  pl.BlockSpec((tms, D_A), lambda s: (row_o(s), 0)),
                  pl.BlockSpec((tmp, d_conv), lambda s: (row_o(s), 0)),
                  pl.BlockSpec((tms, d_conv), lambda s: (row_o(s), 0))],
        out_specs=(pl.BlockSpec((tmp, zbo), lambda s: (row_o(s), col_o(s))),
                   pl.BlockSpec((tms, zbo), lambda s: (row_o(s), col_o(s)))),
        out_shape=pair_shapes(D),
        scratch_shapes=[pltpu.VMEM((D // MXU_N, D_A + d_conv, MXU_N), BF16),
                        pltpu.VMEM((tm, D_A), BF16), pltpu.VMEM((tm, d_conv), BF16)],
        compiler_params=_cparams(("arbitrary",)),
        name="out_proj",
    )(w_out[0], xp, x_sample, oa_p, oa_s, c_p, c_s)

    tf = 256
    nf = d_ff // tf
    assert nf * tf == d_ff
    y_p, y_s = pl.pallas_call(
        functools.partial(_ffn_kernel, tmp=tmp, tms=tms, nf=nf),
        grid=(n_tiles, nf),
        in_specs=[pl.BlockSpec((tmp, D), lambda i, j: (i, 0)), pl.BlockSpec((tms, D), lambda i, j: (i, 0)),
                  _fixed((1, D)), _fixed((1, D)),
                  pl.BlockSpec((D, tf), lambda i, j: (0, j)),
                  pl.BlockSpec((D, tf), lambda i, j: (0, j + nf)),
                  pl.BlockSpec((tf, D), lambda i, j: (j, 0))],
        out_specs=(pl.BlockSpec((tmp, D), lambda i, j: (i, 0)),
                   pl.BlockSpec((tms, 1, D), lambda i, j: (i, 0, 0))),
        out_shape=(jax.ShapeDtypeStruct((MP, D), F32), jax.ShapeDtypeStruct((BS, 1, D), F32)),
        scratch_shapes=[pltpu.VMEM((tm, D), BF16), pltpu.VMEM((tms, D), F32)],
        compiler_params=_cparams(("arbitrary", "arbitrary")),
        name="ffn",
    )(x1_p, x1_s, norm_ffn[0].reshape(1, D), norm_final.reshape(1, D), w_ffn_in[0], w_ffn_in[0], w_ffn_out[0])

    return (y_p.reshape(B, T, D), y_s, gla_new_p, conv_new_p, gla_new_s, conv_new_s)
```

```python
import functools

import numpy as np
import jax
import jax.numpy as jnp
from jax import lax
from jax.experimental import pallas as pl
from jax.experimental.pallas import tpu as pltpu

F32 = jnp.float32
BF16 = jnp.bfloat16

RMS_EPS = 1e-6
LN_EPS = 1e-5
GATE_NORM = 16.0
N_HEADS = 4
HEAD_K = 128
HEAD_V = 256
D_K = N_HEADS * HEAD_K
D_A = N_HEADS * HEAD_V
GATE_RANK = 16
CONV_W = 31

LANES = 128
MXU_N = 256
GLA_CHUNK = 128
VMEM_LIMIT = 58 * 1024 * 1024

ZB = 512
Z_QKV, Z_SG, Z_GLU, Z_LA = 0, 2 * D_K + D_A, 2 * D_K + 2 * D_A, 2 * D_K + 3 * D_A
Z_COLS = Z_LA + D_K


def _cparams(sem):
    return pltpu.CompilerParams(dimension_semantics=sem, vmem_limit_bytes=VMEM_LIMIT)


def _dot(a, b):
    return jnp.dot(a, b, preferred_element_type=F32)


def _dot_nt(a, b):
    return lax.dot_general(a, b, (((1,), (1,)), ((), ())), preferred_element_type=F32)


def _rms_rows(x, gain):
    ms = jnp.mean(x * x, axis=-1, keepdims=True)
    return x * lax.rsqrt(ms + RMS_EPS) * gain


def _log_sigmoid(x):
    return jnp.minimum(x, 0.0) - jnp.log(1.0 + jnp.exp(-jnp.abs(x)))


def _norm_to_scratch(h_ref, xp_ref, xs, gain, tmp, tms, rows_per_pass=256):
    for r0 in range(0, tmp, rows_per_pass):
        h_ref[r0:r0 + rows_per_pass, :] = _rms_rows(xp_ref[r0:r0 + rows_per_pass, :], gain).astype(BF16)
    h_ref[tmp:tmp + tms, :] = _rms_rows(xs, gain).astype(BF16)


def _in_proj_kernel(wt_ref, wnext_ref, xp_ref, xs_ref, g_ref, wup_ref, bg_ref, zp_ref, zs_ref,
                    wb_ref, wlr_ref, h_ref, *, nw, nj, tmp, tms):
    s = pl.program_id(0)
    rb = MXU_N
    n_direct = (2 * D_K + D_A) // rb

    @pl.when(s < n_direct)
    def _():
        wb_ref[s] = wt_ref[...].T.astype(BF16)

    @pl.when((s >= n_direct) & (s < nw))
    def _():
        rows = jnp.concatenate([wt_ref[GATE_RANK:rb, :], wnext_ref[...]], axis=0)
        wb_ref[s] = rows.T.astype(BF16)

    @pl.when(s == n_direct)
    def _():
        wlr_ref[...] = wt_ref[0:LANES, :].T.astype(BF16)

    @pl.when(s >= nw)
    def _project():
        j = lax.rem(s - nw, nj)

        def pair(t0):
            h = h_ref[...]
            return jnp.concatenate([_dot(h, wb_ref[t0]), _dot(h, wb_ref[t0 + 1])], axis=1)

        def put(z):
            zp_ref[...] = z[:tmp]
            zs_ref[...] = z[tmp:]

        nqkv, nsg, nglu = Z_SG // ZB, (Z_GLU - Z_SG) // ZB, (Z_LA - Z_GLU) // ZB
        bpz = ZB // rb

        @pl.when(j == 0)
        def _():
            _norm_to_scratch(h_ref, xp_ref, xs_ref[:, 0, :], g_ref[...], tmp, tms)
            put(pair(0))

        @pl.when((j > 0) & (j < nqkv))
        def _():
            put(pair(bpz * j))

        @pl.when((j >= nqkv) & (j < nqkv + nsg))
        def _():
            z = pair(bpz * j)
            put(z * jax.nn.sigmoid(z))

        @pl.when((j >= nqkv + nsg) & (j < nqkv + nsg + nglu))
        def _():
            ta = bpz * j
            ua = pair(ta)
            ug = pair(ta + bpz * nglu)
            put(ua * jax.nn.sigmoid(ug))

        @pl.when(j == nqkv + nsg + nglu)
        def _():
            glr = _dot(h_ref[...], wlr_ref[...])[:, 0:GATE_RANK]
            logit = _dot(glr, wup_ref[...]) + bg_ref[...]
            put(_log_sigmoid(logit) * (1.0 / GATE_NORM))


def _conv_prompt_kernel(glu_ref, cw_ref, cb_ref, lg_ref, lb_ref, c_ref, cnew_ref, slab_ref, y_ref,
                        *, tt, nsteps, hist):
    t_idx = pl.program_id(1)
    nslab = slab_ref.shape[0]
    pad = hist - (CONV_W - 1)

    @pl.when(t_idx == 0)
    def _():
        slab_ref[:, 0:hist, :] = jnp.zeros((nslab, hist, LANES), F32)

    @pl.when(t_idx > 0)
    def _():
        slab_ref[:, 0:hist, :] = slab_ref[:, tt:tt + hist, :]

    for sl in range(nslab):
        slab_ref[sl, hist:hist + tt, :] = glu_ref[:, sl * LANES:(sl + 1) * LANES]

    def slab_conv(sl, carry):
        acc = None
        for res in range(8):
            part = None
            for w in range(res, CONV_W, 8):
                term = slab_ref[sl, pad + w:pad + w + tt, :] * cw_ref[w, pl.ds(sl, 1), :]
                part = term if part is None else part + term
            acc = part if acc is None else acc + part
        y_ref[sl] = acc + cb_ref[pl.ds(sl, 1), :]
        return carry

    lax.fori_loop(0, nslab, slab_conv, 0)

    y = y_ref[...]
    inv_c = 1.0 / (nslab * LANES)
    mu = jnp.sum(jnp.sum(y, axis=0), axis=-1, keepdims=True) * inv_c
    yc = y - mu[None]
    var = jnp.sum(jnp.sum(yc * yc, axis=0), axis=-1, keepdims=True) * inv_c
    rstd = lax.rsqrt(var + LN_EPS)
    for sl in range(nslab):
        cs = slice(sl * LANES, (sl + 1) * LANES)
        z = yc[sl] * rstd * lg_ref[:, cs] + lb_ref[:, cs]
        c_ref[:, cs] = (z * jax.nn.sigmoid(z)).astype(c_ref.dtype)

    @pl.when(t_idx == nsteps - 1)
    def _():
        for sl in range(nslab):
            cnew_ref[0, 0, :, sl * LANES:(sl + 1) * LANES] = slab_ref[sl, hist + tt - (CONV_W - 1):hist + tt, :]


def _outproj_kernel(w_ref, xp_ref, xs_ref, oap_ref, oas_ref, cp_ref, cs_ref, op_ref, os_ref,
                    wb_ref, a_ref, c_ref, *, nw, nj, tmp, tms):
    s = pl.program_id(0)
    rb = MXU_N
    wpl = w_ref.shape[1] // rb
    bpc = xp_ref.shape[1] // rb

    @pl.when(s < nw)
    def _load_weight():
        for q in range(wpl):
            wb_ref[s * wpl + q] = w_ref[:, q * rb:(q + 1) * rb].astype(BF16)

    @pl.when(s >= nw)
    def _project():
        j = lax.rem(s - nw, nj)

        @pl.when(j == 0)
        def _():
            a_ref[0:tmp, :] = oap_ref[...]
            a_ref[tmp:tmp + tms, :] = oas_ref[...].astype(BF16)
            c_ref[0:tmp, :] = cp_ref[...]
            c_ref[tmp:tmp + tms, :] = cs_ref[...].astype(BF16)

        for half in range(bpc):
            t = bpc * j + half
            cols = slice(half * rb, (half + 1) * rb)
            z = _dot(a_ref[...], wb_ref[t, 0:D_A, :]) + _dot(c_ref[...], wb_ref[t, D_A:, :])
            op_ref[:, cols] = xp_ref[:, cols] + z[:tmp]
            os_ref[:, cols] = xs_ref[:, 0, cols] + z[tmp:]


def _ffn_kernel(xp_ref, xs_ref, g_ref, gf_ref, wg_ref, wu_ref, wo_ref, op_ref, os_ref, h_ref, accs_ref,
                *, tmp, tms, nf):
    j = pl.program_id(1)

    def ffn_block(acc_p, acc_s):
        h = h_ref[...]
        fg = _dot(h, wg_ref[...].astype(BF16))
        fu = _dot(h, wu_ref[...].astype(BF16))
        act = (fg * jax.nn.sigmoid(fg) * fu).astype(BF16)
        wo = wo_ref[...].astype(BF16)
        half = tmp // 2
        op_ref[0:half, :] = acc_p[0:half, :] + _dot(act[0:half], wo)
        z = _dot(act[half:], wo)
        op_ref[half:tmp, :] = acc_p[half:tmp, :] + z[:tmp - half]
        accs_ref[...] = acc_s[...] + z[tmp - half:]

    @pl.when(j == 0)
    def _():
        _norm_to_scratch(h_ref, xp_ref, xs_ref[...], g_ref[...], tmp, tms)
        ffn_block(xp_ref, xs_ref)

    @pl.when((j > 0) & (j < nf - 1))
    def _():
        ffn_block(op_ref, accs_ref)

    @pl.when(j == nf - 1)
    def _():
        ffn_block(op_ref, accs_ref)
        gf = gf_ref[...]
        for r0 in range(0, tmp, 256):
            op_ref[r0:r0 + 256, :] = _rms_rows(op_ref[r0:r0 + 256, :], gf)
        os_ref[:, 0, :] = _rms_rows(accs_ref[...], gf)


@functools.lru_cache(maxsize=None)
def _gla_constants(L):
    nlev = int(np.log2(L)) + 1
    m = np.zeros((nlev + 2, L, L), np.float32)
    wm = np.zeros((nlev, L, L), np.float32)
    idx = np.arange(L)
    wm[0] = np.eye(L, dtype=np.float32)
    for lev in range(1, nlev):
        half = L >> lev
        pos = idx % (2 * half)
        upper = pos >= half
        r = idx - pos + half - 1
        t = idx[None, :]
        up_rows = (t > r[:, None]) & (t <= idx[:, None]) & upper[:, None]
        lo_rows = (t > idx[:, None]) & (t <= r[:, None]) & (~upper)[:, None]
        m[lev] = (up_rows | lo_rows).astype(np.float32)
        same = (idx[:, None] // (2 * half)) == (idx[None, :] // (2 * half))
        wm[lev] = (same & upper[:, None] & (~upper)[None, :]).astype(np.float32)
    m[nlev] = (idx[None, :] <= idx[:, None]).astype(np.float32)
    m[nlev + 1] = (idx[None, :] > idx[:, None]).astype(np.float32)
    m = m.reshape((nlev + 2) * L, L)
    return nlev, np.concatenate([m, m], axis=1), wm


def _gla_prompt_kernel(qkv_ref, la_ref, sg_ref, mall_ref, wm_ref, gn_ref,
                       o_ref, snew_ref, s_ref, *, nlev, nsteps, chunks_per_step):
    c = pl.program_id(1)

    @pl.when(c == 0)
    def _():
        s_ref[...] = jnp.zeros_like(s_ref)

    L = GLA_CHUNK
    scale = HEAD_K ** -0.5
    gn = gn_ref[...]
    chunks = [slice(sub * L, (sub + 1) * L) for sub in range(chunks_per_step)]
    pairs = [(rows, h) for rows in chunks for h in range(N_HEADS)]

    def q_of(rows, h):
        return qkv_ref[rows, h * HEAD_K:(h + 1) * HEAD_K] * scale

    def k_of(rows, h):
        return qkv_ref[rows, D_K + h * HEAD_K:D_K + (h + 1) * HEAD_K]

    mall2 = mall_ref[...]
    e_all = {}
    for rows in chunks:
        la = la_ref[rows, :]
        la_hi = la.astype(BF16)
        la_lo = (la - la_hi.astype(F32)).astype(BF16)
        e_all[rows.start] = _dot(mall2, jnp.concatenate([la_hi, la_lo], axis=0))

    def scaled(rows, h, lev):
        if lev == 0:
            return q_of(rows, h).astype(BF16), k_of(rows, h).astype(BF16)
        e = jnp.exp(e_all[rows.start][lev * L:(lev + 1) * L, h * HEAD_K:(h + 1) * HEAD_K])
        return (q_of(rows, h) * e).astype(BF16), (k_of(rows, h) * e).astype(BF16)

    assert len(pairs) % 2 == 0
    zero = jnp.zeros((L, HEAD_K), BF16)
    amat = {}
    for lev in range(nlev):
        for pa, pb in zip(pairs[0::2], pairs[1::2]):
            qa, ka = scaled(*pa, lev)
            qb, kb = scaled(*pb, lev)
            lhs = jnp.concatenate([qa, qb], axis=1)
            rhs_t = jnp.concatenate([jnp.concatenate([ka, zero], axis=1),
                                     jnp.concatenate([zero, kb], axis=1)], axis=0)
            p = _dot_nt(lhs, rhs_t)
            for key, blk in (((pa[0].start, pa[1]), p[:, 0:L]), ((pb[0].start, pb[1]), p[:, L:2 * L])):
                term = wm_ref[lev] * blk
                amat[key] = term if lev == 0 else amat[key] + term

    for rows, h in pairs:
        ks = slice(h * HEAD_K, (h + 1) * HEAD_K)
        vs = slice(h * HEAD_V, (h + 1) * HEAD_V)
        q, k, a = q_of(rows, h), k_of(rows, h), amat[rows.start, h]
        vb = qkv_ref[rows, 2 * D_K + h * HEAD_V:2 * D_K + (h + 1) * HEAD_V].astype(BF16)

        eb = e_all[rows.start][nlev * L:(nlev + 1) * L, ks]
        ek = e_all[rows.start][(nlev + 1) * L:(nlev + 2) * L, ks]
        qh = (q * jnp.exp(eb)).astype(BF16)
        kh = k * jnp.exp(ek)
        s = s_ref[h]
        o = _dot(jnp.concatenate([a.astype(BF16), qh], axis=1),
                 jnp.concatenate([vb, s.astype(BF16)], axis=0))

        blast = jnp.broadcast_to(eb[L - 1:L, :], (HEAD_K, HEAD_K))
        acol = jnp.exp(blast.T)
        acol = jnp.concatenate([acol, acol], axis=1)
        s_ref[h] = acol * s + _dot(kh.T.astype(BF16), vb)

        y = _rms_rows(o, gn) * sg_ref[rows, vs]
        o_ref[rows, vs] = y.astype(o_ref.dtype)

    @pl.when(c == nsteps - 1)
    def _():
        snew_ref[0, 0] = s_ref[...]


def _sample_mixers_kernel(la_ref, qkv_ref, sg_ref, st_ref, gn_ref, sc_ref, glu_ref, cw_ref, cb_ref, lg_ref, lb_ref,
                          o_ref, snew_ref, c_ref, cnew_ref, at_ref, *, bblk):
    s_idx = pl.program_id(0)
    _conv_sample_kernel(sc_ref, glu_ref, cw_ref, cb_ref, lg_ref, lb_ref, c_ref, cnew_ref)

    @pl.when(s_idx == 0)
    def _():
        for h in range(N_HEADS):
            at_ref[h] = jnp.exp(la_ref[:, h * HEAD_K:(h + 1) * HEAD_K]).T

    gn = gn_ref[...]
    scale = HEAD_K ** -0.5
    lane = lax.broadcasted_iota(jnp.int32, (HEAD_K, LANES), 1)
    row = lax.broadcasted_iota(jnp.int32, (bblk, HEAD_V), 0)
    for h in range(N_HEADS):
        vs = slice(h * HEAD_V, (h + 1) * HEAD_V)
        q = (qkv_ref[:, h * HEAD_K:(h + 1) * HEAD_K] * scale).astype(BF16)
        kt = qkv_ref[:, D_K + h * HEAD_K:D_K + (h + 1) * HEAD_K].T.astype(BF16)
        v = qkv_ref[:, 2 * D_K + h * HEAD_V:2 * D_K + (h + 1) * HEAD_V].astype(BF16)
        vdiag = jnp.concatenate([jnp.where(row == j, v, jnp.zeros_like(v)) for j in range(bblk)], axis=1)
        kv = _dot(kt, vdiag)
        outs = []
        for j in range(bblk):
            b = s_idx * bblk + j
            acol = jnp.sum(jnp.where(lane == b, at_ref[h], 0.0), axis=1, keepdims=True)
            s_new = acol * st_ref[j, h] + kv[:, j * HEAD_V:(j + 1) * HEAD_V]
            snew_ref[0, j, h] = s_new
            outs.append(_dot(q, s_new.astype(BF16))[j:j + 1, :])
        for j in range(bblk):
            o_ref[j:j + 1, vs] = _rms_rows(outs[j], gn) * sg_ref[j:j + 1, vs]


def _ln_silu(y, g, b):
    mu = jnp.mean(y, axis=-1, keepdims=True)
    yc = y - mu
    var = jnp.mean(yc * yc, axis=-1, keepdims=True)
    z = yc * lax.rsqrt(var + LN_EPS) * g + b
    return z * jax.nn.sigmoid(z)


def _conv_sample_kernel(sc_ref, glu_ref, cw_ref, cb_ref, lg_ref, lb_ref, c_ref, cnew_ref):
    nh = CONV_W - 1
    g = glu_ref[...]
    ys = []
    for sl in range(cw_ref.shape[1]):
        ls = slice(sl * LANES, (sl + 1) * LANES)
        y = g[:, ls] * cw_ref[nh, sl:sl + 1, :] + cb_ref[sl:sl + 1, :]
        for w in range(nh):
            y = y + sc_ref[w, :, ls] * cw_ref[w, sl:sl + 1, :]
        ys.append(y)
    c_ref[...] = _ln_silu(jnp.concatenate(ys, axis=1), lg_ref[...], lb_ref[...])
    for w in range(nh - 1):
        cnew_ref[w] = sc_ref[w + 1]
    cnew_ref[nh - 1] = g


def _fixed(shape, idx=None):
    idx = (0,) * len(shape) if idx is None else idx
    return pl.BlockSpec(shape, lambda *_: idx)


def kernel(x_prompt, x_sample, state_gla, state_conv, norm_mix, w_in, w_gate_up, b_gate, gla_norm,
           conv_w, conv_b, conv_ln_g, conv_ln_b, w_out, norm_ffn, w_ffn_in, w_ffn_out, norm_final):
    B, T, D = x_prompt.shape
    BS = x_sample.shape[0]
    assert state_gla.shape[0] == 1 and x_sample.shape[1] == 1
    d_conv = state_conv.shape[-1]
    d_ff = w_ffn_out.shape[1]
    in_cols = w_in.shape[-1]
    assert d_conv == D_A and in_cols == 2 * D_K + D_A + GATE_RANK + D_A + 2 * d_conv
    MP = B * T
    n_tiles = 8
    tmp, tms = MP // n_tiles, BS // n_tiles
    assert tmp * n_tiles == MP and tms * n_tiles == BS and tms % 8 == 0 and tmp % 256 == 0
    tm = tmp + tms

    xp = x_prompt.reshape(MP, D)

    def pair_shapes(cols):
        return (jax.ShapeDtypeStruct((MP, cols), F32), jax.ShapeDtypeStruct((BS, cols), F32))

    wt = jnp.swapaxes(w_in, 1, 2)[0]
    nw = (in_cols - GATE_RANK) // MXU_N
    assert nw * MXU_N == in_cols - GATE_RANK and MXU_N % GATE_RANK == 0
    nj = Z_COLS // ZB
    row_i = lambda s: jnp.maximum(s - nw, 0) // nj
    col_j = lambda s: lax.rem(jnp.maximum(s - nw, 0), nj)
    wblk = lambda s: jnp.minimum(s, nw - 1)
    z_p, z_s = pl.pallas_call(
        functools.partial(_in_proj_kernel, nw=nw, nj=nj, tmp=tmp, tms=tms),
        grid=(nw + n_tiles * nj,),
        in_specs=[pl.BlockSpec((MXU_N, D), lambda s: (wblk(s), 0)),
                  pl.BlockSpec((GATE_RANK, D), lambda s: ((wblk(s) + 1) * (MXU_N // GATE_RANK), 0)),
                  pl.BlockSpec((tmp, D), lambda s: (row_i(s), 0)),
                  pl.BlockSpec((tms, 1, D), lambda s: (row_i(s), 0, 0)),
                  _fixed((1, D)), _fixed((GATE_RANK, D_K)), _fixed((1, D_K))],
        out_specs=(pl.BlockSpec((tmp, ZB), lambda s: (row_i(s), col_j(s))),
                   pl.BlockSpec((tms, ZB), lambda s: (row_i(s), col_j(s)))),
        out_shape=pair_shapes(Z_COLS),
        scratch_shapes=[pltpu.VMEM((nw, D, MXU_N), BF16), pltpu.VMEM((D, LANES), BF16),
                        pltpu.VMEM((tm, D), BF16)],
        compiler_params=_cparams(("arbitrary",)),
        name="in_proj",
    )(wt, wt, xp, x_sample, norm_mix[0].reshape(1, D), w_gate_up[0], b_gate[0].reshape(1, D_K))

    chunks_per_step = 4
    L = GLA_CHUNK * chunks_per_step
    nchunks = T // L
    assert nchunks * L == T
    nlev, mall, wm = _gla_constants(GLA_CHUNK)
    gn = gla_norm[0].reshape(1, HEAD_V)
    zrow = lambda b, c: b * nchunks + c
    oa_p, gla_new_p = pl.pallas_call(
        functools.partial(_gla_prompt_kernel, nlev=nlev, nsteps=nchunks, chunks_per_step=chunks_per_step),
        grid=(B, nchunks),
        in_specs=[pl.BlockSpec((L, Z_SG), lambda b, c: (zrow(b, c), 0)),
                  pl.BlockSpec((L, D_K), lambda b, c: (zrow(b, c), Z_LA // D_K)),
                  pl.BlockSpec((L, D_A), lambda b, c: (zrow(b, c), Z_SG // D_A)),
                  _fixed(mall.shape), _fixed(wm.shape),
                  _fixed((1, HEAD_V))],
        out_specs=(pl.BlockSpec((L, D_A), lambda b, c: (zrow(b, c), 0)),
                   pl.BlockSpec((1, 1, N_HEADS, HEAD_K, HEAD_V), lambda b, c: (0, b, 0, 0, 0))),
        out_shape=(jax.ShapeDtypeStruct((MP, D_A), BF16),
                   jax.ShapeDtypeStruct((1, B, N_HEADS, HEAD_K, HEAD_V), F32)),
        scratch_shapes=[pltpu.VMEM((N_HEADS, HEAD_K, HEAD_V), F32)],
        compiler_params=_cparams(("arbitrary", "arbitrary")),
        name="gla_prompt",
    )(z_p, z_p, z_p, jnp.asarray(mall, BF16), jnp.asarray(wm), gn)

    nslab = d_conv // LANES
    vec = lambda a: a.reshape(1, d_conv)
    cw_slabs = conv_w[0].reshape(CONV_W, nslab, LANES)
    cb_slabs = conv_b[0].reshape(nslab, LANES)

    bblk = 16
    assert BS == LANES and BS % bblk == 0
    sc_t = jnp.transpose(state_conv[0], (1, 0, 2))
    oa_s, gla_new_s, c_s, conv_new_s_t = pl.pallas_call(
        functools.partial(_sample_mixers_kernel, bblk=bblk),
        grid=(BS // bblk,),
        in_specs=[_fixed((BS, D_K), (0, Z_LA // D_K)),
                  pl.BlockSpec((bblk, Z_SG), lambda s: (s, 0)),
                  pl.BlockSpec((bblk, D_A), lambda s: (s, Z_SG // D_A)),
                  pl.BlockSpec((bblk, N_HEADS, HEAD_K, HEAD_V), lambda s: (s, 0, 0, 0)),
                  _fixed((1, HEAD_V)),
                  pl.BlockSpec((CONV_W - 1, bblk, d_conv), lambda s: (0, s, 0)),
                  pl.BlockSpec((bblk, d_conv), lambda s: (s, Z_GLU // d_conv)),
                  _fixed((CONV_W, nslab, LANES)), _fixed((nslab, LANES)), _fixed((1, d_conv)), _fixed((1, d_conv))],
        out_specs=(pl.BlockSpec((bblk, D_A), lambda s: (s, 0)),
                   pl.BlockSpec((1, bblk, N_HEADS, HEAD_K, HEAD_V), lambda s: (0, s, 0, 0, 0)),
                   pl.BlockSpec((bblk, d_conv), lambda s: (s, 0)),
                   pl.BlockSpec((CONV_W - 1, bblk, d_conv), lambda s: (0, s, 0))),
        out_shape=(jax.ShapeDtypeStruct((BS, D_A), F32),
                   jax.ShapeDtypeStruct((1, BS, N_HEADS, HEAD_K, HEAD_V), F32),
                   jax.ShapeDtypeStruct((BS, d_conv), F32),
                   jax.ShapeDtypeStruct((CONV_W - 1, BS, d_conv), F32)),
        scratch_shapes=[pltpu.VMEM((N_HEADS, HEAD_K, BS), F32)],
        compiler_params=_cparams(("arbitrary",)),
        name="sample_mixers",
    )(z_s, z_s, z_s, state_gla[0], gn,
      sc_t, z_s, cw_slabs, cb_slabs, vec(conv_ln_g[0]), vec(conv_ln_b[0]))
    conv_new_s = jnp.transpose(conv_new_s_t, (1, 0, 2))[None]

    tt = 512
    hist = 32
    nsteps = T // tt
    c_p, conv_new_p = pl.pallas_call(
        functools.partial(_conv_prompt_kernel, tt=tt, nsteps=nsteps, hist=hist),
        grid=(B, nsteps),
        in_specs=[pl.BlockSpec((tt, d_conv), lambda b, t: (b * nsteps + t, Z_GLU // d_conv)),
                  _fixed((CONV_W, nslab, LANES)), _fixed((nslab, LANES)),
                  _fixed((1, d_conv)), _fixed((1, d_conv))],
        out_specs=(pl.BlockSpec((tt, d_conv), lambda b, t: (b * nsteps + t, 0)),
                   pl.BlockSpec((1, 1, CONV_W - 1, d_conv), lambda b, t: (0, b, 0, 0))),
        out_shape=(jax.ShapeDtypeStruct((MP, d_conv), BF16),
                   jax.ShapeDtypeStruct((1, B, CONV_W - 1, d_conv), F32)),
        scratch_shapes=[pltpu.VMEM((nslab, hist + tt, LANES), F32), pltpu.VMEM((nslab, tt, LANES), F32)],
        compiler_params=_cparams(("arbitrary", "arbitrary")),
        name="conv_prompt",
    )(z_p, cw_slabs, cb_slabs, vec(conv_ln_g[0]), vec(conv_ln_b[0]))

    zbo = min(1024, D)
    wcols = min(2 * MXU_N, D)
    nwo = D // wcols
    njo = D // zbo
    assert nwo * wcols == D and njo * zbo == D
    row_o = lambda s: jnp.maximum(s - nwo, 0) // njo
    col_o = lambda s: lax.rem(jnp.maximum(s - nwo, 0), njo)
    x1_p, x1_s = pl.pallas_call(
        functools.partial(_outproj_kernel, nw=nwo, nj=njo, tmp=tmp, tms=tms),
        grid=(nwo + n_tiles * njo,),
        in_specs=[pl.BlockSpec((D_A + d_conv, wcols), lambda s: (0, jnp.minimum(s, nwo - 1))),
                  pl.BlockSpec((tmp, zbo), lambda s: (row_o(s), col_o(s))),
                  pl.BlockSpec((tms, 1, zbo), lambda s: (row_o(s), 0, col_o(s))),
                  pl.BlockSpec((tmp, D_A), lambda s: (row_o(s), 0)),
                  pl.BlockSpec((tms, D_A), lambda s: (row_o(s), 0)),
                  pl.BlockSpec((tmp, d_conv), lambda s: (row_o(s), 0)),
                  pl.BlockSpec((tms, d_conv), lambda s: (row_o(s), 0))],
        out_specs=(pl.BlockSpec((tmp, zbo), lambda s: (row_o(s), col_o(s))),
                   pl.BlockSpec((tms, zbo), lambda s: (row_o(s), col_o(s)))),
        out_shape=pair_shapes(D),
        scratch_shapes=[pltpu.VMEM((D // MXU_N, D_A + d_conv, MXU_N), BF16),
                        pltpu.VMEM((tm, D_A), BF16), pltpu.VMEM((tm, d_conv), BF16)],
        compiler_params=_cparams(("arbitrary",)),
        name="out_proj",
    )(w_out[0], xp, x_sample, oa_p, oa_s, c_p, c_s)

    tf = 256
    nf = d_ff // tf
    assert nf * tf == d_ff
    y_p, y_s = pl.pallas_call(
        functools.partial(_ffn_kernel, tmp=tmp, tms=tms, nf=nf),
        grid=(n_tiles, nf),
        in_specs=[pl.BlockSpec((tmp, D), lambda i, j: (i, 0)), pl.BlockSpec((tms, D), lambda i, j: (i, 0)),
                  _fixed((1, D)), _fixed((1, D)),
                  pl.BlockSpec((D, tf), lambda i, j: (0, j)),
                  pl.BlockSpec((D, tf), lambda i, j: (0, j + nf)),
                  pl.BlockSpec((tf, D), lambda i, j: (j, 0))],
        out_specs=(pl.BlockSpec((tmp, D), lambda i, j: (i, 0)),
                   pl.BlockSpec((tms, 1, D), lambda i, j: (i, 0, 0))),
        out_shape=(jax.ShapeDtypeStruct((MP, D), F32), jax.ShapeDtypeStruct((BS, 1, D), F32)),
        scratch_shapes=[pltpu.VMEM((tm, D), BF16), pltpu.VMEM((tms, D), F32)],
        compiler_params=_cparams(("arbitrary", "arbitrary")),
        name="ffn",
    )(x1_p, x1_s, norm_ffn[0].reshape(1, D), norm_final.reshape(1, D), w_ffn_in[0], w_ffn_in[0], w_ffn_out[0])

    return (y_p.reshape(B, T, D), y_s, gla_new_p, conv_new_p, gla_new_s, conv_new_s)
```

```python
import functools

import numpy as np
import jax
import jax.numpy as jnp
from jax import lax
from jax.experimental import pallas as pl
from jax.experimental.pallas import tpu as pltpu

F32 = jnp.float32
BF16 = jnp.bfloat16

RMS_EPS = 1e-6
LN_EPS = 1e-5
GATE_NORM = 16.0
N_HEADS = 4
HEAD_K = 128
HEAD_V = 256
D_K = N_HEADS * HEAD_K
D_A = N_HEADS * HEAD_V
GATE_RANK = 16
CONV_W = 31

LANES = 128
MXU_N = 256
GLA_CHUNK = 128
VMEM_LIMIT = 58 * 1024 * 1024

ZB = 512
Z_QKV, Z_SG, Z_GLU, Z_LA = 0, 2 * D_K + D_A, 2 * D_K + 2 * D_A, 2 * D_K + 3 * D_A
Z_COLS = Z_LA + D_K


def _cparams(sem):
    return pltpu.CompilerParams(dimension_semantics=sem, vmem_limit_bytes=VMEM_LIMIT)


def _dot(a, b):
    return jnp.dot(a, b, preferred_element_type=F32)


def _dot_nt(a, b):
    return lax.dot_general(a, b, (((1,), (1,)), ((), ())), preferred_element_type=F32)


def _rms_rows(x, gain):
    ms = jnp.mean(x * x, axis=-1, keepdims=True)
    return x * lax.rsqrt(ms + RMS_EPS) * gain


def _sigmoid(x):
    return 0.5 * jnp.tanh(0.5 * x) + 0.5


def _log_sigmoid(x):
    return jnp.minimum(x, 0.0) - jnp.log(1.0 + jnp.exp(-jnp.abs(x)))


def _norm_to_scratch(h_ref, xp_ref, xs, gain, tmp, tms, rows_per_pass=256):
    for r0 in range(0, tmp, rows_per_pass):
        h_ref[r0:r0 + rows_per_pass, :] = _rms_rows(xp_ref[r0:r0 + rows_per_pass, :], gain).astype(BF16)
    h_ref[tmp:tmp + tms, :] = _rms_rows(xs, gain).astype(BF16)


def _in_proj_kernel(wt_ref, wnext_ref, xp_ref, xs_ref, g_ref, wup_ref, bg_ref, zp_ref, zs_ref,
                    wb_ref, wlr_ref, h_ref, *, nw, nj, tmp, tms):
    s = pl.program_id(0)
    rb = MXU_N
    n_direct = (2 * D_K + D_A) // rb

    @pl.when(s < n_direct)
    def _():
        wb_ref[s] = wt_ref[...].T.astype(BF16)

    @pl.when((s >= n_direct) & (s < nw))
    def _():
        rows = jnp.concatenate([wt_ref[GATE_RANK:rb, :], wnext_ref[...]], axis=0)
        wb_ref[s] = rows.T.astype(BF16)

    @pl.when(s == n_direct)
    def _():
        wlr_ref[...] = wt_ref[0:LANES, :].T.astype(BF16)

    @pl.when(s >= nw)
    def _project():
        j = lax.rem(s - nw, nj)

        def pair(t0):
            h = h_ref[...]
            return jnp.concatenate([_dot(h, wb_ref[t0]), _dot(h, wb_ref[t0 + 1])], axis=1)

        def put(z):
            zp_ref[...] = z[:tmp]
            zs_ref[...] = z[tmp:]

        nqkv, nsg, nglu = Z_SG // ZB, (Z_GLU - Z_SG) // ZB, (Z_LA - Z_GLU) // ZB
        bpz = ZB // rb

        @pl.when(j == 0)
        def _():
            _norm_to_scratch(h_ref, xp_ref, xs_ref[:, 0, :], g_ref[...], tmp, tms)
            put(pair(0))

        @pl.when((j > 0) & (j < nqkv))
        def _():
            put(pair(bpz * j))

        @pl.when((j >= nqkv) & (j < nqkv + nsg))
        def _():
            z = pair(bpz * j)
            put(z * _sigmoid(z))

        @pl.when((j >= nqkv + nsg) & (j < nqkv + nsg + nglu))
        def _():
            ta = bpz * j
            ua = pair(ta)
            ug = pair(ta + bpz * nglu)
            put(ua * _sigmoid(ug))

        @pl.when(j == nqkv + nsg + nglu)
        def _():
            glr = _dot(h_ref[...], wlr_ref[...])[:, 0:GATE_RANK]
            logit = _dot(glr, wup_ref[...]) + bg_ref[...]
            put(_log_sigmoid(logit) * (1.0 / GATE_NORM))


def _conv_prompt_kernel(glu_ref, cw_ref, cb_ref, lg_ref, lb_ref, c_ref, cnew_ref, slab_ref, y_ref,
                        *, tt, nsteps, hist):
    t_idx = pl.program_id(1)
    nslab = slab_ref.shape[0]
    pad = hist - (CONV_W - 1)

    @pl.when(t_idx == 0)
    def _():
        slab_ref[:, 0:hist, :] = jnp.zeros((nslab, hist, LANES), F32)

    @pl.when(t_idx > 0)
    def _():
        slab_ref[:, 0:hist, :] = slab_ref[:, tt:tt + hist, :]

    for sl in range(nslab):
        slab_ref[sl, hist:hist + tt, :] = glu_ref[:, sl * LANES:(sl + 1) * LANES]

    def slab_conv(sl, carry):
        acc = None
        for res in range(8):
            part = None
            for w in range(res, CONV_W, 8):
                term = slab_ref[sl, pad + w:pad + w + tt, :] * cw_ref[w, pl.ds(sl, 1), :]
                part = term if part is None else part + term
            acc = part if acc is None else acc + part
        y_ref[sl] = acc + cb_ref[pl.ds(sl, 1), :]
        return carry

    lax.fori_loop(0, nslab, slab_conv, 0)

    y = y_ref[...]
    inv_c = 1.0 / (nslab * LANES)
    mu = jnp.sum(jnp.sum(y, axis=0), axis=-1, keepdims=True) * inv_c
    yc = y - mu[None]
    var = jnp.sum(jnp.sum(yc * yc, axis=0), axis=-1, keepdims=True) * inv_c
    rstd = lax.rsqrt(var + LN_EPS)
    for sl in range(nslab):
        cs = slice(sl * LANES, (sl + 1) * LANES)
        z = yc[sl] * rstd * lg_ref[:, cs] + lb_ref[:, cs]
        c_ref[:, cs] = (z * _sigmoid(z)).astype(c_ref.dtype)

    @pl.when(t_idx == nsteps - 1)
    def _():
        for sl in range(nslab):
            cnew_ref[0, 0, :, sl * LANES:(sl + 1) * LANES] = slab_ref[sl, hist + tt - (CONV_W - 1):hist + tt, :]


def _outproj_kernel(w_ref, xp_ref, xs_ref, oap_ref, oas_ref, cp_ref, cs_ref, op_ref, os_ref,
                    wb_ref, a_ref, c_ref, *, nw, nj, tmp, tms):
    s = pl.program_id(0)
    rb = MXU_N
    wpl = w_ref.shape[1] // rb
    bpc = xp_ref.shape[1] // rb

    @pl.when(s < nw)
    def _load_weight():
        for q in range(wpl):
            wb_ref[s * wpl + q] = w_ref[:, q * rb:(q + 1) * rb].astype(BF16)

    @pl.when(s >= nw)
    def _project():
        j = lax.rem(s - nw, nj)

        @pl.when(j == 0)
        def _():
            a_ref[0:tmp, :] = oap_ref[...]
            a_ref[tmp:tmp + tms, :] = oas_ref[...].astype(BF16)
            c_ref[0:tmp, :] = cp_ref[...]
            c_ref[tmp:tmp + tms, :] = cs_ref[...].astype(BF16)

        for half in range(bpc):
            t = bpc * j + half
            cols = slice(half * rb, (half + 1) * rb)
            z = _dot(a_ref[...], wb_ref[t, 0:D_A, :]) + _dot(c_ref[...], wb_ref[t, D_A:, :])
            op_ref[:, cols] = xp_ref[:, cols] + z[:tmp]
            os_ref[:, cols] = xs_ref[:, 0, cols] + z[tmp:]


def _ffn_kernel(xp_ref, xs_ref, g_ref, gf_ref, wg_ref, wu_ref, wo_ref, op_ref, os_ref, h_ref, accs_ref,
                *, tmp, tms, nf):
    j = pl.program_id(1)

    def ffn_block(acc_p, acc_s):
        h = h_ref[...]
        fg = _dot(h, wg_ref[...].astype(BF16))
        fu = _dot(h, wu_ref[...].astype(BF16))
        act = (fg * _sigmoid(fg) * fu).astype(BF16)
        wo = wo_ref[...].astype(BF16)
        half = tmp // 2
        op_ref[0:half, :] = acc_p[0:half, :] + _dot(act[0:half], wo)
        z = _dot(act[half:], wo)
        op_ref[half:tmp, :] = acc_p[half:tmp, :] + z[:tmp - half]
        accs_ref[...] = acc_s[...] + z[tmp - half:]

    @pl.when(j == 0)
    def _():
        _norm_to_scratch(h_ref, xp_ref, xs_ref[...], g_ref[...], tmp, tms)
        ffn_block(xp_ref, xs_ref)

    @pl.when((j > 0) & (j < nf - 1))
    def _():
        ffn_block(op_ref, accs_ref)

    @pl.when(j == nf - 1)
    def _():
        ffn_block(op_ref, accs_ref)
        gf = gf_ref[...]
        for r0 in range(0, tmp, 256):
            op_ref[r0:r0 + 256, :] = _rms_rows(op_ref[r0:r0 + 256, :], gf)
        os_ref[:, 0, :] = _rms_rows(accs_ref[...], gf)


@functools.lru_cache(maxsize=None)
def _gla_constants(L):
    nlev = int(np.log2(L)) + 1
    m = np.zeros((nlev + 2, L, L), np.float32)
    wm = np.zeros((nlev, L, L), np.float32)
    idx = np.arange(L)
    wm[0] = np.eye(L, dtype=np.float32)
    for lev in range(1, nlev):
        half = L >> lev
        pos = idx % (2 * half)
        upper = pos >= half
        r = idx - pos + half - 1
        t = idx[None, :]
        up_rows = (t > r[:, None]) & (t <= idx[:, None]) & upper[:, None]
        lo_rows = (t > idx[:, None]) & (t <= r[:, None]) & (~upper)[:, None]
        m[lev] = (up_rows | lo_rows).astype(np.float32)
        same = (idx[:, None] // (2 * half)) == (idx[None, :] // (2 * half))
        wm[lev] = (same & upper[:, None] & (~upper)[None, :]).astype(np.float32)
    m[nlev] = (idx[None, :] <= idx[:, None]).astype(np.float32)
    m[nlev + 1] = (idx[None, :] > idx[:, None]).astype(np.float32)
    m = m.reshape((nlev + 2) * L, L)
    return nlev, np.concatenate([m, m], axis=1), wm


def _gla_prompt_kernel(qkv_ref, la_ref, sg_ref, mall_ref, wm_ref, gn_ref,
                       o_ref, snew_ref, s_ref, *, nlev, nsteps, chunks_per_step):
    c = pl.program_id(1)

    @pl.when(c == 0)
    def _():
        s_ref[...] = jnp.zeros_like(s_ref)

    L = GLA_CHUNK
    scale = HEAD_K ** -0.5
    gn = gn_ref[...]
    chunks = [slice(sub * L, (sub + 1) * L) for sub in range(chunks_per_step)]
    pairs = [(rows, h) for rows in chunks for h in range(N_HEADS)]

    def q_of(rows, h):
        return qkv_ref[rows, h * HEAD_K:(h + 1) * HEAD_K] * scale

    def k_of(rows, h):
        return qkv_ref[rows, D_K + h * HEAD_K:D_K + (h + 1) * HEAD_K]

    mall2 = mall_ref[...]
    e_all = {}
    for rows in chunks:
        la = la_ref[rows, :]
        la_hi = la.astype(BF16)
        la_lo = (la - la_hi.astype(F32)).astype(BF16)
        e_all[rows.start] = _dot(mall2, jnp.concatenate([la_hi, la_lo], axis=0))

    def scaled(rows, h, lev):
        if lev == 0:
            return q_of(rows, h).astype(BF16), k_of(rows, h).astype(BF16)
        e = jnp.exp(e_all[rows.start][lev * L:(lev + 1) * L, h * HEAD_K:(h + 1) * HEAD_K])
        return (q_of(rows, h) * e).astype(BF16), (k_of(rows, h) * e).astype(BF16)

    assert len(pairs) % 2 == 0
    zero = jnp.zeros((L, HEAD_K), BF16)
    amat = {}
    for lev in range(nlev):
        for pa, pb in zip(pairs[0::2], pairs[1::2]):
            qa, ka = scaled(*pa, lev)
            qb, kb = scaled(*pb, lev)
            lhs = jnp.concatenate([qa, qb], axis=1)
            rhs_t = jnp.concatenate([jnp.concatenate([ka, zero], axis=1),
                                     jnp.concatenate([zero, kb], axis=1)], axis=0)
            p = _dot_nt(lhs, rhs_t)
            for key, blk in (((pa[0].start, pa[1]), p[:, 0:L]), ((pb[0].start, pb[1]), p[:, L:2 * L])):
                term = wm_ref[lev] * blk
                amat[key] = term if lev == 0 else amat[key] + term

    for rows, h in pairs:
        ks = slice(h * HEAD_K, (h + 1) * HEAD_K)
        vs = slice(h * HEAD_V, (h + 1) * HEAD_V)
        q, k, a = q_of(rows, h), k_of(rows, h), amat[rows.start, h]
        vb = qkv_ref[rows, 2 * D_K + h * HEAD_V:2 * D_K + (h + 1) * HEAD_V].astype(BF16)

        eb = e_all[rows.start][nlev * L:(nlev + 1) * L, ks]
        ek = e_all[rows.start][(nlev + 1) * L:(nlev + 2) * L, ks]
        qh = (q * jnp.exp(eb)).astype(BF16)
        kh = k * jnp.exp(ek)
        s = s_ref[h]
        o = _dot(jnp.concatenate([a.astype(BF16), qh], axis=1),
                 jnp.concatenate([vb, s.astype(BF16)], axis=0))

        blast = jnp.broadcast_to(eb[L - 1:L, :], (HEAD_K, HEAD_K))
        acol = jnp.exp(blast.T)
        acol = jnp.concatenate([acol, acol], axis=1)
        s_ref[h] = acol * s + _dot(kh.T.astype(BF16), vb)

        y = _rms_rows(o, gn) * sg_ref[rows, vs]
        o_ref[rows, vs] = y.astype(o_ref.dtype)

    @pl.when(c == nsteps - 1)
    def _():
        snew_ref[0, 0] = s_ref[...]


def _sample_mixers_kernel(la_ref, qkv_ref, sg_ref, st_ref, gn_ref, sc_ref, glu_ref, cw_ref, cb_ref, lg_ref, lb_ref,
                          o_ref, snew_ref, c_ref, cnew_ref, at_ref, *, bblk):
    s_idx = pl.program_id(0)
    _conv_sample_kernel(sc_ref, glu_ref, cw_ref, cb_ref, lg_ref, lb_ref, c_ref, cnew_ref)

    @pl.when(s_idx == 0)
    def _():
        for h in range(N_HEADS):
            at_ref[h] = jnp.exp(la_ref[:, h * HEAD_K:(h + 1) * HEAD_K]).T

    gn = gn_ref[...]
    scale = HEAD_K ** -0.5
    lane = lax.broadcasted_iota(jnp.int32, (HEAD_K, LANES), 1)
    row = lax.broadcasted_iota(jnp.int32, (bblk, HEAD_V), 0)
    for h in range(N_HEADS):
        vs = slice(h * HEAD_V, (h + 1) * HEAD_V)
        q = (qkv_ref[:, h * HEAD_K:(h + 1) * HEAD_K] * scale).astype(BF16)
        kt = qkv_ref[:, D_K + h * HEAD_K:D_K + (h + 1) * HEAD_K].T.astype(BF16)
        v = qkv_ref[:, 2 * D_K + h * HEAD_V:2 * D_K + (h + 1) * HEAD_V].astype(BF16)
        vdiag = jnp.concatenate([jnp.where(row == j, v, jnp.zeros_like(v)) for j in range(bblk)], axis=1)
        kv = _dot(kt, vdiag)
        outs = []
        for j in range(bblk):
            b = s_idx * bblk + j
            acol = jnp.sum(jnp.where(lane == b, at_ref[h], 0.0), axis=1, keepdims=True)
            s_new = acol * st_ref[j, h] + kv[:, j * HEAD_V:(j + 1) * HEAD_V]
            snew_ref[0, j, h] = s_new
            outs.append(_dot(q, s_new.astype(BF16))[j:j + 1, :])
        for j in range(bblk):
            o_ref[j:j + 1, vs] = _rms_rows(outs[j], gn) * sg_ref[j:j + 1, vs]


def _ln_silu(y, g, b):
    mu = jnp.mean(y, axis=-1, keepdims=True)
    yc = y - mu
    var = jnp.mean(yc * yc, axis=-1, keepdims=True)
    z = yc * lax.rsqrt(var + LN_EPS) * g + b
    return z * _sigmoid(z)


def _conv_sample_kernel(sc_ref, glu_ref, cw_ref, cb_ref, lg_ref, lb_ref, c_ref, cnew_ref):
    nh = CONV_W - 1
    g = glu_ref[...]
    ys = []
    for sl in range(cw_ref.shape[1]):
        ls = slice(sl * LANES, (sl + 1) * LANES)
        y = g[:, ls] * cw_ref[nh, sl:sl + 1, :] + cb_ref[sl:sl + 1, :]
        for w in range(nh):
            y = y + sc_ref[w, :, ls] * cw_ref[w, sl:sl + 1, :]
        ys.append(y)
    c_ref[...] = _ln_silu(jnp.concatenate(ys, axis=1), lg_ref[...], lb_ref[...])
    for w in range(nh - 1):
        cnew_ref[w] = sc_ref[w + 1]
    cnew_ref[nh - 1] = g


def _fixed(shape, idx=None):
    idx = (0,) * len(shape) if idx is None else idx
    return pl.BlockSpec(shape, lambda *_: idx)


def kernel(x_prompt, x_sample, state_gla, state_conv, norm_mix, w_in, w_gate_up, b_gate, gla_norm,
           conv_w, conv_b, conv_ln_g, conv_ln_b, w_out, norm_ffn, w_ffn_in, w_ffn_out, norm_final):
    B, T, D = x_prompt.shape
    BS = x_sample.shape[0]
    assert state_gla.shape[0] == 1 and x_sample.shape[1] == 1
    d_conv = state_conv.shape[-1]
    d_ff = w_ffn_out.shape[1]
    in_cols = w_in.shape[-1]
    assert d_conv == D_A and in_cols == 2 * D_K + D_A + GATE_RANK + D_A + 2 * d_conv
    MP = B * T
    n_tiles = 8
    tmp, tms = MP // n_tiles, BS // n_tiles
    assert tmp * n_tiles == MP and tms * n_tiles == BS and tms % 8 == 0 and tmp % 256 == 0
    tm = tmp + tms

    xp = x_prompt.reshape(MP, D)

    def pair_shapes(cols):
        return (jax.ShapeDtypeStruct((MP, cols), F32), jax.ShapeDtypeStruct((BS, cols), F32))

    wt = jnp.swapaxes(w_in, 1, 2)[0]
    nw = (in_cols - GATE_RANK) // MXU_N
    assert nw * MXU_N == in_cols - GATE_RANK and MXU_N % GATE_RANK == 0
    nj = Z_COLS // ZB
    row_i = lambda s: jnp.maximum(s - nw, 0) // nj
    col_j = lambda s: lax.rem(jnp.maximum(s - nw, 0), nj)
    wblk = lambda s: jnp.minimum(s, nw - 1)
    z_p, z_s = pl.pallas_call(
        functools.partial(_in_proj_kernel, nw=nw, nj=nj, tmp=tmp, tms=tms),
        grid=(nw + n_tiles * nj,),
        in_specs=[pl.BlockSpec((MXU_N, D), lambda s: (wblk(s), 0)),
                  pl.BlockSpec((GATE_RANK, D), lambda s: ((wblk(s) + 1) * (MXU_N // GATE_RANK), 0)),
                  pl.BlockSpec((tmp, D), lambda s: (row_i(s), 0)),
                  pl.BlockSpec((tms, 1, D), lambda s: (row_i(s), 0, 0)),
                  _fixed((1, D)), _fixed((GATE_RANK, D_K)), _fixed((1, D_K))],
        out_specs=(pl.BlockSpec((tmp, ZB), lambda s: (row_i(s), col_j(s))),
                   pl.BlockSpec((tms, ZB), lambda s: (row_i(s), col_j(s)))),
        out_shape=pair_shapes(Z_COLS),
        scratch_shapes=[pltpu.VMEM((nw, D, MXU_N), BF16), pltpu.VMEM((D, LANES), BF16),
                        pltpu.VMEM((tm, D), BF16)],
        compiler_params=_cparams(("arbitrary",)),
        name="in_proj",
    )(wt, wt, xp, x_sample, norm_mix[0].reshape(1, D), w_gate_up[0], b_gate[0].reshape(1, D_K))

    chunks_per_step = 4
    L = GLA_CHUNK * chunks_per_step
    nchunks = T // L
    assert nchunks * L == T
    nlev, mall, wm = _gla_constants(GLA_CHUNK)
    gn = gla_norm[0].reshape(1, HEAD_V)
    zrow = lambda b, c: b * nchunks + c
    oa_p, gla_new_p = pl.pallas_call(
        functools.partial(_gla_prompt_kernel, nlev=nlev, nsteps=nchunks, chunks_per_step=chunks_per_step),
        grid=(B, nchunks),
        in_specs=[pl.BlockSpec((L, Z_SG), lambda b, c: (zrow(b, c), 0)),
                  pl.BlockSpec((L, D_K), lambda b, c: (zrow(b, c), Z_LA // D_K)),
                  pl.BlockSpec((L, D_A), lambda b, c: (zrow(b, c), Z_SG // D_A)),
                  _fixed(mall.shape), _fixed(wm.shape),
                  _fixed((1, HEAD_V))],
        out_specs=(pl.BlockSpec((L, D_A), lambda b, c: (zrow(b, c), 0)),
                   pl.BlockSpec((1, 1, N_HEADS, HEAD_K, HEAD_V), lambda b, c: (0, b, 0, 0, 0))),
        out_shape=(jax.ShapeDtypeStruct((MP, D_A), BF16),
                   jax.ShapeDtypeStruct((1, B, N_HEADS, HEAD_K, HEAD_V), F32)),
        scratch_shapes=[pltpu.VMEM((N_HEADS, HEAD_K, HEAD_V), F32)],
        compiler_params=_cparams(("arbitrary", "arbitrary")),
        name="gla_prompt",
    )(z_p, z_p, z_p, jnp.asarray(mall, BF16), jnp.asarray(wm), gn)

    nslab = d_conv // LANES
    vec = lambda a: a.reshape(1, d_conv)
    cw_slabs = conv_w[0].reshape(CONV_W, nslab, LANES)
    cb_slabs = conv_b[0].reshape(nslab, LANES)

    bblk = 16
    assert BS == LANES and BS % bblk == 0
    sc_t = jnp.transpose(state_conv[0], (1, 0, 2))
    oa_s, gla_new_s, c_s, conv_new_s_t = pl.pallas_call(
        functools.partial(_sample_mixers_kernel, bblk=bblk),
        grid=(BS // bblk,),
        in_specs=[_fixed((BS, D_K), (0, Z_LA // D_K)),
                  pl.BlockSpec((bblk, Z_SG), lambda s: (s, 0)),
                  pl.BlockSpec((bblk, D_A), lambda s: (s, Z_SG // D_A)),
                  pl.BlockSpec((bblk, N_HEADS, HEAD_K, HEAD_V), lambda s: (s, 0, 0, 0)),
                  _fixed((1, HEAD_V)),
                  pl.BlockSpec((CONV_W - 1, bblk, d_conv), lambda s: (0, s, 0)),
                  pl.BlockSpec((bblk, d_conv), lambda s: (s, Z_GLU // d_conv)),
                  _fixed((CONV_W, nslab, LANES)), _fixed((nslab, LANES)), _fixed((1, d_conv)), _fixed((1, d_conv))],
        out_specs=(pl.BlockSpec((bblk, D_A), lambda s: (s, 0)),
                   pl.BlockSpec((1, bblk, N_HEADS, HEAD_K, HEAD_V), lambda s: (0, s, 0, 0, 0)),
                   pl.BlockSpec((bblk, d_conv), lambda s: (s, 0)),
                   pl.BlockSpec((CONV_W - 1, bblk, d_conv), lambda s: (0, s, 0))),
        out_shape=(jax.ShapeDtypeStruct((BS, D_A), F32),
                   jax.ShapeDtypeStruct((1, BS, N_HEADS, HEAD_K, HEAD_V), F32),
                   jax.ShapeDtypeStruct((BS, d_conv), F32),
                   jax.ShapeDtypeStruct((CONV_W - 1, BS, d_conv), F32)),
        scratch_shapes=[pltpu.VMEM((N_HEADS, HEAD_K, BS), F32)],
        compiler_params=_cparams(("arbitrary",)),
        name="sample_mixers",
    )(z_s, z_s, z_s, state_gla[0], gn,
      sc_t, z_s, cw_slabs, cb_slabs, vec(conv_ln_g[0]), vec(conv_ln_b[0]))
    conv_new_s = jnp.transpose(conv_new_s_t, (1, 0, 2))[None]

    tt = 512
    hist = 32
    nsteps = T // tt
    c_p, conv_new_p = pl.pallas_call(
        functools.partial(_conv_prompt_kernel, tt=tt, nsteps=nsteps, hist=hist),
        grid=(B, nsteps),
        in_specs=[pl.BlockSpec((tt, d_conv), lambda b, t: (b * nsteps + t, Z_GLU // d_conv)),
                  _fixed((CONV_W, nslab, LANES)), _fixed((nslab, LANES)),
                  _fixed((1, d_conv)), _fixed((1, d_conv))],
        out_specs=(pl.BlockSpec((tt, d_conv), lambda b, t: (b * nsteps + t, 0)),
                   pl.BlockSpec((1, 1, CONV_W - 1, d_conv), lambda b, t: (0, b, 0, 0))),
        out_shape=(jax.ShapeDtypeStruct((MP, d_conv), BF16),
                   jax.ShapeDtypeStruct((1, B, CONV_W - 1, d_conv), F32)),
        scratch_shapes=[pltpu.VMEM((nslab, hist + tt, LANES), F32), pltpu.VMEM((nslab, tt, LANES), F32)],
        compiler_params=_cparams(("arbitrary", "arbitrary")),
        name="conv_prompt",
    )(z_p, cw_slabs, cb_slabs, vec(conv_ln_g[0]), vec(conv_ln_b[0]))

    zbo = min(1024, D)
    wcols = min(2 * MXU_N, D)
    nwo = D // wcols
    njo = D // zbo
    assert nwo * wcols == D and njo * zbo == D
    row_o = lambda s: jnp.maximum(s - nwo, 0) // njo
    col_o = lambda s: lax.rem(jnp.maximum(s - nwo, 0), njo)
    x1_p, x1_s = pl.pallas_call(
        functools.partial(_outproj_kernel, nw=nwo, nj=njo, tmp=tmp, tms=tms),
        grid=(nwo + n_tiles * njo,),
        in_specs=[pl.BlockSpec((D_A + d_conv, wcols), lambda s: (0, jnp.minimum(s, nwo - 1))),
                  pl.BlockSpec((tmp, zbo), lambda s: (row_o(s), col_o(s))),
                  pl.BlockSpec((tms, 1, zbo), lambda s: (row_o(s), 0, col_o(s))),
                  pl.BlockSpec((tmp, D_A), lambda s: (row_o(s), 0)),
                  pl.BlockSpec((tms, D_A), lambda s: (row_o(s), 0)),
                  pl.BlockSpec((tmp, d_conv), lambda s: (row_o(s), 0)),
                  pl.BlockSpec((tms, d_conv), lambda s: (row_o(s), 0))],
        out_specs=(pl.BlockSpec((tmp, zbo), lambda s: (row_o(s), col_o(s))),
                   pl.BlockSpec((tms, zbo), lambda s: (row_o(s), col_o(s)))),
        out_shape=pair_shapes(D),
        scratch_shapes=[pltpu.VMEM((D // MXU_N, D_A + d_conv, MXU_N), BF16),
                        pltpu.VMEM((tm, D_A), BF16), pltpu.VMEM((tm, d_conv), BF16)],
        compiler_params=_cparams(("arbitrary",)),
        name="out_proj",
    )(w_out[0], xp, x_sample, oa_p, oa_s, c_p, c_s)

    tf = 256
    nf = d_ff // tf
    assert nf * tf == d_ff
    y_p, y_s = pl.pallas_call(
        functools.partial(_ffn_kernel, tmp=tmp, tms=tms, nf=nf),
        grid=(n_tiles, nf),
        in_specs=[pl.BlockSpec((tmp, D), lambda i, j: (i, 0)), pl.BlockSpec((tms, D), lambda i, j: (i, 0)),
                  _fixed((1, D)), _fixed((1, D)),
                  pl.BlockSpec((D, tf), lambda i, j: (0, j)),
                  pl.BlockSpec((D, tf), lambda i, j: (0, j + nf)),
                  pl.BlockSpec((tf, D), lambda i, j: (j, 0))],
        out_specs=(pl.BlockSpec((tmp, D), lambda i, j: (i, 0)),
                   pl.BlockSpec((tms, 1, D), lambda i, j: (i, 0, 0))),
        out_shape=(jax.ShapeDtypeStruct((MP, D), F32), jax.ShapeDtypeStruct((BS, 1, D), F32)),
        scratch_shapes=[pltpu.VMEM((tm, D), BF16), pltpu.VMEM((tms, D), F32)],
        compiler_params=_cparams(("arbitrary", "arbitrary")),
        name="ffn",
    )(x1_p, x1_s, norm_ffn[0].reshape(1, D), norm_final.reshape(1, D), w_ffn_in[0], w_ffn_in[0], w_ffn_out[0])

    return (y_p.reshape(B, T, D), y_s, gla_new_p, conv_new_p, gla_new_s, conv_new_s)
```

```python
import functools

import numpy as np
import jax
import jax.numpy as jnp
from jax import lax
from jax.experimental import pallas as pl
from jax.experimental.pallas import tpu as pltpu

F32 = jnp.float32
BF16 = jnp.bfloat16

RMS_EPS = 1e-6
LN_EPS = 1e-5
GATE_NORM = 16.0
N_HEADS = 4
HEAD_K = 128
HEAD_V = 256
D_K = N_HEADS * HEAD_K
D_A = N_HEADS * HEAD_V
GATE_RANK = 16
CONV_W = 31

LANES = 128
MXU_N = 256
GLA_CHUNK = 128
VMEM_LIMIT = 58 * 1024 * 1024

ZB = 512
Z_QKV, Z_SG, Z_GLU, Z_LA = 0, 2 * D_K + D_A, 2 * D_K + 2 * D_A, 2 * D_K + 3 * D_A
Z_COLS = Z_LA + D_K


def _cparams(sem):
    return pltpu.CompilerParams(dimension_semantics=sem, vmem_limit_bytes=VMEM_LIMIT)


def _dot(a, b):
    return jnp.dot(a, b, preferred_element_type=F32)


def _dot_nt(a, b):
    return lax.dot_general(a, b, (((1,), (1,)), ((), ())), preferred_element_type=F32)


def _rms_rows(x, gain):
    ms = jnp.mean(x * x, axis=-1, keepdims=True)
    return x * lax.rsqrt(ms + RMS_EPS) * gain


def _sigmoid(x):
    return 0.5 * jnp.tanh(0.5 * x) + 0.5


def _silu(x):
    t = 0.5 * x
    return t * jnp.tanh(t) + t


def _log_sigmoid(x):
    return jnp.minimum(x, 0.0) - jnp.log(1.0 + jnp.exp(-jnp.abs(x)))


def _norm_to_scratch(h_ref, xp_ref, xs, gain, tmp, tms, rows_per_pass=256):
    for r0 in range(0, tmp, rows_per_pass):
        h_ref[r0:r0 + rows_per_pass, :] = _rms_rows(xp_ref[r0:r0 + rows_per_pass, :], gain).astype(BF16)
    h_ref[tmp:tmp + tms, :] = _rms_rows(xs, gain).astype(BF16)


def _in_proj_kernel(wt_ref, wnext_ref, xp_ref, xs_ref, g_ref, wup_ref, bg_ref, zp_ref, zs_ref,
                    wb_ref, wlr_ref, h_ref, *, nw, nj, tmp, tms):
    s = pl.program_id(0)
    rb = MXU_N
    n_direct = (2 * D_K + D_A) // rb

    @pl.when(s < n_direct)
    def _():
        wb_ref[s] = wt_ref[...].T.astype(BF16)

    @pl.when((s >= n_direct) & (s < nw))
    def _():
        rows = jnp.concatenate([wt_ref[GATE_RANK:rb, :], wnext_ref[...]], axis=0)
        wb_ref[s] = rows.T.astype(BF16)

    @pl.when(s == n_direct)
    def _():
        wlr_ref[...] = wt_ref[0:LANES, :].T.astype(BF16)

    @pl.when(s >= nw)
    def _project():
        j = lax.rem(s - nw, nj)

        def pair(t0):
            h = h_ref[...]
            return jnp.concatenate([_dot(h, wb_ref[t0]), _dot(h, wb_ref[t0 + 1])], axis=1)

        def put(z):
            zp_ref[...] = z[:tmp]
            zs_ref[...] = z[tmp:]

        nqkv, nsg, nglu = Z_SG // ZB, (Z_GLU - Z_SG) // ZB, (Z_LA - Z_GLU) // ZB
        bpz = ZB // rb

        @pl.when(j == 0)
        def _():
            _norm_to_scratch(h_ref, xp_ref, xs_ref[:, 0, :], g_ref[...], tmp, tms)
            put(pair(0))

        @pl.when((j > 0) & (j < nqkv))
        def _():
            put(pair(bpz * j))

        @pl.when((j >= nqkv) & (j < nqkv + nsg))
        def _():
            z = pair(bpz * j)
            put(_silu(z))

        @pl.when((j >= nqkv + nsg) & (j < nqkv + nsg + nglu))
        def _():
            ta = bpz * j
            ua = pair(ta)
            ug = pair(ta + bpz * nglu)
            put(ua * _sigmoid(ug))

        @pl.when(j == nqkv + nsg + nglu)
        def _():
            glr = _dot(h_ref[...], wlr_ref[...])[:, 0:GATE_RANK]
            logit = _dot(glr, wup_ref[...]) + bg_ref[...]
            put(_log_sigmoid(logit) * (1.0 / GATE_NORM))


def _conv_prompt_kernel(glu_ref, cw_ref, cb_ref, lg_ref, lb_ref, c_ref, cnew_ref, slab_ref, y_ref,
                        *, tt, nsteps, hist):
    t_idx = pl.program_id(1)
    nslab = slab_ref.shape[0]
    pad = hist - (CONV_W - 1)

    @pl.when(t_idx == 0)
    def _():
        slab_ref[:, 0:hist, :] = jnp.zeros((nslab, hist, LANES), F32)

    @pl.when(t_idx > 0)
    def _():
        slab_ref[:, 0:hist, :] = slab_ref[:, tt:tt + hist, :]

    for sl in range(nslab):
        slab_ref[sl, hist:hist + tt, :] = glu_ref[:, sl * LANES:(sl + 1) * LANES]

    def slab_conv(sl, carry):
        acc = None
        for res in range(8):
            part = None
            for w in range(res, CONV_W, 8):
                term = slab_ref[sl, pad + w:pad + w + tt, :] * cw_ref[w, pl.ds(sl, 1), :]
                part = term if part is None else part + term
            acc = part if acc is None else acc + part
        y_ref[sl] = acc + cb_ref[pl.ds(sl, 1), :]
        return carry

    lax.fori_loop(0, nslab, slab_conv, 0)

    y = y_ref[...]
    inv_c = 1.0 / (nslab * LANES)
    mu = jnp.sum(jnp.sum(y, axis=0), axis=-1, keepdims=True) * inv_c
    yc = y - mu[None]
    var = jnp.sum(jnp.sum(yc * yc, axis=0), axis=-1, keepdims=True) * inv_c
    rstd = lax.rsqrt(var + LN_EPS)
    for sl in range(nslab):
        cs = slice(sl * LANES, (sl + 1) * LANES)
        z = yc[sl] * rstd * lg_ref[:, cs] + lb_ref[:, cs]
        c_ref[:, cs] = _silu(z).astype(c_ref.dtype)

    @pl.when(t_idx == nsteps - 1)
    def _():
        for sl in range(nslab):
            cnew_ref[0, 0, :, sl * LANES:(sl + 1) * LANES] = slab_ref[sl, hist + tt - (CONV_W - 1):hist + tt, :]


def _outproj_kernel(w_ref, xp_ref, xs_ref, oap_ref, oas_ref, cp_ref, cs_ref, op_ref, os_ref,
                    wb_ref, a_ref, c_ref, *, nw, nj, tmp, tms):
    s = pl.program_id(0)
    rb = MXU_N
    wpl = w_ref.shape[1] // rb
    bpc = xp_ref.shape[1] // rb

    @pl.when(s < nw)
    def _load_weight():
        for q in range(wpl):
            wb_ref[s * wpl + q] = w_ref[:, q * rb:(q + 1) * rb].astype(BF16)

    @pl.when(s >= nw)
    def _project():
        j = lax.rem(s - nw, nj)

        @pl.when(j == 0)
        def _():
            a_ref[0:tmp, :] = oap_ref[...]
            a_ref[tmp:tmp + tms, :] = oas_ref[...].astype(BF16)
            c_ref[0:tmp, :] = cp_ref[...]
            c_ref[tmp:tmp + tms, :] = cs_ref[...].astype(BF16)

        for half in range(bpc):
            t = bpc * j + half
            cols = slice(half * rb, (half + 1) * rb)
            z = _dot(a_ref[...], wb_ref[t, 0:D_A, :]) + _dot(c_ref[...], wb_ref[t, D_A:, :])
            op_ref[:, cols] = xp_ref[:, cols] + z[:tmp]
            os_ref[:, cols] = xs_ref[:, 0, cols] + z[tmp:]


def _ffn_kernel(xp_ref, xs_ref, g_ref, gf_ref, wg_ref, wu_ref, wo_ref, op_ref, os_ref, h_ref, accs_ref,
                *, tmp, tms, nf):
    j = pl.program_id(1)

    def ffn_block(acc_p, acc_s):
        h = h_ref[...]
        fg = _dot(h, wg_ref[...].astype(BF16))
        fu = _dot(h, wu_ref[...].astype(BF16))
        act = (_silu(fg) * fu).astype(BF16)
        wo = wo_ref[...].astype(BF16)
        half = tmp // 2
        op_ref[0:half, :] = acc_p[0:half, :] + _dot(act[0:half], wo)
        z = _dot(act[half:], wo)
        op_ref[half:tmp, :] = acc_p[half:tmp, :] + z[:tmp - half]
        accs_ref[...] = acc_s[...] + z[tmp - half:]

    @pl.when(j == 0)
    def _():
        _norm_to_scratch(h_ref, xp_ref, xs_ref[...], g_ref[...], tmp, tms)
        ffn_block(xp_ref, xs_ref)

    @pl.when((j > 0) & (j < nf - 1))
    def _():
        ffn_block(op_ref, accs_ref)

    @pl.when(j == nf - 1)
    def _():
        ffn_block(op_ref, accs_ref)
        gf = gf_ref[...]
        for r0 in range(0, tmp, 256):
            op_ref[r0:r0 + 256, :] = _rms_rows(op_ref[r0:r0 + 256, :], gf)
        os_ref[:, 0, :] = _rms_rows(accs_ref[...], gf)


@functools.lru_cache(maxsize=None)
def _gla_constants(L):
    nlev = int(np.log2(L)) + 1
    m = np.zeros((nlev + 2, L, L), np.float32)
    wm = np.zeros((nlev, L, L), np.float32)
    idx = np.arange(L)
    wm[0] = np.eye(L, dtype=np.float32)
    for lev in range(1, nlev):
        half = L >> lev
        pos = idx % (2 * half)
        upper = pos >= half
        r = idx - pos + half - 1
        t = idx[None, :]
        up_rows = (t > r[:, None]) & (t <= idx[:, None]) & upper[:, None]
        lo_rows = (t > idx[:, None]) & (t <= r[:, None]) & (~upper)[:, None]
        m[lev] = (up_rows | lo_rows).astype(np.float32)
        same = (idx[:, None] // (2 * half)) == (idx[None, :] // (2 * half))
        wm[lev] = (same & upper[:, None] & (~upper)[None, :]).astype(np.float32)
    m[nlev] = (idx[None, :] <= idx[:, None]).astype(np.float32)
    m[nlev + 1] = (idx[None, :] > idx[:, None]).astype(np.float32)
    m = m.reshape((nlev + 2) * L, L)
    return nlev, np.concatenate([m, m], axis=1), wm


def _gla_prompt_kernel(qkv_ref, la_ref, sg_ref, mall_ref, wm_ref, gn_ref,
                       o_ref, snew_ref, s_ref, *, nlev, nsteps, chunks_per_step):
    c = pl.program_id(1)

    @pl.when(c == 0)
    def _():
        s_ref[...] = jnp.zeros_like(s_ref)

    L = GLA_CHUNK
    scale = HEAD_K ** -0.5
    gn = gn_ref[...]
    chunks = [slice(sub * L, (sub + 1) * L) for sub in range(chunks_per_step)]
    pairs = [(rows, h) for rows in chunks for h in range(N_HEADS)]

    def q_of(rows, h):
        return qkv_ref[rows, h * HEAD_K:(h + 1) * HEAD_K] * scale

    def k_of(rows, h):
        return qkv_ref[rows, D_K + h * HEAD_K:D_K + (h + 1) * HEAD_K]

    mall2 = mall_ref[...]
    e_all = {}
    for rows in chunks:
        la = la_ref[rows, :]
        la_hi = la.astype(BF16)
        la_lo = (la - la_hi.astype(F32)).astype(BF16)
        e_all[rows.start] = _dot(mall2, jnp.concatenate([la_hi, la_lo], axis=0))

    def scaled(rows, h, lev):
        if lev == 0:
            return q_of(rows, h).astype(BF16), k_of(rows, h).astype(BF16)
        e = jnp.exp(e_all[rows.start][lev * L:(lev + 1) * L, h * HEAD_K:(h + 1) * HEAD_K])
        return (q_of(rows, h) * e).astype(BF16), (k_of(rows, h) * e).astype(BF16)

    assert len(pairs) % 2 == 0
    zero = jnp.zeros((L, HEAD_K), BF16)
    amat = {}
    for lev in range(nlev):
        for pa, pb in zip(pairs[0::2], pairs[1::2]):
            qa, ka = scaled(*pa, lev)
            qb, kb = scaled(*pb, lev)
            lhs = jnp.concatenate([qa, qb], axis=1)
            rhs_t = jnp.concatenate([jnp.concatenate([ka, zero], axis=1),
                                     jnp.concatenate([zero, kb], axis=1)], axis=0)
            p = _dot_nt(lhs, rhs_t)
            for key, blk in (((pa[0].start, pa[1]), p[:, 0:L]), ((pb[0].start, pb[1]), p[:, L:2 * L])):
                term = wm_ref[lev] * blk
                amat[key] = term if lev == 0 else amat[key] + term

    for rows, h in pairs:
        ks = slice(h * HEAD_K, (h + 1) * HEAD_K)
        vs = slice(h * HEAD_V, (h + 1) * HEAD_V)
        q, k, a = q_of(rows, h), k_of(rows, h), amat[rows.start, h]
        vb = qkv_ref[rows, 2 * D_K + h * HEAD_V:2 * D_K + (h + 1) * HEAD_V].astype(BF16)

        eb = e_all[rows.start][nlev * L:(nlev + 1) * L, ks]
        ek = e_all[rows.start][(nlev + 1) * L:(nlev + 2) * L, ks]
        qh = (q * jnp.exp(eb)).astype(BF16)
        kh = k * jnp.exp(ek)
        s = s_ref[h]
        o = _dot(jnp.concatenate([a.astype(BF16), qh], axis=1),
                 jnp.concatenate([vb, s.astype(BF16)], axis=0))

        blast = jnp.broadcast_to(eb[L - 1:L, :], (HEAD_K, HEAD_K))
        acol = jnp.exp(blast.T)
        acol = jnp.concatenate([acol, acol], axis=1)
        s_ref[h] = acol * s + _dot(kh.T.astype(BF16), vb)

        y = _rms_rows(o, gn) * sg_ref[rows, vs]
        o_ref[rows, vs] = y.astype(o_ref.dtype)

    @pl.when(c == nsteps - 1)
    def _():
        snew_ref[0, 0] = s_ref[...]


def _sample_mixers_kernel(la_ref, qkv_ref, sg_ref, st_ref, gn_ref, sc_ref, glu_ref, cw_ref, cb_ref, lg_ref, lb_ref,
                          o_ref, snew_ref, c_ref, cnew_ref, at_ref, *, bblk):
    s_idx = pl.program_id(0)
    _conv_sample_kernel(sc_ref, glu_ref, cw_ref, cb_ref, lg_ref, lb_ref, c_ref, cnew_ref)

    @pl.when(s_idx == 0)
    def _():
        for h in range(N_HEADS):
            at_ref[h] = jnp.exp(la_ref[:, h * HEAD_K:(h + 1) * HEAD_K]).T

    gn = gn_ref[...]
    scale = HEAD_K ** -0.5
    lane = lax.broadcasted_iota(jnp.int32, (HEAD_K, LANES), 1)
    row = lax.broadcasted_iota(jnp.int32, (bblk, HEAD_V), 0)
    for h in range(N_HEADS):
        vs = slice(h * HEAD_V, (h + 1) * HEAD_V)
        q = (qkv_ref[:, h * HEAD_K:(h + 1) * HEAD_K] * scale).astype(BF16)
        kt = qkv_ref[:, D_K + h * HEAD_K:D_K + (h + 1) * HEAD_K].T.astype(BF16)
        v = qkv_ref[:, 2 * D_K + h * HEAD_V:2 * D_K + (h + 1) * HEAD_V].astype(BF16)
        vdiag = jnp.concatenate([jnp.where(row == j, v, jnp.zeros_like(v)) for j in range(bblk)], axis=1)
        kv = _dot(kt, vdiag)
        outs = []
        for j in range(bblk):
            b = s_idx * bblk + j
            acol = jnp.sum(jnp.where(lane == b, at_ref[h], 0.0), axis=1, keepdims=True)
            s_new = acol * st_ref[j, h] + kv[:, j * HEAD_V:(j + 1) * HEAD_V]
            snew_ref[0, j, h] = s_new
            outs.append(_dot(q, s_new.astype(BF16))[j:j + 1, :])
        for j in range(bblk):
            o_ref[j:j + 1, vs] = _rms_rows(outs[j], gn) * sg_ref[j:j + 1, vs]


def _ln_silu(y, g, b):
    mu = jnp.mean(y, axis=-1, keepdims=True)
    yc = y - mu
    var = jnp.mean(yc * yc, axis=-1, keepdims=True)
    z = yc * lax.rsqrt(var + LN_EPS) * g + b
    return _silu(z)


def _conv_sample_kernel(sc_ref, glu_ref, cw_ref, cb_ref, lg_ref, lb_ref, c_ref, cnew_ref):
    nh = CONV_W - 1
    g = glu_ref[...]
    ys = []
    for sl in range(cw_ref.shape[1]):
        ls = slice(sl * LANES, (sl + 1) * LANES)
        y = g[:, ls] * cw_ref[nh, sl:sl + 1, :] + cb_ref[sl:sl + 1, :]
        for w in range(nh):
            y = y + sc_ref[w, :, ls] * cw_ref[w, sl:sl + 1, :]
        ys.append(y)
    c_ref[...] = _ln_silu(jnp.concatenate(ys, axis=1), lg_ref[...], lb_ref[...])
    for w in range(nh - 1):
        cnew_ref[w] = sc_ref[w + 1]
    cnew_ref[nh - 1] = g


def _fixed(shape, idx=None):
    idx = (0,) * len(shape) if idx is None else idx
    return pl.BlockSpec(shape, lambda *_: idx)


def kernel(x_prompt, x_sample, state_gla, state_conv, norm_mix, w_in, w_gate_up, b_gate, gla_norm,
           conv_w, conv_b, conv_ln_g, conv_ln_b, w_out, norm_ffn, w_ffn_in, w_ffn_out, norm_final):
    B, T, D = x_prompt.shape
    BS = x_sample.shape[0]
    assert state_gla.shape[0] == 1 and x_sample.shape[1] == 1
    d_conv = state_conv.shape[-1]
    d_ff = w_ffn_out.shape[1]
    in_cols = w_in.shape[-1]
    assert d_conv == D_A and in_cols == 2 * D_K + D_A + GATE_RANK + D_A + 2 * d_conv
    MP = B * T
    n_tiles = 8
    tmp, tms = MP // n_tiles, BS // n_tiles
    assert tmp * n_tiles == MP and tms * n_tiles == BS and tms % 8 == 0 and tmp % 256 == 0
    tm = tmp + tms

    xp = x_prompt.reshape(MP, D)

    def pair_shapes(cols):
        return (jax.ShapeDtypeStruct((MP, cols), F32), jax.ShapeDtypeStruct((BS, cols), F32))

    wt = jnp.swapaxes(w_in, 1, 2)[0]
    nw = (in_cols - GATE_RANK) // MXU_N
    assert nw * MXU_N == in_cols - GATE_RANK and MXU_N % GATE_RANK == 0
    nj = Z_COLS // ZB
    row_i = lambda s: jnp.maximum(s - nw, 0) // nj
    col_j = lambda s: lax.rem(jnp.maximum(s - nw, 0), nj)
    wblk = lambda s: jnp.minimum(s, nw - 1)
    z_p, z_s = pl.pallas_call(
        functools.partial(_in_proj_kernel, nw=nw, nj=nj, tmp=tmp, tms=tms),
        grid=(nw + n_tiles * nj,),
        in_specs=[pl.BlockSpec((MXU_N, D), lambda s: (wblk(s), 0)),
                  pl.BlockSpec((GATE_RANK, D), lambda s: ((wblk(s) + 1) * (MXU_N // GATE_RANK), 0)),
                  pl.BlockSpec((tmp, D), lambda s: (row_i(s), 0)),
                  pl.BlockSpec((tms, 1, D), lambda s: (row_i(s), 0, 0)),
                  _fixed((1, D)), _fixed((GATE_RANK, D_K)), _fixed((1, D_K))],
        out_specs=(pl.BlockSpec((tmp, ZB), lambda s: (row_i(s), col_j(s))),
                   pl.BlockSpec((tms, ZB), lambda s: (row_i(s), col_j(s)))),
        out_shape=pair_shapes(Z_COLS),
        scratch_shapes=[pltpu.VMEM((nw, D, MXU_N), BF16), pltpu.VMEM((D, LANES), BF16),
                        pltpu.VMEM((tm, D), BF16)],
        compiler_params=_cparams(("arbitrary",)),
        name="in_proj",
    )(wt, wt, xp, x_sample, norm_mix[0].reshape(1, D), w_gate_up[0], b_gate[0].reshape(1, D_K))

    chunks_per_step = 4
    L = GLA_CHUNK * chunks_per_step
    nchunks = T // L
    assert nchunks * L == T
    nlev, mall, wm = _gla_constants(GLA_CHUNK)
    gn = gla_norm[0].reshape(1, HEAD_V)
    zrow = lambda b, c: b * nchunks + c
    oa_p, gla_new_p = pl.pallas_call(
        functools.partial(_gla_prompt_kernel, nlev=nlev, nsteps=nchunks, chunks_per_step=chunks_per_step),
        grid=(B, nchunks),
        in_specs=[pl.BlockSpec((L, Z_SG), lambda b, c: (zrow(b, c), 0)),
                  pl.BlockSpec((L, D_K), lambda b, c: (zrow(b, c), Z_LA // D_K)),
                  pl.BlockSpec((L, D_A), lambda b, c: (zrow(b, c), Z_SG // D_A)),
                  _fixed(mall.shape), _fixed(wm.shape),
                  _fixed((1, HEAD_V))],
        out_specs=(pl.BlockSpec((L, D_A), lambda b, c: (zrow(b, c), 0)),
                   pl.BlockSpec((1, 1, N_HEADS, HEAD_K, HEAD_V), lambda b, c: (0, b, 0, 0, 0))),
        out_shape=(jax.ShapeDtypeStruct((MP, D_A), BF16),
                   jax.ShapeDtypeStruct((1, B, N_HEADS, HEAD_K, HEAD_V), F32)),
        scratch_shapes=[pltpu.VMEM((N_HEADS, HEAD_K, HEAD_V), F32)],
        compiler_params=_cparams(("arbitrary", "arbitrary")),
        name="gla_prompt",
    )(z_p, z_p, z_p, jnp.asarray(mall, BF16), jnp.asarray(wm), gn)

    nslab = d_conv // LANES
    vec = lambda a: a.reshape(1, d_conv)
    cw_slabs = conv_w[0].reshape(CONV_W, nslab, LANES)
    cb_slabs = conv_b[0].reshape(nslab, LANES)

    bblk = 16
    assert BS == LANES and BS % bblk == 0
    sc_t = jnp.transpose(state_conv[0], (1, 0, 2))
    oa_s, gla_new_s, c_s, conv_new_s_t = pl.pallas_call(
        functools.partial(_sample_mixers_kernel, bblk=bblk),
        grid=(BS // bblk,),
        in_specs=[_fixed((BS, D_K), (0, Z_LA // D_K)),
                  pl.BlockSpec((bblk, Z_SG), lambda s: (s, 0)),
                  pl.BlockSpec((bblk, D_A), lambda s: (s, Z_SG // D_A)),
                  pl.BlockSpec((bblk, N_HEADS, HEAD_K, HEAD_V), lambda s: (s, 0, 0, 0)),
                  _fixed((1, HEAD_V)),
                  pl.BlockSpec((CONV_W - 1, bblk, d_conv), lambda s: (0, s, 0)),
                  pl.BlockSpec((bblk, d_conv), lambda s: (s, Z_GLU // d_conv)),
                  _fixed((CONV_W, nslab, LANES)), _fixed((nslab, LANES)), _fixed((1, d_conv)), _fixed((1, d_conv))],
        out_specs=(pl.BlockSpec((bblk, D_A), lambda s: (s, 0)),
                   pl.BlockSpec((1, bblk, N_HEADS, HEAD_K, HEAD_V), lambda s: (0, s, 0, 0, 0)),
                   pl.BlockSpec((bblk, d_conv), lambda s: (s, 0)),
                   pl.BlockSpec((CONV_W - 1, bblk, d_conv), lambda s: (0, s, 0))),
        out_shape=(jax.ShapeDtypeStruct((BS, D_A), F32),
                   jax.ShapeDtypeStruct((1, BS, N_HEADS, HEAD_K, HEAD_V), F32),
                   jax.ShapeDtypeStruct((BS, d_conv), F32),
                   jax.ShapeDtypeStruct((CONV_W - 1, BS, d_conv), F32)),
        scratch_shapes=[pltpu.VMEM((N_HEADS, HEAD_K, BS), F32)],
        compiler_params=_cparams(("arbitrary",)),
        name="sample_mixers",
    )(z_s, z_s, z_s, state_gla[0], gn,
      sc_t, z_s, cw_slabs, cb_slabs, vec(conv_ln_g[0]), vec(conv_ln_b[0]))
    conv_new_s = jnp.transpose(conv_new_s_t, (1, 0, 2))[None]

    tt = 512
    hist = 32
    nsteps = T // tt
    c_p, conv_new_p = pl.pallas_call(
        functools.partial(_conv_prompt_kernel, tt=tt, nsteps=nsteps, hist=hist),
        grid=(B, nsteps),
        in_specs=[pl.BlockSpec((tt, d_conv), lambda b, t: (b * nsteps + t, Z_GLU // d_conv)),
                  _fixed((CONV_W, nslab, LANES)), _fixed((nslab, LANES)),
                  _fixed((1, d_conv)), _fixed((1, d_conv))],
        out_specs=(pl.BlockSpec((tt, d_conv), lambda b, t: (b * nsteps + t, 0)),
                   pl.BlockSpec((1, 1, CONV_W - 1, d_conv), lambda b, t: (0, b, 0, 0))),
        out_shape=(jax.ShapeDtypeStruct((MP, d_conv), BF16),
                   jax.ShapeDtypeStruct((1, B, CONV_W - 1, d_conv), F32)),
        scratch_shapes=[pltpu.VMEM((nslab, hist + tt, LANES), F32), pltpu.VMEM((nslab, tt, LANES), F32)],
        compiler_params=_cparams(("arbitrary", "arbitrary")),
        name="conv_prompt",
    )(z_p, cw_slabs, cb_slabs, vec(conv_ln_g[0]), vec(conv_ln_b[0]))

    zbo = min(1024, D)
    wcols = min(2 * MXU_N, D)
    nwo = D // wcols
    njo = D // zbo
    assert nwo * wcols == D and njo * zbo == D
    row_o = lambda s: jnp.maximum(s - nwo, 0) // njo
    col_o = lambda s: lax.rem(jnp.maximum(s - nwo, 0), njo)
    x1_p, x1_s = pl.pallas_call(
        functools.partial(_outproj_kernel, nw=nwo, nj=njo, tmp=tmp, tms=tms),
        grid=(nwo + n_tiles * njo,),
        in_specs=[pl.BlockSpec((D_A + d_conv, wcols), lambda s: (0, jnp.minimum(s, nwo - 1))),
                  pl.BlockSpec((tmp, zbo), lambda s: (row_o(s), col_o(s))),
                  pl.BlockSpec((tms, 1, zbo), lambda s: (row_o(s), 0, col_o(s))),
                  pl.BlockSpec((tmp, D_A), lambda s: (row_o(s), 0)),
                  pl.BlockSpec((tms, D_A), lambda s: (row_o(s), 0)),
                  pl.BlockSpec((tmp, d_conv), lambda s: (row_o(s), 0)),
                  pl.BlockSpec((tms, d_conv), lambda s: (row_o(s), 0))],
        out_specs=(pl.BlockSpec((tmp, zbo), lambda s: (row_o(s), col_o(s))),
                   pl.BlockSpec((tms, zbo), lambda s: (row_o(s), col_o(s)))),
        out_shape=pair_shapes(D),
        scratch_shapes=[pltpu.VMEM((D // MXU_N, D_A + d_conv, MXU_N), BF16),
                        pltpu.VMEM((tm, D_A), BF16), pltpu.VMEM((tm, d_conv), BF16)],
        compiler_params=_cparams(("arbitrary",)),
        name="out_proj",
    )(w_out[0], xp, x_sample, oa_p, oa_s, c_p, c_s)

    tf = 256
    nf = d_ff // tf
    assert nf * tf == d_ff
    y_p, y_s = pl.pallas_call(
        functools.partial(_ffn_kernel, tmp=tmp, tms=tms, nf=nf),
        grid=(n_tiles, nf),
        in_specs=[pl.BlockSpec((tmp, D), lambda i, j: (i, 0)), pl.BlockSpec((tms, D), lambda i, j: (i, 0)),
                  _fixed((1, D)), _fixed((1, D)),
                  pl.BlockSpec((D, tf), lambda i, j: (0, j)),
                  pl.BlockSpec((D, tf), lambda i, j: (0, j + nf)),
                  pl.BlockSpec((tf, D), lambda i, j: (j, 0))],
        out_specs=(pl.BlockSpec((tmp, D), lambda i, j: (i, 0)),
                   pl.BlockSpec((tms, 1, D), lambda i, j: (i, 0, 0))),
        out_shape=(jax.ShapeDtypeStruct((MP, D), F32), jax.ShapeDtypeStruct((BS, 1, D), F32)),
        scratch_shapes=[pltpu.VMEM((tm, D), BF16), pltpu.VMEM((tms, D), F32)],
        compiler_params=_cparams(("arbitrary", "arbitrary")),
        name="ffn",
    )(x1_p, x1_s, norm_ffn[0].reshape(1, D), norm_final.reshape(1, D), w_ffn_in[0], w_ffn_in[0], w_ffn_out[0])

    return (y_p.reshape(B, T, D), y_s, gla_new_p, conv_new_p, gla_new_s, conv_new_s)
```

```python
import functools

import numpy as np
import jax
import jax.numpy as jnp
from jax import lax
from jax.experimental import pallas as pl
from jax.experimental.pallas import tpu as pltpu

F32 = jnp.float32
BF16 = jnp.bfloat16

RMS_EPS = 1e-6
LN_EPS = 1e-5
GATE_NORM = 16.0
N_HEADS = 4
HEAD_K = 128
HEAD_V = 256
D_K = N_HEADS * HEAD_K
D_A = N_HEADS * HEAD_V
GATE_RANK = 16
CONV_W = 31

LANES = 128
MXU_N = 256
GLA_CHUNK = 128
VMEM_LIMIT = 58 * 1024 * 1024

ZB = 512
Z_QKV, Z_SG, Z_GLU, Z_LA = 0, 2 * D_K + D_A, 2 * D_K + 2 * D_A, 2 * D_K + 3 * D_A
Z_COLS = Z_LA + D_K


def _cparams(sem):
    return pltpu.CompilerParams(dimension_semantics=sem, vmem_limit_bytes=VMEM_LIMIT)


def _dot(a, b):
    return jnp.dot(a, b, preferred_element_type=F32)


def _dot_nt(a, b):
    return lax.dot_general(a, b, (((1,), (1,)), ((), ())), preferred_element_type=F32)


def _rms_rows(x, gain):
    ms = jnp.mean(x * x, axis=-1, keepdims=True)
    return x * lax.rsqrt(ms + RMS_EPS) * gain


def _sigmoid(x):
    return 0.5 * jnp.tanh(0.5 * x) + 0.5


def _silu(x):
    t = 0.5 * x
    return t * jnp.tanh(t) + t


def _log_sigmoid(x):
    return jnp.minimum(x, 0.0) - jnp.log(1.0 + jnp.exp(-jnp.abs(x)))


def _norm_to_scratch(h_ref, xp_ref, xs, gain, tmp, tms, rows_per_pass=256):
    for r0 in range(0, tmp, rows_per_pass):
        h_ref[r0:r0 + rows_per_pass, :] = _rms_rows(xp_ref[r0:r0 + rows_per_pass, :], gain).astype(BF16)
    h_ref[tmp:tmp + tms, :] = _rms_rows(xs, gain).astype(BF16)


def _in_proj_kernel(wt_ref, wnext_ref, xp_hbm, xs_ref, g_ref, wup_ref, bg_ref, zp_ref, zs_ref,
                    wb_ref, wlr_ref, h_ref, xbuf_ref, xsem, *, nw, nj, tmp, tms, n_tiles):
    s = pl.program_id(0)
    rb = MXU_N
    n_direct = (2 * D_K + D_A) // rb

    def x_copy(i):
        slot = lax.rem(i, 2)
        return pltpu.make_async_copy(xp_hbm.at[pl.ds(i * tmp, tmp), :], xbuf_ref.at[slot], xsem.at[slot])

    @pl.when(s == 0)
    def _():
        x_copy(0).start()

    @pl.when(s < n_direct)
    def _():
        wb_ref[s] = wt_ref[...].T.astype(BF16)

    @pl.when((s >= n_direct) & (s < nw))
    def _():
        rows = jnp.concatenate([wt_ref[GATE_RANK:rb, :], wnext_ref[...]], axis=0)
        wb_ref[s] = rows.T.astype(BF16)

    @pl.when(s == n_direct)
    def _():
        wlr_ref[...] = wt_ref[0:LANES, :].T.astype(BF16)

    @pl.when(s >= nw)
    def _project():
        j = lax.rem(s - nw, nj)

        def pair(t0):
            h = h_ref[...]
            return jnp.concatenate([_dot(h, wb_ref[t0]), _dot(h, wb_ref[t0 + 1])], axis=1)

        def put(z):
            zp_ref[...] = z[:tmp]
            zs_ref[...] = z[tmp:]

        nqkv, nsg, nglu = Z_SG // ZB, (Z_GLU - Z_SG) // ZB, (Z_LA - Z_GLU) // ZB
        bpz = ZB // rb

        @pl.when(j == 0)
        def _():
            i = (s - nw) // nj
            x_copy(i).wait()

            @pl.when(i + 1 < n_tiles)
            def _():
                x_copy(i + 1).start()

            _norm_to_scratch(h_ref, xbuf_ref.at[lax.rem(i, 2)], xs_ref[:, 0, :], g_ref[...], tmp, tms)
            put(pair(0))

        @pl.when((j > 0) & (j < nqkv))
        def _():
            put(pair(bpz * j))

        @pl.when((j >= nqkv) & (j < nqkv + nsg))
        def _():
            z = pair(bpz * j)
            put(_silu(z))

        @pl.when((j >= nqkv + nsg) & (j < nqkv + nsg + nglu))
        def _():
            ta = bpz * j
            ua = pair(ta)
            ug = pair(ta + bpz * nglu)
            put(ua * _sigmoid(ug))

        @pl.when(j == nqkv + nsg + nglu)
        def _():
            glr = _dot(h_ref[...], wlr_ref[...])[:, 0:GATE_RANK]
            logit = _dot(glr, wup_ref[...]) + bg_ref[...]
            put(_log_sigmoid(logit) * (1.0 / GATE_NORM))


def _conv_prompt_kernel(glu_ref, cw_ref, cb_ref, lg_ref, lb_ref, c_ref, cnew_ref, slab_ref, y_ref,
                        *, tt, nsteps, hist):
    t_idx = pl.program_id(1)
    nslab = slab_ref.shape[0]
    pad = hist - (CONV_W - 1)

    @pl.when(t_idx == 0)
    def _():
        slab_ref[:, 0:hist, :] = jnp.zeros((nslab, hist, LANES), F32)

    @pl.when(t_idx > 0)
    def _():
        slab_ref[:, 0:hist, :] = slab_ref[:, tt:tt + hist, :]

    for sl in range(nslab):
        slab_ref[sl, hist:hist + tt, :] = glu_ref[:, sl * LANES:(sl + 1) * LANES]

    def slab_conv(sl, carry):
        acc = None
        for res in range(8):
            part = None
            for w in range(res, CONV_W, 8):
                term = slab_ref[sl, pad + w:pad + w + tt, :] * cw_ref[w, pl.ds(sl, 1), :]
                part = term if part is None else part + term
            acc = part if acc is None else acc + part
        y_ref[sl] = acc + cb_ref[pl.ds(sl, 1), :]
        return carry

    lax.fori_loop(0, nslab, slab_conv, 0)

    y = y_ref[...]
    inv_c = 1.0 / (nslab * LANES)
    mu = jnp.sum(jnp.sum(y, axis=0), axis=-1, keepdims=True) * inv_c
    yc = y - mu[None]
    var = jnp.sum(jnp.sum(yc * yc, axis=0), axis=-1, keepdims=True) * inv_c
    rstd = lax.rsqrt(var + LN_EPS)
    for sl in range(nslab):
        cs = slice(sl * LANES, (sl + 1) * LANES)
        z = yc[sl] * rstd * lg_ref[:, cs] + lb_ref[:, cs]
        c_ref[:, cs] = _silu(z).astype(c_ref.dtype)

    @pl.when(t_idx == nsteps - 1)
    def _():
        for sl in range(nslab):
            cnew_ref[0, 0, :, sl * LANES:(sl + 1) * LANES] = slab_ref[sl, hist + tt - (CONV_W - 1):hist + tt, :]


def _outproj_kernel(w_ref, xp_ref, xs_ref, oap_ref, oas_ref, cp_ref, cs_ref, op_ref, os_ref,
                    wb_ref, a_ref, c_ref, *, nw, nj, tmp, tms):
    s = pl.program_id(0)
    rb = MXU_N
    wpl = w_ref.shape[1] // rb
    bpc = xp_ref.shape[1] // rb

    @pl.when(s < nw)
    def _load_weight():
        for q in range(wpl):
            wb_ref[s * wpl + q] = w_ref[:, q * rb:(q + 1) * rb].astype(BF16)

    @pl.when(s >= nw)
    def _project():
        j = lax.rem(s - nw, nj)

        @pl.when(j == 0)
        def _():
            a_ref[0:tmp, :] = oap_ref[...]
            a_ref[tmp:tmp + tms, :] = oas_ref[...].astype(BF16)
            c_ref[0:tmp, :] = cp_ref[...]
            c_ref[tmp:tmp + tms, :] = cs_ref[...].astype(BF16)

        for half in range(bpc):
            t = bpc * j + half
            cols = slice(half * rb, (half + 1) * rb)
            z = _dot(a_ref[...], wb_ref[t, 0:D_A, :]) + _dot(c_ref[...], wb_ref[t, D_A:, :])
            op_ref[:, cols] = xp_ref[:, cols] + z[:tmp]
            os_ref[:, cols] = xs_ref[:, 0, cols] + z[tmp:]


def _ffn_kernel(xp_ref, xs_ref, g_ref, gf_ref, wg_ref, wu_ref, wo_ref, op_ref, os_ref, h_ref, accs_ref,
                *, tmp, tms, nf):
    j = pl.program_id(1)

    def ffn_block(acc_p, acc_s):
        h = h_ref[...]
        fg = _dot(h, wg_ref[...].astype(BF16))
        fu = _dot(h, wu_ref[...].astype(BF16))
        act = (_silu(fg) * fu).astype(BF16)
        wo = wo_ref[...].astype(BF16)
        half = tmp // 2
        op_ref[0:half, :] = acc_p[0:half, :] + _dot(act[0:half], wo)
        z = _dot(act[half:], wo)
        op_ref[half:tmp, :] = acc_p[half:tmp, :] + z[:tmp - half]
        accs_ref[...] = acc_s[...] + z[tmp - half:]

    @pl.when(j == 0)
    def _():
        _norm_to_scratch(h_ref, xp_ref, xs_ref[...], g_ref[...], tmp, tms)
        ffn_block(xp_ref, xs_ref)

    @pl.when((j > 0) & (j < nf - 1))
    def _():
        ffn_block(op_ref, accs_ref)

    @pl.when(j == nf - 1)
    def _():
        ffn_block(op_ref, accs_ref)
        gf = gf_ref[...]
        for r0 in range(0, tmp, 256):
            op_ref[r0:r0 + 256, :] = _rms_rows(op_ref[r0:r0 + 256, :], gf)
        os_ref[:, 0, :] = _rms_rows(accs_ref[...], gf)


@functools.lru_cache(maxsize=None)
def _gla_constants(L):
    nlev = int(np.log2(L)) + 1
    m = np.zeros((nlev + 2, L, L), np.float32)
    wm = np.zeros((nlev, L, L), np.float32)
    idx = np.arange(L)
    wm[0] = np.eye(L, dtype=np.float32)
    for lev in range(1, nlev):
        half = L >> lev
        pos = idx % (2 * half)
        upper = pos >= half
        r = idx - pos + half - 1
        t = idx[None, :]
        up_rows = (t > r[:, None]) & (t <= idx[:, None]) & upper[:, None]
        lo_rows = (t > idx[:, None]) & (t <= r[:, None]) & (~upper)[:, None]
        m[lev] = (up_rows | lo_rows).astype(np.float32)
        same = (idx[:, None] // (2 * half)) == (idx[None, :] // (2 * half))
        wm[lev] = (same & upper[:, None] & (~upper)[None, :]).astype(np.float32)
    m[nlev] = (idx[None, :] <= idx[:, None]).astype(np.float32)
    m[nlev + 1] = (idx[None, :] > idx[:, None]).astype(np.float32)
    m = m.reshape((nlev + 2) * L, L)
    return nlev, np.concatenate([m, m], axis=1), wm


def _gla_prompt_kernel(qkv_ref, la_ref, sg_ref, mall_ref, wm_ref, gn_ref,
                       o_ref, snew_ref, s_ref, *, nlev, nsteps, chunks_per_step):
    c = pl.program_id(1)

    @pl.when(c == 0)
    def _():
        s_ref[...] = jnp.zeros_like(s_ref)

    L = GLA_CHUNK
    scale = HEAD_K ** -0.5
    gn = gn_ref[...]
    chunks = [slice(sub * L, (sub + 1) * L) for sub in range(chunks_per_step)]
    pairs = [(rows, h) for rows in chunks for h in range(N_HEADS)]

    def q_of(rows, h):
        return qkv_ref[rows, h * HEAD_K:(h + 1) * HEAD_K] * scale

    def k_of(rows, h):
        return qkv_ref[rows, D_K + h * HEAD_K:D_K + (h + 1) * HEAD_K]

    mall2 = mall_ref[...]
    e_all = {}
    for rows in chunks:
        la = la_ref[rows, :]
        la_hi = la.astype(BF16)
        la_lo = (la - la_hi.astype(F32)).astype(BF16)
        e_all[rows.start] = _dot(mall2, jnp.concatenate([la_hi, la_lo], axis=0))

    def scaled(rows, h, lev):
        if lev == 0:
            return q_of(rows, h).astype(BF16), k_of(rows, h).astype(BF16)
        e = jnp.exp(e_all[rows.start][lev * L:(lev + 1) * L, h * HEAD_K:(h + 1) * HEAD_K])
        return (q_of(rows, h) * e).astype(BF16), (k_of(rows, h) * e).astype(BF16)

    assert len(pairs) % 2 == 0
    zero = jnp.zeros((L, HEAD_K), BF16)
    amat = {}
    for lev in range(nlev):
        for pa, pb in zip(pairs[0::2], pairs[1::2]):
            qa, ka = scaled(*pa, lev)
            qb, kb = scaled(*pb, lev)
            lhs = jnp.concatenate([qa, qb], axis=1)
            rhs_t = jnp.concatenate([jnp.concatenate([ka, zero], axis=1),
                                     jnp.concatenate([zero, kb], axis=1)], axis=0)
            p = _dot_nt(lhs, rhs_t)
            for key, blk in (((pa[0].start, pa[1]), p[:, 0:L]), ((pb[0].start, pb[1]), p[:, L:2 * L])):
                term = wm_ref[lev] * blk
                amat[key] = term if lev == 0 else amat[key] + term

    for rows, h in pairs:
        ks = slice(h * HEAD_K, (h + 1) * HEAD_K)
        vs = slice(h * HEAD_V, (h + 1) * HEAD_V)
        q, k, a = q_of(rows, h), k_of(rows, h), amat[rows.start, h]
        vb = qkv_ref[rows, 2 * D_K + h * HEAD_V:2 * D_K + (h + 1) * HEAD_V].astype(BF16)

        eb = e_all[rows.start][nlev * L:(nlev + 1) * L, ks]
        ek = e_all[rows.start][(nlev + 1) * L:(nlev + 2) * L, ks]
        qh = (q * jnp.exp(eb)).astype(BF16)
        kh = k * jnp.exp(ek)
        s = s_ref[h]
        o = _dot(jnp.concatenate([a.astype(BF16), qh], axis=1),
                 jnp.concatenate([vb, s.astype(BF16)], axis=0))

        blast = jnp.broadcast_to(eb[L - 1:L, :], (HEAD_K, HEAD_K))
        acol = jnp.exp(blast.T)
        acol = jnp.concatenate([acol, acol], axis=1)
        s_ref[h] = acol * s + _dot(kh.T.astype(BF16), vb)

        y = _rms_rows(o, gn) * sg_ref[rows, vs]
        o_ref[rows, vs] = y.astype(o_ref.dtype)

    @pl.when(c == nsteps - 1)
    def _():
        snew_ref[0, 0] = s_ref[...]


def _sample_mixers_kernel(la_ref, qkv_ref, sg_ref, st_ref, gn_ref, sc_ref, glu_ref, cw_ref, cb_ref, lg_ref, lb_ref,
                          o_ref, snew_ref, c_ref, cnew_ref, at_ref, *, bblk):
    s_idx = pl.program_id(0)
    _conv_sample_kernel(sc_ref, glu_ref, cw_ref, cb_ref, lg_ref, lb_ref, c_ref, cnew_ref)

    @pl.when(s_idx == 0)
    def _():
        for h in range(N_HEADS):
            at_ref[h] = jnp.exp(la_ref[:, h * HEAD_K:(h + 1) * HEAD_K]).T

    gn = gn_ref[...]
    scale = HEAD_K ** -0.5
    lane = lax.broadcasted_iota(jnp.int32, (HEAD_K, LANES), 1)
    row = lax.broadcasted_iota(jnp.int32, (bblk, HEAD_V), 0)
    for h in range(N_HEADS):
        vs = slice(h * HEAD_V, (h + 1) * HEAD_V)
        q = (qkv_ref[:, h * HEAD_K:(h + 1) * HEAD_K] * scale).astype(BF16)
        kt = qkv_ref[:, D_K + h * HEAD_K:D_K + (h + 1) * HEAD_K].T.astype(BF16)
        v = qkv_ref[:, 2 * D_K + h * HEAD_V:2 * D_K + (h + 1) * HEAD_V].astype(BF16)
        vdiag = jnp.concatenate([jnp.where(row == j, v, jnp.zeros_like(v)) for j in range(bblk)], axis=1)
        kv = _dot(kt, vdiag)
        outs = []
        for j in range(bblk):
            b = s_idx * bblk + j
            acol = jnp.sum(jnp.where(lane == b, at_ref[h], 0.0), axis=1, keepdims=True)
            s_new = acol * st_ref[j, h] + kv[:, j * HEAD_V:(j + 1) * HEAD_V]
            snew_ref[0, j, h] = s_new
            outs.append(_dot(q, s_new.astype(BF16))[j:j + 1, :])
        for j in range(bblk):
            o_ref[j:j + 1, vs] = _rms_rows(outs[j], gn) * sg_ref[j:j + 1, vs]


def _ln_silu(y, g, b):
    mu = jnp.mean(y, axis=-1, keepdims=True)
    yc = y - mu
    var = jnp.mean(yc * yc, axis=-1, keepdims=True)
    z = yc * lax.rsqrt(var + LN_EPS) * g + b
    return _silu(z)


def _conv_sample_kernel(sc_ref, glu_ref, cw_ref, cb_ref, lg_ref, lb_ref, c_ref, cnew_ref):
    nh = CONV_W - 1
    g = glu_ref[...]
    ys = []
    for sl in range(cw_ref.shape[1]):
        ls = slice(sl * LANES, (sl + 1) * LANES)
        y = g[:, ls] * cw_ref[nh, sl:sl + 1, :] + cb_ref[sl:sl + 1, :]
        for w in range(nh):
            y = y + sc_ref[w, :, ls] * cw_ref[w, sl:sl + 1, :]
        ys.append(y)
    c_ref[...] = _ln_silu(jnp.concatenate(ys, axis=1), lg_ref[...], lb_ref[...])
    for w in range(nh - 1):
        cnew_ref[w] = sc_ref[w + 1]
    cnew_ref[nh - 1] = g


def _fixed(shape, idx=None):
    idx = (0,) * len(shape) if idx is None else idx
    return pl.BlockSpec(shape, lambda *_: idx)


def kernel(x_prompt, x_sample, state_gla, state_conv, norm_mix, w_in, w_gate_up, b_gate, gla_norm,
           conv_w, conv_b, conv_ln_g, conv_ln_b, w_out, norm_ffn, w_ffn_in, w_ffn_out, norm_final):
    B, T, D = x_prompt.shape
    BS = x_sample.shape[0]
    assert state_gla.shape[0] == 1 and x_sample.shape[1] == 1
    d_conv = state_conv.shape[-1]
    d_ff = w_ffn_out.shape[1]
    in_cols = w_in.shape[-1]
    assert d_conv == D_A and in_cols == 2 * D_K + D_A + GATE_RANK + D_A + 2 * d_conv
    MP = B * T
    n_tiles = 8
    tmp, tms = MP // n_tiles, BS // n_tiles
    assert tmp * n_tiles == MP and tms * n_tiles == BS and tms % 8 == 0 and tmp % 256 == 0
    tm = tmp + tms

    xp = x_prompt.reshape(MP, D)

    def pair_shapes(cols):
        return (jax.ShapeDtypeStruct((MP, cols), F32), jax.ShapeDtypeStruct((BS, cols), F32))

    wt = jnp.swapaxes(w_in, 1, 2)[0]
    nw = (in_cols - GATE_RANK) // MXU_N
    assert nw * MXU_N == in_cols - GATE_RANK and MXU_N % GATE_RANK == 0
    nj = Z_COLS // ZB
    row_i = lambda s: jnp.maximum(s - nw, 0) // nj
    col_j = lambda s: lax.rem(jnp.maximum(s - nw, 0), nj)
    wblk = lambda s: jnp.minimum(s, nw - 1)
    z_p, z_s = pl.pallas_call(
        functools.partial(_in_proj_kernel, nw=nw, nj=nj, tmp=tmp, tms=tms, n_tiles=n_tiles),
        grid=(nw + n_tiles * nj,),
        in_specs=[pl.BlockSpec((MXU_N, D), lambda s: (wblk(s), 0)),
                  pl.BlockSpec((GATE_RANK, D), lambda s: ((wblk(s) + 1) * (MXU_N // GATE_RANK), 0)),
                  pl.BlockSpec(memory_space=pl.ANY),
                  pl.BlockSpec((tms, 1, D), lambda s: (row_i(s), 0, 0)),
                  _fixed((1, D)), _fixed((GATE_RANK, D_K)), _fixed((1, D_K))],
        out_specs=(pl.BlockSpec((tmp, ZB), lambda s: (row_i(s), col_j(s))),
                   pl.BlockSpec((tms, ZB), lambda s: (row_i(s), col_j(s)))),
        out_shape=pair_shapes(Z_COLS),
        scratch_shapes=[pltpu.VMEM((nw, D, MXU_N), BF16), pltpu.VMEM((D, LANES), BF16),
                        pltpu.VMEM((tm, D), BF16),
                        pltpu.VMEM((2, tmp, D), F32), pltpu.SemaphoreType.DMA((2,))],
        compiler_params=_cparams(("arbitrary",)),
        name="in_proj",
    )(wt, wt, xp, x_sample, norm_mix[0].reshape(1, D), w_gate_up[0], b_gate[0].reshape(1, D_K))

    chunks_per_step = 4
    L = GLA_CHUNK * chunks_per_step
    nchunks = T // L
    assert nchunks * L == T
    nlev, mall, wm = _gla_constants(GLA_CHUNK)
    gn = gla_norm[0].reshape(1, HEAD_V)
    zrow = lambda b, c: b * nchunks + c
    oa_p, gla_new_p = pl.pallas_call(
        functools.partial(_gla_prompt_kernel, nlev=nlev, nsteps=nchunks, chunks_per_step=chunks_per_step),
        grid=(B, nchunks),
        in_specs=[pl.BlockSpec((L, Z_SG), lambda b, c: (zrow(b, c), 0)),
                  pl.BlockSpec((L, D_K), lambda b, c: (zrow(b, c), Z_LA // D_K)),
                  pl.BlockSpec((L, D_A), lambda b, c: (zrow(b, c), Z_SG // D_A)),
                  _fixed(mall.shape), _fixed(wm.shape),
                  _fixed((1, HEAD_V))],
        out_specs=(pl.BlockSpec((L, D_A), lambda b, c: (zrow(b, c), 0)),
                   pl.BlockSpec((1, 1, N_HEADS, HEAD_K, HEAD_V), lambda b, c: (0, b, 0, 0, 0))),
        out_shape=(jax.ShapeDtypeStruct((MP, D_A), BF16),
                   jax.ShapeDtypeStruct((1, B, N_HEADS, HEAD_K, HEAD_V), F32)),
        scratch_shapes=[pltpu.VMEM((N_HEADS, HEAD_K, HEAD_V), F32)],
        compiler_params=_cparams(("arbitrary", "arbitrary")),
        name="gla_prompt",
    )(z_p, z_p, z_p, jnp.asarray(mall, BF16), jnp.asarray(wm), gn)

    nslab = d_conv // LANES
    vec = lambda a: a.reshape(1, d_conv)
    cw_slabs = conv_w[0].reshape(CONV_W, nslab, LANES)
    cb_slabs = conv_b[0].reshape(nslab, LANES)

    bblk = 16
    assert BS == LANES and BS % bblk == 0
    sc_t = jnp.transpose(state_conv[0], (1, 0, 2))
    oa_s, gla_new_s, c_s, conv_new_s_t = pl.pallas_call(
        functools.partial(_sample_mixers_kernel, bblk=bblk),
        grid=(BS // bblk,),
        in_specs=[_fixed((BS, D_K), (0, Z_LA // D_K)),
                  pl.BlockSpec((bblk, Z_SG), lambda s: (s, 0)),
                  pl.BlockSpec((bblk, D_A), lambda s: (s, Z_SG // D_A)),
                  pl.BlockSpec((bblk, N_HEADS, HEAD_K, HEAD_V), lambda s: (s, 0, 0, 0)),
                  _fixed((1, HEAD_V)),
                  pl.BlockSpec((CONV_W - 1, bblk, d_conv), lambda s: (0, s, 0)),
                  pl.BlockSpec((bblk, d_conv), lambda s: (s, Z_GLU // d_conv)),
                  _fixed((CONV_W, nslab, LANES)), _fixed((nslab, LANES)), _fixed((1, d_conv)), _fixed((1, d_conv))],
        out_specs=(pl.BlockSpec((bblk, D_A), lambda s: (s, 0)),
                   pl.BlockSpec((1, bblk, N_HEADS, HEAD_K, HEAD_V), lambda s: (0, s, 0, 0, 0)),
                   pl.BlockSpec((bblk, d_conv), lambda s: (s, 0)),
                   pl.BlockSpec((CONV_W - 1, bblk, d_conv), lambda s: (0, s, 0))),
        out_shape=(jax.ShapeDtypeStruct((BS, D_A), F32),
                   jax.ShapeDtypeStruct((1, BS, N_HEADS, HEAD_K, HEAD_V), F32),
                   jax.ShapeDtypeStruct((BS, d_conv), F32),
                   jax.ShapeDtypeStruct((CONV_W - 1, BS, d_conv), F32)),
        scratch_shapes=[pltpu.VMEM((N_HEADS, HEAD_K, BS), F32)],
        compiler_params=_cparams(("arbitrary",)),
        name="sample_mixers",
    )(z_s, z_s, z_s, state_gla[0], gn,
      sc_t, z_s, cw_slabs, cb_slabs, vec(conv_ln_g[0]), vec(conv_ln_b[0]))
    conv_new_s = jnp.transpose(conv_new_s_t, (1, 0, 2))[None]

    tt = 512
    hist = 32
    nsteps = T // tt
    c_p, conv_new_p = pl.pallas_call(
        functools.partial(_conv_prompt_kernel, tt=tt, nsteps=nsteps, hist=hist),
        grid=(B, nsteps),
        in_specs=[pl.BlockSpec((tt, d_conv), lambda b, t: (b * nsteps + t, Z_GLU // d_conv)),
                  _fixed((CONV_W, nslab, LANES)), _fixed((nslab, LANES)),
                  _fixed((1, d_conv)), _fixed((1, d_conv))],
        out_specs=(pl.BlockSpec((tt, d_conv), lambda b, t: (b * nsteps + t, 0)),
                   pl.BlockSpec((1, 1, CONV_W - 1, d_conv), lambda b, t: (0, b, 0, 0))),
        out_shape=(jax.ShapeDtypeStruct((MP, d_conv), BF16),
                   jax.ShapeDtypeStruct((1, B, CONV_W - 1, d_conv), F32)),
        scratch_shapes=[pltpu.VMEM((nslab, hist + tt, LANES), F32), pltpu.VMEM((nslab, tt, LANES), F32)],
        compiler_params=_cparams(("arbitrary", "arbitrary")),
        name="conv_prompt",
    )(z_p, cw_slabs, cb_slabs, vec(conv_ln_g[0]), vec(conv_ln_b[0]))

    zbo = min(1024, D)
    wcols = min(2 * MXU_N, D)
    nwo = D // wcols
    njo = D // zbo
    assert nwo * wcols == D and njo * zbo == D
    row_o = lambda s: jnp.maximum(s - nwo, 0) // njo
    col_o = lambda s: lax.rem(jnp.maximum(s - nwo, 0), njo)
    x1_p, x1_s = pl.pallas_call(
        functools.partial(_outproj_kernel, nw=nwo, nj=njo, tmp=tmp, tms=tms),
        grid=(nwo + n_tiles * njo,),
        in_specs=[pl.BlockSpec((D_A + d_conv, wcols), lambda s: (0, jnp.minimum(s, nwo - 1))),
                  pl.BlockSpec((tmp, zbo), lambda s: (row_o(s), col_o(s))),
                  pl.BlockSpec((tms, 1, zbo), lambda s: (row_o(s), 0, col_o(s))),
                  pl.BlockSpec((tmp, D_A), lambda s: (row_o(s), 0)),
                  pl.BlockSpec((tms, D_A), lambda s: (row_o(s), 0)),
                  pl.BlockSpec((tmp, d_conv), lambda s: (row_o(s), 0)),
                  pl.BlockSpec((tms, d_conv), lambda s: (row_o(s), 0))],
        out_specs=(pl.BlockSpec((tmp, zbo), lambda s: (row_o(s), col_o(s))),
                   pl.BlockSpec((tms, zbo), lambda s: (row_o(s), col_o(s)))),
        out_shape=pair_shapes(D),
        scratch_shapes=[pltpu.VMEM((D // MXU_N, D_A + d_conv, MXU_N), BF16),
                        pltpu.VMEM((tm, D_A), BF16), pltpu.VMEM((tm, d_conv), BF16)],
        compiler_params=_cparams(("arbitrary",)),
        name="out_proj",
    )(w_out[0], xp, x_sample, oa_p, oa_s, c_p, c_s)

    tf = 256
    nf = d_ff // tf
    assert nf * tf == d_ff
    y_p, y_s = pl.pallas_call(
        functools.partial(_ffn_kernel, tmp=tmp, tms=tms, nf=nf),
        grid=(n_tiles, nf),
        in_specs=[pl.BlockSpec((tmp, D), lambda i, j: (i, 0)), pl.BlockSpec((tms, D), lambda i, j: (i, 0)),
                  _fixed((1, D)), _fixed((1, D)),
                  pl.BlockSpec((D, tf), lambda i, j: (0, j)),
                  pl.BlockSpec((D, tf), lambda i, j: (0, j + nf)),
                  pl.BlockSpec((tf, D), lambda i, j: (j, 0))],
        out_specs=(pl.BlockSpec((tmp, D), lambda i, j: (i, 0)),
                   pl.BlockSpec((tms, 1, D), lambda i, j: (i, 0, 0))),
        out_shape=(jax.ShapeDtypeStruct((MP, D), F32), jax.ShapeDtypeStruct((BS, 1, D), F32)),
        scratch_shapes=[pltpu.VMEM((tm, D), BF16), pltpu.VMEM((tms, D), F32)],
        compiler_params=_cparams(("arbitrary", "arbitrary")),
        name="ffn",
    )(x1_p, x1_s, norm_ffn[0].reshape(1, D), norm_final.reshape(1, D), w_ffn_in[0], w_ffn_in[0], w_ffn_out[0])

    return (y_p.reshape(B, T, D), y_s, gla_new_p, conv_new_p, gla_new_s, conv_new_s)
```

```python
import functools

import numpy as np
import jax
import jax.numpy as jnp
from jax import lax
from jax.experimental import pallas as pl
from jax.experimental.pallas import tpu as pltpu

F32 = jnp.float32
BF16 = jnp.bfloat16

RMS_EPS = 1e-6
LN_EPS = 1e-5
GATE_NORM = 16.0
N_HEADS = 4
HEAD_K = 128
HEAD_V = 256
D_K = N_HEADS * HEAD_K
D_A = N_HEADS * HEAD_V
GATE_RANK = 16
CONV_W = 31

LANES = 128
MXU_N = 256
GLA_CHUNK = 128
VMEM_LIMIT = 58 * 1024 * 1024

ZB = 512
Z_QKV, Z_SG, Z_GLU, Z_LA = 0, 2 * D_K + D_A, 2 * D_K + 2 * D_A, 2 * D_K + 3 * D_A
Z_COLS = Z_LA + D_K


def _cparams(sem):
    return pltpu.CompilerParams(dimension_semantics=sem, vmem_limit_bytes=VMEM_LIMIT)


def _dot(a, b):
    return jnp.dot(a, b, preferred_element_type=F32)


def _dot_nt(a, b):
    return lax.dot_general(a, b, (((1,), (1,)), ((), ())), preferred_element_type=F32)


def _rms_rows(x, gain):
    ms = jnp.mean(x * x, axis=-1, keepdims=True)
    return x * lax.rsqrt(ms + RMS_EPS) * gain


def _sigmoid(x):
    return 0.5 * jnp.tanh(0.5 * x) + 0.5


def _silu(x):
    t = 0.5 * x
    return t * jnp.tanh(t) + t


def _log_sigmoid(x):
    return jnp.minimum(x, 0.0) - jnp.log(1.0 + jnp.exp(-jnp.abs(x)))


def _norm_to_scratch(h_ref, xp_ref, xs, gain, tmp, tms, rows_per_pass=256):
    for r0 in range(0, tmp, rows_per_pass):
        h_ref[r0:r0 + rows_per_pass, :] = _rms_rows(xp_ref[r0:r0 + rows_per_pass, :], gain).astype(BF16)
    h_ref[tmp:tmp + tms, :] = _rms_rows(xs, gain).astype(BF16)


def _in_proj_kernel(wt_ref, wnext_ref, xp_hbm, xs_ref, g_ref, wup_ref, bg_ref, zp_ref, zs_ref,
                    wb_ref, wlr_ref, h_ref, xbuf_ref, xsem, *, nw, nj, tmp, tms, n_tiles):
    s = pl.program_id(0)
    rb = MXU_N
    n_direct = (2 * D_K + D_A) // rb

    def x_copy(i):
        slot = lax.rem(i, 2)
        return pltpu.make_async_copy(xp_hbm.at[pl.ds(i * tmp, tmp), :], xbuf_ref.at[slot], xsem.at[slot])

    @pl.when(s == 0)
    def _():
        x_copy(0).start()

    @pl.when(s < n_direct)
    def _():
        wb_ref[s] = wt_ref[...].T.astype(BF16)

    @pl.when((s >= n_direct) & (s < nw))
    def _():
        rows = jnp.concatenate([wt_ref[GATE_RANK:rb, :], wnext_ref[...]], axis=0)
        wb_ref[s] = rows.T.astype(BF16)

    @pl.when(s == n_direct)
    def _():
        wlr_ref[...] = wt_ref[0:LANES, :].T.astype(BF16)

    @pl.when(s >= nw)
    def _project():
        j = lax.rem(s - nw, nj)

        def pair(t0):
            h = h_ref[...]
            return jnp.concatenate([_dot(h, wb_ref[t0]), _dot(h, wb_ref[t0 + 1])], axis=1)

        def put(z):
            zp_ref[...] = z[:tmp]
            zs_ref[...] = z[tmp:]

        nqkv, nsg, nglu = Z_SG // ZB, (Z_GLU - Z_SG) // ZB, (Z_LA - Z_GLU) // ZB
        bpz = ZB // rb

        @pl.when(j == 0)
        def _():
            i = (s - nw) // nj
            x_copy(i).wait()

            @pl.when(i + 1 < n_tiles)
            def _():
                x_copy(i + 1).start()

            _norm_to_scratch(h_ref, xbuf_ref.at[lax.rem(i, 2)], xs_ref[:, 0, :], g_ref[...], tmp, tms)
            put(pair(0))

        @pl.when((j > 0) & (j < nqkv))
        def _():
            put(pair(bpz * j))

        @pl.when((j >= nqkv) & (j < nqkv + nsg))
        def _():
            z = pair(bpz * j)
            put(_silu(z))

        @pl.when((j >= nqkv + nsg) & (j < nqkv + nsg + nglu))
        def _():
            ta = bpz * j
            ua = pair(ta)
            ug = pair(ta + bpz * nglu)
            put(ua * _sigmoid(ug))

        @pl.when(j == nqkv + nsg + nglu)
        def _():
            glr = _dot(h_ref[...], wlr_ref[...])[:, 0:GATE_RANK]
            logit = _dot(glr, wup_ref[...]) + bg_ref[...]
            put(_log_sigmoid(logit) * (1.0 / GATE_NORM))


def _conv_prompt_kernel(glu_ref, cw_ref, cb_ref, lg_ref, lb_ref, c_ref, cnew_ref, slab_ref, y_ref,
                        *, tt, nsteps, hist):
    t_idx = pl.program_id(1)
    nslab = slab_ref.shape[0]
    pad = hist - (CONV_W - 1)

    @pl.when(t_idx == 0)
    def _():
        slab_ref[:, 0:hist, :] = jnp.zeros((nslab, hist, LANES), F32)

    @pl.when(t_idx > 0)
    def _():
        slab_ref[:, 0:hist, :] = slab_ref[:, tt:tt + hist, :]

    for sl in range(nslab):
        slab_ref[sl, hist:hist + tt, :] = glu_ref[:, sl * LANES:(sl + 1) * LANES]

    def slab_conv(sl, carry):
        acc = None
        for res in range(8):
            part = None
            for w in range(res, CONV_W, 8):
                term = slab_ref[sl, pad + w:pad + w + tt, :] * cw_ref[w, pl.ds(sl, 1), :]
                part = term if part is None else part + term
            acc = part if acc is None else acc + part
        y_ref[sl] = acc + cb_ref[pl.ds(sl, 1), :]
        return carry

    lax.fori_loop(0, nslab, slab_conv, 0)

    y = y_ref[...]
    inv_c = 1.0 / (nslab * LANES)
    mu = jnp.sum(jnp.sum(y, axis=0), axis=-1, keepdims=True) * inv_c
    yc = y - mu[None]
    var = jnp.sum(jnp.sum(yc * yc, axis=0), axis=-1, keepdims=True) * inv_c
    rstd = lax.rsqrt(var + LN_EPS)
    for sl in range(nslab):
        cs = slice(sl * LANES, (sl + 1) * LANES)
        z = yc[sl] * rstd * lg_ref[:, cs] + lb_ref[:, cs]
        c_ref[:, cs] = _silu(z).astype(c_ref.dtype)

    @pl.when(t_idx == nsteps - 1)
    def _():
        for sl in range(nslab):
            cnew_ref[0, 0, :, sl * LANES:(sl + 1) * LANES] = slab_ref[sl, hist + tt - (CONV_W - 1):hist + tt, :]


def _outproj_kernel(w_ref, xp_ref, xs_ref, oap_ref, oas_ref, cp_ref, cs_ref, op_ref, os_ref,
                    wb_ref, a_ref, c_ref, *, nw, nj, tmp, tms):
    s = pl.program_id(0)
    rb = MXU_N
    wpl = w_ref.shape[1] // rb
    bpc = xp_ref.shape[1] // rb

    @pl.when(s < nw)
    def _load_weight():
        for q in range(wpl):
            wb_ref[s * wpl + q] = w_ref[:, q * rb:(q + 1) * rb].astype(BF16)

    @pl.when(s >= nw)
    def _project():
        j = lax.rem(s - nw, nj)

        @pl.when(j == 0)
        def _():
            a_ref[0:tmp, :] = oap_ref[...]
            a_ref[tmp:tmp + tms, :] = oas_ref[...].astype(BF16)
            c_ref[0:tmp, :] = cp_ref[...]
            c_ref[tmp:tmp + tms, :] = cs_ref[...].astype(BF16)

        for half in range(bpc):
            t = bpc * j + half
            cols = slice(half * rb, (half + 1) * rb)
            z = _dot(a_ref[...], wb_ref[t, 0:D_A, :]) + _dot(c_ref[...], wb_ref[t, D_A:, :])
            op_ref[:, cols] = xp_ref[:, cols] + z[:tmp]
            os_ref[:, cols] = xs_ref[:, 0, cols] + z[tmp:]


def _ffn_kernel(xp_hbm, xs_ref, g_ref, gf_ref, wg_ref, wu_ref, wo_ref, op_ref, os_ref, h_ref, accs_ref,
                xbuf_ref, xsem, *, tmp, tms, nf, n_tiles):
    i = pl.program_id(0)
    j = pl.program_id(1)

    def x_copy(t):
        slot = lax.rem(t, 2)
        return pltpu.make_async_copy(xp_hbm.at[pl.ds(t * tmp, tmp), :], xbuf_ref.at[slot], xsem.at[slot])

    def ffn_block(acc_p, acc_s):
        h = h_ref[...]
        fg = _dot(h, wg_ref[...].astype(BF16))
        fu = _dot(h, wu_ref[...].astype(BF16))
        act = (_silu(fg) * fu).astype(BF16)
        wo = wo_ref[...].astype(BF16)
        half = tmp // 2
        op_ref[0:half, :] = acc_p[0:half, :] + _dot(act[0:half], wo)
        z = _dot(act[half:], wo)
        op_ref[half:tmp, :] = acc_p[half:tmp, :] + z[:tmp - half]
        accs_ref[...] = acc_s[...] + z[tmp - half:]

    @pl.when(j == 0)
    def _():
        @pl.when(i == 0)
        def _():
            x_copy(0).start()

        x_copy(i).wait()

        @pl.when(i + 1 < n_tiles)
        def _():
            x_copy(i + 1).start()

        xp_ref = xbuf_ref.at[lax.rem(i, 2)]
        _norm_to_scratch(h_ref, xp_ref, xs_ref[...], g_ref[...], tmp, tms)
        ffn_block(xp_ref, xs_ref)

    @pl.when((j > 0) & (j < nf - 1))
    def _():
        ffn_block(op_ref, accs_ref)

    @pl.when(j == nf - 1)
    def _():
        ffn_block(op_ref, accs_ref)
        gf = gf_ref[...]
        for r0 in range(0, tmp, 256):
            op_ref[r0:r0 + 256, :] = _rms_rows(op_ref[r0:r0 + 256, :], gf)
        os_ref[:, 0, :] = _rms_rows(accs_ref[...], gf)


@functools.lru_cache(maxsize=None)
def _gla_constants(L):
    nlev = int(np.log2(L)) + 1
    m = np.zeros((nlev + 2, L, L), np.float32)
    wm = np.zeros((nlev, L, L), np.float32)
    idx = np.arange(L)
    wm[0] = np.eye(L, dtype=np.float32)
    for lev in range(1, nlev):
        half = L >> lev
        pos = idx % (2 * half)
        upper = pos >= half
        r = idx - pos + half - 1
        t = idx[None, :]
        up_rows = (t > r[:, None]) & (t <= idx[:, None]) & upper[:, None]
        lo_rows = (t > idx[:, None]) & (t <= r[:, None]) & (~upper)[:, None]
        m[lev] = (up_rows | lo_rows).astype(np.float32)
        same = (idx[:, None] // (2 * half)) == (idx[None, :] // (2 * half))
        wm[lev] = (same & upper[:, None] & (~upper)[None, :]).astype(np.float32)
    m[nlev] = (idx[None, :] <= idx[:, None]).astype(np.float32)
    m[nlev + 1] = (idx[None, :] > idx[:, None]).astype(np.float32)
    m = m.reshape((nlev + 2) * L, L)
    return nlev, np.concatenate([m, m], axis=1), wm


def _gla_prompt_kernel(qkv_ref, la_ref, sg_ref, mall_ref, wm_ref, gn_ref,
                       o_ref, snew_ref, s_ref, *, nlev, nsteps, chunks_per_step):
    c = pl.program_id(1)

    @pl.when(c == 0)
    def _():
        s_ref[...] = jnp.zeros_like(s_ref)

    L = GLA_CHUNK
    scale = HEAD_K ** -0.5
    gn = gn_ref[...]
    chunks = [slice(sub * L, (sub + 1) * L) for sub in range(chunks_per_step)]
    pairs = [(rows, h) for rows in chunks for h in range(N_HEADS)]

    def q_of(rows, h):
        return qkv_ref[rows, h * HEAD_K:(h + 1) * HEAD_K] * scale

    def k_of(rows, h):
        return qkv_ref[rows, D_K + h * HEAD_K:D_K + (h + 1) * HEAD_K]

    mall2 = mall_ref[...]
    e_all = {}
    for rows in chunks:
        la = la_ref[rows, :]
        la_hi = la.astype(BF16)
        la_lo = (la - la_hi.astype(F32)).astype(BF16)
        e_all[rows.start] = _dot(mall2, jnp.concatenate([la_hi, la_lo], axis=0))

    def scaled(rows, h, lev):
        if lev == 0:
            return q_of(rows, h).astype(BF16), k_of(rows, h).astype(BF16)
        e = jnp.exp(e_all[rows.start][lev * L:(lev + 1) * L, h * HEAD_K:(h + 1) * HEAD_K])
        return (q_of(rows, h) * e).astype(BF16), (k_of(rows, h) * e).astype(BF16)

    assert len(pairs) % 2 == 0
    zero = jnp.zeros((L, HEAD_K), BF16)
    amat = {}
    for lev in range(nlev):
        for pa, pb in zip(pairs[0::2], pairs[1::2]):
            qa, ka = scaled(*pa, lev)
            qb, kb = scaled(*pb, lev)
            lhs = jnp.concatenate([qa, qb], axis=1)
            rhs_t = jnp.concatenate([jnp.concatenate([ka, zero], axis=1),
                                     jnp.concatenate([zero, kb], axis=1)], axis=0)
            p = _dot_nt(lhs, rhs_t)
            for key, blk in (((pa[0].start, pa[1]), p[:, 0:L]), ((pb[0].start, pb[1]), p[:, L:2 * L])):
                term = wm_ref[lev] * blk
                amat[key] = term if lev == 0 else amat[key] + term

    for rows, h in pairs:
        ks = slice(h * HEAD_K, (h + 1) * HEAD_K)
        vs = slice(h * HEAD_V, (h + 1) * HEAD_V)
        q, k, a = q_of(rows, h), k_of(rows, h), amat[rows.start, h]
        vb = qkv_ref[rows, 2 * D_K + h * HEAD_V:2 * D_K + (h + 1) * HEAD_V].astype(BF16)

        eb = e_all[rows.start][nlev * L:(nlev + 1) * L, ks]
        ek = e_all[rows.start][(nlev + 1) * L:(nlev + 2) * L, ks]
        qh = (q * jnp.exp(eb)).astype(BF16)
        kh = k * jnp.exp(ek)
        s = s_ref[h]
        o = _dot(jnp.concatenate([a.astype(BF16), qh], axis=1),
                 jnp.concatenate([vb, s.astype(BF16)], axis=0))

        blast = jnp.broadcast_to(eb[L - 1:L, :], (HEAD_K, HEAD_K))
        acol = jnp.exp(blast.T)
        acol = jnp.concatenate([acol, acol], axis=1)
        s_ref[h] = acol * s + _dot(kh.T.astype(BF16), vb)

        y = _rms_rows(o, gn) * sg_ref[rows, vs]
        o_ref[rows, vs] = y.astype(o_ref.dtype)

    @pl.when(c == nsteps - 1)
    def _():
        snew_ref[0, 0] = s_ref[...]


def _sample_mixers_kernel(la_ref, qkv_ref, sg_ref, st_ref, gn_ref, sc_ref, glu_ref, cw_ref, cb_ref, lg_ref, lb_ref,
                          o_ref, snew_ref, c_ref, cnew_ref, at_ref, *, bblk):
    s_idx = pl.program_id(0)
    _conv_sample_kernel(sc_ref, glu_ref, cw_ref, cb_ref, lg_ref, lb_ref, c_ref, cnew_ref)

    @pl.when(s_idx == 0)
    def _():
        for h in range(N_HEADS):
            at_ref[h] = jnp.exp(la_ref[:, h * HEAD_K:(h + 1) * HEAD_K]).T

    gn = gn_ref[...]
    scale = HEAD_K ** -0.5
    lane = lax.broadcasted_iota(jnp.int32, (HEAD_K, LANES), 1)
    row = lax.broadcasted_iota(jnp.int32, (bblk, HEAD_V), 0)
    for h in range(N_HEADS):
        vs = slice(h * HEAD_V, (h + 1) * HEAD_V)
        q = (qkv_ref[:, h * HEAD_K:(h + 1) * HEAD_K] * scale).astype(BF16)
        kt = qkv_ref[:, D_K + h * HEAD_K:D_K + (h + 1) * HEAD_K].T.astype(BF16)
        v = qkv_ref[:, 2 * D_K + h * HEAD_V:2 * D_K + (h + 1) * HEAD_V].astype(BF16)
        vdiag = jnp.concatenate([jnp.where(row == j, v, jnp.zeros_like(v)) for j in range(bblk)], axis=1)
        kv = _dot(kt, vdiag)
        outs = []
        for j in range(bblk):
            b = s_idx * bblk + j
            acol = jnp.sum(jnp.where(lane == b, at_ref[h], 0.0), axis=1, keepdims=True)
            s_new = acol * st_ref[j, h] + kv[:, j * HEAD_V:(j + 1) * HEAD_V]
            snew_ref[0, j, h] = s_new
            outs.append(_dot(q, s_new.astype(BF16))[j:j + 1, :])
        for j in range(bblk):
            o_ref[j:j + 1, vs] = _rms_rows(outs[j], gn) * sg_ref[j:j + 1, vs]


def _ln_silu(y, g, b):
    mu = jnp.mean(y, axis=-1, keepdims=True)
    yc = y - mu
    var = jnp.mean(yc * yc, axis=-1, keepdims=True)
    z = yc * lax.rsqrt(var + LN_EPS) * g + b
    return _silu(z)


def _conv_sample_kernel(sc_ref, glu_ref, cw_ref, cb_ref, lg_ref, lb_ref, c_ref, cnew_ref):
    nh = CONV_W - 1
    g = glu_ref[...]
    ys = []
    for sl in range(cw_ref.shape[1]):
        ls = slice(sl * LANES, (sl + 1) * LANES)
        y = g[:, ls] * cw_ref[nh, sl:sl + 1, :] + cb_ref[sl:sl + 1, :]
        for w in range(nh):
            y = y + sc_ref[w, :, ls] * cw_ref[w, sl:sl + 1, :]
        ys.append(y)
    c_ref[...] = _ln_silu(jnp.concatenate(ys, axis=1), lg_ref[...], lb_ref[...])
    for w in range(nh - 1):
        cnew_ref[w] = sc_ref[w + 1]
    cnew_ref[nh - 1] = g


def _fixed(shape, idx=None):
    idx = (0,) * len(shape) if idx is None else idx
    return pl.BlockSpec(shape, lambda *_: idx)


def kernel(x_prompt, x_sample, state_gla, state_conv, norm_mix, w_in, w_gate_up, b_gate, gla_norm,
           conv_w, conv_b, conv_ln_g, conv_ln_b, w_out, norm_ffn, w_ffn_in, w_ffn_out, norm_final):
    B, T, D = x_prompt.shape
    BS = x_sample.shape[0]
    assert state_gla.shape[0] == 1 and x_sample.shape[1] == 1
    d_conv = state_conv.shape[-1]
    d_ff = w_ffn_out.shape[1]
    in_cols = w_in.shape[-1]
    assert d_conv == D_A and in_cols == 2 * D_K + D_A + GATE_RANK + D_A + 2 * d_conv
    MP = B * T
    n_tiles = 8
    tmp, tms = MP // n_tiles, BS // n_tiles
    assert tmp * n_tiles == MP and tms * n_tiles == BS and tms % 8 == 0 and tmp % 256 == 0
    tm = tmp + tms

    xp = x_prompt.reshape(MP, D)

    def pair_shapes(cols):
        return (jax.ShapeDtypeStruct((MP, cols), F32), jax.ShapeDtypeStruct((BS, cols), F32))

    wt = jnp.swapaxes(w_in, 1, 2)[0]
    nw = (in_cols - GATE_RANK) // MXU_N
    assert nw * MXU_N == in_cols - GATE_RANK and MXU_N % GATE_RANK == 0
    nj = Z_COLS // ZB
    row_i = lambda s: jnp.maximum(s - nw, 0) // nj
    col_j = lambda s: lax.rem(jnp.maximum(s - nw, 0), nj)
    wblk = lambda s: jnp.minimum(s, nw - 1)
    z_p, z_s = pl.pallas_call(
        functools.partial(_in_proj_kernel, nw=nw, nj=nj, tmp=tmp, tms=tms, n_tiles=n_tiles),
        grid=(nw + n_tiles * nj,),
        in_specs=[pl.BlockSpec((MXU_N, D), lambda s: (wblk(s), 0)),
                  pl.BlockSpec((GATE_RANK, D), lambda s: ((wblk(s) + 1) * (MXU_N // GATE_RANK), 0)),
                  pl.BlockSpec(memory_space=pl.ANY),
                  pl.BlockSpec((tms, 1, D), lambda s: (row_i(s), 0, 0)),
                  _fixed((1, D)), _fixed((GATE_RANK, D_K)), _fixed((1, D_K))],
        out_specs=(pl.BlockSpec((tmp, ZB), lambda s: (row_i(s), col_j(s))),
                   pl.BlockSpec((tms, ZB), lambda s: (row_i(s), col_j(s)))),
        out_shape=pair_shapes(Z_COLS),
        scratch_shapes=[pltpu.VMEM((nw, D, MXU_N), BF16), pltpu.VMEM((D, LANES), BF16),
                        pltpu.VMEM((tm, D), BF16),
                        pltpu.VMEM((2, tmp, D), F32), pltpu.SemaphoreType.DMA((2,))],
        compiler_params=_cparams(("arbitrary",)),
        name="in_proj",
    )(wt, wt, xp, x_sample, norm_mix[0].reshape(1, D), w_gate_up[0], b_gate[0].reshape(1, D_K))

    chunks_per_step = 4
    L = GLA_CHUNK * chunks_per_step
    nchunks = T // L
    assert nchunks * L == T
    nlev, mall, wm = _gla_constants(GLA_CHUNK)
    gn = gla_norm[0].reshape(1, HEAD_V)
    zrow = lambda b, c: b * nchunks + c
    oa_p, gla_new_p = pl.pallas_call(
        functools.partial(_gla_prompt_kernel, nlev=nlev, nsteps=nchunks, chunks_per_step=chunks_per_step),
        grid=(B, nchunks),
        in_specs=[pl.BlockSpec((L, Z_SG), lambda b, c: (zrow(b, c), 0)),
                  pl.BlockSpec((L, D_K), lambda b, c: (zrow(b, c), Z_LA // D_K)),
                  pl.BlockSpec((L, D_A), lambda b, c: (zrow(b, c), Z_SG // D_A)),
                  _fixed(mall.shape), _fixed(wm.shape),
                  _fixed((1, HEAD_V))],
        out_specs=(pl.BlockSpec((L, D_A), lambda b, c: (zrow(b, c), 0)),
                   pl.BlockSpec((1, 1, N_HEADS, HEAD_K, HEAD_V), lambda b, c: (0, b, 0, 0, 0))),
        out_shape=(jax.ShapeDtypeStruct((MP, D_A), BF16),
                   jax.ShapeDtypeStruct((1, B, N_HEADS, HEAD_K, HEAD_V), F32)),
        scratch_shapes=[pltpu.VMEM((N_HEADS, HEAD_K, HEAD_V), F32)],
        compiler_params=_cparams(("arbitrary", "arbitrary")),
        name="gla_prompt",
    )(z_p, z_p, z_p, jnp.asarray(mall, BF16), jnp.asarray(wm), gn)

    nslab = d_conv // LANES
    vec = lambda a: a.reshape(1, d_conv)
    cw_slabs = conv_w[0].reshape(CONV_W, nslab, LANES)
    cb_slabs = conv_b[0].reshape(nslab, LANES)

    bblk = 16
    assert BS == LANES and BS % bblk == 0
    sc_t = jnp.transpose(state_conv[0], (1, 0, 2))
    oa_s, gla_new_s, c_s, conv_new_s_t = pl.pallas_call(
        functools.partial(_sample_mixers_kernel, bblk=bblk),
        grid=(BS // bblk,),
        in_specs=[_fixed((BS, D_K), (0, Z_LA // D_K)),
                  pl.BlockSpec((bblk, Z_SG), lambda s: (s, 0)),
                  pl.BlockSpec((bblk, D_A), lambda s: (s, Z_SG // D_A)),
                  pl.BlockSpec((bblk, N_HEADS, HEAD_K, HEAD_V), lambda s: (s, 0, 0, 0)),
                  _fixed((1, HEAD_V)),
                  pl.BlockSpec((CONV_W - 1, bblk, d_conv), lambda s: (0, s, 0)),
                  pl.BlockSpec((bblk, d_conv), lambda s: (s, Z_GLU // d_conv)),
                  _fixed((CONV_W, nslab, LANES)), _fixed((nslab, LANES)), _fixed((1, d_conv)), _fixed((1, d_conv))],
        out_specs=(pl.BlockSpec((bblk, D_A), lambda s: (s, 0)),
                   pl.BlockSpec((1, bblk, N_HEADS, HEAD_K, HEAD_V), lambda s: (0, s, 0, 0, 0)),
                   pl.BlockSpec((bblk, d_conv), lambda s: (s, 0)),
                   pl.BlockSpec((CONV_W - 1, bblk, d_conv), lambda s: (0, s, 0))),
        out_shape=(jax.ShapeDtypeStruct((BS, D_A), F32),
                   jax.ShapeDtypeStruct((1, BS, N_HEADS, HEAD_K, HEAD_V), F32),
                   jax.ShapeDtypeStruct((BS, d_conv), F32),
                   jax.ShapeDtypeStruct((CONV_W - 1, BS, d_conv), F32)),
        scratch_shapes=[pltpu.VMEM((N_HEADS, HEAD_K, BS), F32)],
        compiler_params=_cparams(("arbitrary",)),
        name="sample_mixers",
    )(z_s, z_s, z_s, state_gla[0], gn,
      sc_t, z_s, cw_slabs, cb_slabs, vec(conv_ln_g[0]), vec(conv_ln_b[0]))
    conv_new_s = jnp.transpose(conv_new_s_t, (1, 0, 2))[None]

    tt = 512
    hist = 32
    nsteps = T // tt
    c_p, conv_new_p = pl.pallas_call(
        functools.partial(_conv_prompt_kernel, tt=tt, nsteps=nsteps, hist=hist),
        grid=(B, nsteps),
        in_specs=[pl.BlockSpec((tt, d_conv), lambda b, t: (b * nsteps + t, Z_GLU // d_conv)),
                  _fixed((CONV_W, nslab, LANES)), _fixed((nslab, LANES)),
                  _fixed((1, d_conv)), _fixed((1, d_conv))],
        out_specs=(pl.BlockSpec((tt, d_conv), lambda b, t: (b * nsteps + t, 0)),
                   pl.BlockSpec((1, 1, CONV_W - 1, d_conv), lambda b, t: (0, b, 0, 0))),
        out_shape=(jax.ShapeDtypeStruct((MP, d_conv), BF16),
                   jax.ShapeDtypeStruct((1, B, CONV_W - 1, d_conv), F32)),
        scratch_shapes=[pltpu.VMEM((nslab, hist + tt, LANES), F32), pltpu.VMEM((nslab, tt, LANES), F32)],
        compiler_params=_cparams(("arbitrary", "arbitrary")),
        name="conv_prompt",
    )(z_p, cw_slabs, cb_slabs, vec(conv_ln_g[0]), vec(conv_ln_b[0]))

    zbo = min(1024, D)
    wcols = min(2 * MXU_N, D)
    nwo = D // wcols
    njo = D // zbo
    assert nwo * wcols == D and njo * zbo == D
    row_o = lambda s: jnp.maximum(s - nwo, 0) // njo
    col_o = lambda s: lax.rem(jnp.maximum(s - nwo, 0), njo)
    x1_p, x1_s = pl.pallas_call(
        functools.partial(_outproj_kernel, nw=nwo, nj=njo, tmp=tmp, tms=tms),
        grid=(nwo + n_tiles * njo,),
        in_specs=[pl.BlockSpec((D_A + d_conv, wcols), lambda s: (0, jnp.minimum(s, nwo - 1))),
                  pl.BlockSpec((tmp, zbo), lambda s: (row_o(s), col_o(s))),
                  pl.BlockSpec((tms, 1, zbo), lambda s: (row_o(s), 0, col_o(s))),
                  pl.BlockSpec((tmp, D_A), lambda s: (row_o(s), 0)),
                  pl.BlockSpec((tms, D_A), lambda s: (row_o(s), 0)),
                  pl.BlockSpec((tmp, d_conv), lambda s: (row_o(s), 0)),
                  pl.BlockSpec((tms, d_conv), lambda s: (row_o(s), 0))],
        out_specs=(pl.BlockSpec((tmp, zbo), lambda s: (row_o(s), col_o(s))),
                   pl.BlockSpec((tms, zbo), lambda s: (row_o(s), col_o(s)))),
        out_shape=pair_shapes(D),
        scratch_shapes=[pltpu.VMEM((D // MXU_N, D_A + d_conv, MXU_N), BF16),
                        pltpu.VMEM((tm, D_A), BF16), pltpu.VMEM((tm, d_conv), BF16)],
        compiler_params=_cparams(("arbitrary",)),
        name="out_proj",
    )(w_out[0], xp, x_sample, oa_p, oa_s, c_p, c_s)

    tf = 256
    nf = d_ff // tf
    assert nf * tf == d_ff
    y_p, y_s = pl.pallas_call(
        functools.partial(_ffn_kernel, tmp=tmp, tms=tms, nf=nf, n_tiles=n_tiles),
        grid=(n_tiles, nf),
        in_specs=[pl.BlockSpec(memory_space=pl.ANY), pl.BlockSpec((tms, D), lambda i, j: (i, 0)),
                  _fixed((1, D)), _fixed((1, D)),
                  pl.BlockSpec((D, tf), lambda i, j: (0, j)),
                  pl.BlockSpec((D, tf), lambda i, j: (0, j + nf)),
                  pl.BlockSpec((tf, D), lambda i, j: (j, 0))],
        out_specs=(pl.BlockSpec((tmp, D), lambda i, j: (i, 0)),
                   pl.BlockSpec((tms, 1, D), lambda i, j: (i, 0, 0))),
        out_shape=(jax.ShapeDtypeStruct((MP, D), F32), jax.ShapeDtypeStruct((BS, 1, D), F32)),
        scratch_shapes=[pltpu.VMEM((tm, D), BF16), pltpu.VMEM((tms, D), F32),
                        pltpu.VMEM((2, tmp, D), F32), pltpu.SemaphoreType.DMA((2,))],
        compiler_params=_cparams(("arbitrary", "arbitrary")),
        name="ffn",
    )(x1_p, x1_s, norm_ffn[0].reshape(1, D), norm_final.reshape(1, D), w_ffn_in[0], w_ffn_in[0], w_ffn_out[0])

    return (y_p.reshape(B, T, D), y_s, gla_new_p, conv_new_p, gla_new_s, conv_new_s)
```
